```python
import jax, jax.numpy as jnp
from jax import lax
import numpy as np

D_MODEL = 2048
BATCH = 1
SEQ = 8192
DEPTH = 4

GRID_W = 64
MLA_HEADS = 8
Q_LORA = 512
KV_LORA = 512
QK_NOPE = 128
QK_ROPE = 64
V_HEAD = 128
ROPE_THETA = 10000.0
Q_BLOCK = 128
NA_HEADS = 8
NA_HEAD_DIM = 128
NA_KH = 8
NA_KW = 16
D_FF = 4 * D_MODEL
EPS = 1e-6

MLA_W = MLA_HEADS * V_HEAD
NA_W = NA_HEADS * NA_HEAD_DIM
IN_WIDTHS = (Q_LORA, KV_LORA, QK_ROPE, NA_W, NA_W, NA_W, D_MODEL, D_MODEL)
IN_SPLITS = tuple(int(v) for v in np.cumsum(IN_WIDTHS)[:-1])
IN_TOTAL = int(sum(IN_WIDTHS))

kernel_name = "hybrid_mla_natten_sqrelu_encoder"


def rmsnorm(x, g):
    xf = x.astype(jnp.float32)
    y = xf * lax.rsqrt(jnp.mean(xf * xf, axis=-1, keepdims=True) + EPS)
    return (y * g.astype(jnp.float32)).astype(x.dtype)


def rope(x, cos, sin):
    x1, x2 = jnp.split(x, 2, axis=-1)
    return jnp.concatenate([x1 * cos - x2 * sin, x2 * cos + x1 * sin], axis=-1)


def mla_attention(q_nope, q_pe, k_nope, k_pe, v):
    B, S, H, _ = q_nope.shape
    nblk = S // Q_BLOCK
    scale = (QK_NOPE + QK_ROPE) ** -0.5
    qn_b = q_nope.reshape(B, nblk, Q_BLOCK, H, QK_NOPE).transpose(1, 0, 2, 3, 4)
    qp_b = q_pe.reshape(B, nblk, Q_BLOCK, H, QK_ROPE).transpose(1, 0, 2, 3, 4)

    def block(args):
        qn, qp = args
        s = (jnp.einsum('bqhd,bkhd->bhqk', qn, k_nope)
             + jnp.einsum('bqhd,bkd->bhqk', qp, k_pe)).astype(jnp.float32) * scale
        p = jax.nn.softmax(s, axis=-1).astype(v.dtype)
        return jnp.einsum('bhqk,bkhd->bqhd', p, v)

    out = lax.map(block, (qn_b, qp_b))
    return out.transpose(1, 0, 2, 3, 4).reshape(B, S, H * V_HEAD)


def neighbourhood_attention(q, k, v, rpb):
    B, S, H, d = q.shape
    rows = S // GRID_W
    kh = min(NA_KH, rows)
    r = jnp.arange(rows)
    row_start = jnp.clip(r - kh // 2, 0, rows - kh)
    row_idx = row_start[:, None] + jnp.arange(kh)[None, :]
    c = jnp.arange(GRID_W)
    col_start = jnp.clip(c - NA_KW // 2, 0, GRID_W - NA_KW)
    col_ok = (c[None, :] >= col_start[:, None]) & (c[None, :] < col_start[:, None] + NA_KW)

    qg = q.reshape(B, rows, GRID_W, H, d)
    kg = k.reshape(B, rows, GRID_W, H, d)[:, row_idx]
    vg = v.reshape(B, rows, GRID_W, H, d)[:, row_idx]
    s = jnp.einsum('brqhd,brikhd->brhqik', qg, kg).astype(jnp.float32) * (d ** -0.5)

    dy = row_idx - r[:, None] + (NA_KH - 1)
    dx = jnp.clip(c[None, :] - c[:, None], -(NA_KW - 1), NA_KW - 1) + (NA_KW - 1)
    bias = rpb.astype(jnp.float32)[:, dy][..., dx]
    bias = bias.transpose(1, 0, 3, 2, 4)
    s = jnp.where(col_ok[:, None, :], s + bias[None], -jnp.inf)
    p = jax.nn.softmax(s.reshape(B, rows, H, GRID_W, kh * GRID_W), axis=-1)
    p = p.reshape(B, rows, H, GRID_W, kh, GRID_W).astype(v.dtype)
    out = jnp.einsum('brhqik,brikhd->brqhd', p, vg)
    return out.reshape(B, S, H * d)


def setup_inputs(seed: int = 0) -> dict:
    key = jax.random.key(seed)
    ks = jax.random.split(key, 16)
    f32 = jnp.float32

    def w(k, shape, fan_in):
        return jax.random.normal(k, shape, f32) * fan_in ** -0.5

    def gain(k, shape):
        return 1.0 + 0.01 * jax.random.normal(k, shape, f32)

    return {
        "x": jax.random.normal(ks[0], (BATCH, SEQ, D_MODEL), f32),
        "norm_mix": gain(ks[1], (DEPTH, D_MODEL)),
        "w_in": w(ks[2], (DEPTH, D_MODEL, IN_TOTAL), D_MODEL),
        "norm_qa": gain(ks[3], (DEPTH, Q_LORA)),
        "w_uq": w(ks[4], (DEPTH, Q_LORA, MLA_HEADS * (QK_NOPE + QK_ROPE)), Q_LORA),
        "norm_kva": gain(ks[5], (DEPTH, KV_LORA)),
        "w_ukv": w(ks[6], (DEPTH, KV_LORA, MLA_HEADS * (QK_NOPE + V_HEAD)), KV_LORA),
        "rpb": 0.02 * jax.random.normal(ks[7], (DEPTH, NA_HEADS, 2 * NA_KH - 1, 2 * NA_KW - 1), f32),
        "w_o_mla": w(ks[8], (DEPTH, MLA_W, D_MODEL), MLA_W),
        "w_o_na": w(ks[9], (DEPTH, NA_W, D_MODEL), NA_W),
        "w_out": w(ks[10], (DEPTH, D_MODEL, D_MODEL), D_MODEL),
        "norm_mlp": gain(ks[11], (DEPTH, D_MODEL)),
        "w_ff1": w(ks[12], (DEPTH, D_MODEL, D_FF), D_MODEL),
        "w_ff2": w(ks[13], (DEPTH, D_FF, D_MODEL), D_FF),
        "norm_final": gain(ks[14], (D_MODEL,)),
    }


def reference(x, norm_mix, w_in, norm_qa, w_uq, norm_kva, w_ukv, rpb, w_o_mla, w_o_na,
              w_out, norm_mlp, w_ff1, w_ff2, norm_final):
    B, S, _ = x.shape
    pos = jnp.arange(S, dtype=jnp.float32)
    inv_freq = 1.0 / (ROPE_THETA ** (jnp.arange(0, QK_ROPE, 2, dtype=jnp.float32) / QK_ROPE))
    ang = pos[:, None] * inv_freq[None, :]
    cos = jnp.cos(ang).astype(x.dtype)
    sin = jnp.sin(ang).astype(x.dtype)

    for l in range(DEPTH):
        u = rmsnorm(x, norm_mix[l])
        proj = u @ w_in[l]
        c_q, c_kv, k_pe, q_na, k_na, v_na, gate_a, gate_b = jnp.split(proj, IN_SPLITS, axis=-1)

        q = (rmsnorm(c_q, norm_qa[l]) @ w_uq[l]).reshape(B, S, MLA_HEADS, QK_NOPE + QK_ROPE)
        kv = (rmsnorm(c_kv, norm_kva[l]) @ w_ukv[l]).reshape(B, S, MLA_HEADS, QK_NOPE + V_HEAD)
        q_nope, q_pe = q[..., :QK_NOPE], q[..., QK_NOPE:]
        k_nope, v = kv[..., :QK_NOPE], kv[..., QK_NOPE:]
        q_pe = rope(q_pe, cos[:, None, :], sin[:, None, :])
        k_pe = rope(k_pe, cos, sin)
        y_a = mla_attention(q_nope, q_pe, k_nope, k_pe, v) @ w_o_mla[l]

        hs = (B, S, NA_HEADS, NA_HEAD_DIM)
        y_b = neighbourhood_attention(q_na.reshape(hs), k_na.reshape(hs), v_na.reshape(hs), rpb[l]) @ w_o_na[l]

        merged = jax.nn.sigmoid(gate_a) * y_a + jax.nn.sigmoid(gate_b) * y_b
        x = x + merged @ w_out[l]

        h = rmsnorm(x, norm_mlp[l]) @ w_ff1[l]
        x = x + jnp.square(jax.nn.relu(h)) @ w_ff2[l]

    return rmsnorm(x, norm_final)
```

```python
import functools

import jax
import jax.numpy as jnp
import numpy as np
from jax import lax
from jax.experimental import pallas as pl
from jax.experimental.pallas import tpu as pltpu

D_MODEL = 2048
SEQ = 8192
GRID_W = 64
ROWS = SEQ // GRID_W
MLA_HEADS = 8
Q_LORA = 512
KV_LORA = 512
QK_NOPE = 128
QK_ROPE = 64
V_HEAD = 128
ROPE_THETA = 10000.0
NA_HEADS = 8
NA_HEAD_DIM = 128
NA_KH = 8
NA_KW = 16
D_FF = 4 * D_MODEL
EPS = 1e-6

LANES = 128
MLA_QK_PAD = 2 * LANES
NA_Q_ROWS = 8
NA_WIN_ROWS = 2 * NA_KH
NA_TQ = NA_Q_ROWS * GRID_W
NA_TK = NA_WIN_ROWS * GRID_W
NA_STRIP = NA_KH * GRID_W
NA_STRIP_PAD = NA_STRIP + LANES
VMEM_LIMIT = 56 * 1024 * 1024

BF16 = jnp.bfloat16
F32 = jnp.float32


def _params(n_axes):
    return pltpu.CompilerParams(
        dimension_semantics=("arbitrary",) * n_axes, vmem_limit_bytes=VMEM_LIMIT)


def _rms(x, g):
    return x * lax.rsqrt(jnp.mean(x * x, axis=-1, keepdims=True) + EPS) * g


def _rmsnorm_kernel(x_ref, g_ref, o_ref):
    o_ref[...] = _rms(x_ref[...], g_ref[...]).astype(o_ref.dtype)


def _rmsnorm(x, g, out_dtype, tm=512):
    s, d = x.shape
    return pl.pallas_call(
        _rmsnorm_kernel,
        grid=(s // tm,),
        in_specs=[pl.BlockSpec((tm, d), lambda i: (i, 0)),
                  pl.BlockSpec((1, d), lambda i: (0, 0))],
        out_specs=pl.BlockSpec((tm, d), lambda i: (i, 0)),
        out_shape=jax.ShapeDtypeStruct((s, d), out_dtype),
        compiler_params=_params(1),
        name="rmsnorm",
    )(x, g.reshape(1, d))


def _mm_kernel(a_ref, w_ref, o_ref):
    o_ref[...] = jnp.dot(a_ref[...], w_ref[...], preferred_element_type=F32).astype(o_ref.dtype)


def _mm_res_kernel(a_ref, w_ref, r_ref, o_ref):
    o_ref[...] = r_ref[...] + jnp.dot(a_ref[...], w_ref[...], preferred_element_type=F32)


def _mm(a, w, out_dtype, tm, tn, residual=None, name="mm"):
    m, k = a.shape
    n = w.shape[1]
    in_specs = [pl.BlockSpec((tm, k), lambda i, j: (i, 0)),
                pl.BlockSpec((k, tn), lambda i, j: (0, j))]
    args = [a, w]
    body = _mm_kernel
    if residual is not None:
        in_specs.append(pl.BlockSpec((tm, tn), lambda i, j: (i, j)))
        args.append(residual)
        body = _mm_res_kernel
    return pl.pallas_call(
        body,
        grid=(m // tm, n // tn),
        in_specs=in_specs,
        out_specs=pl.BlockSpec((tm, tn), lambda i, j: (i, j)),
        out_shape=jax.ShapeDtypeStruct((m, n), out_dtype),
        compiler_params=_params(2),
        name=name,
    )(*args)


def _rope128(x, c, sl, sr):
    return x * c + pltpu.roll(x, LANES - QK_ROPE // 2, 1) * sl + pltpu.roll(x, QK_ROPE // 2, 1) * sr


def _mla_up_kernel(c_ref, gq_ref, gkv_ref, wqn_ref, wqp_ref, wkn_ref, wv_ref,
                   cos_ref, sl_ref, sr_ref, q_ref, k_ref, v_ref):
    scale = (QK_NOPE + QK_ROPE) ** -0.5
    cq = _rms(c_ref[:, :Q_LORA], gq_ref[...]).astype(BF16)
    ckv = _rms(c_ref[:, Q_LORA:Q_LORA + KV_LORA], gkv_ref[...]).astype(BF16)
    kpe = c_ref[:, Q_LORA + KV_LORA:]
    cos, sl, sr = cos_ref[...], sl_ref[...], sr_ref[...]
    kpe_rot = _rope128(kpe, cos, sl, sr).astype(BF16)
    qn = jnp.dot(cq, wqn_ref[...], preferred_element_type=F32) * scale
    qp = jnp.dot(cq, wqp_ref[...], preferred_element_type=F32)
    kn = jnp.dot(ckv, wkn_ref[...], preferred_element_type=F32)
    v_ref[...] = jnp.dot(ckv, wv_ref[...], preferred_element_type=F32).astype(BF16)
    for h in range(MLA_HEADS):
        lo = h * MLA_QK_PAD
        hs = slice(h * LANES, (h + 1) * LANES)
        q_ref[:, lo:lo + LANES] = qn[:, hs].astype(BF16)
        q_ref[:, lo + LANES:lo + 2 * LANES] = (_rope128(qp[:, hs], cos, sl, sr) * scale).astype(BF16)
        k_ref[:, lo:lo + LANES] = kn[:, hs].astype(BF16)
        k_ref[:, lo + LANES:lo + 2 * LANES] = kpe_rot


def _mla_up(c, gq, gkv, wqn, wqp, wkn, wv, cos_t, sl_t, sr_t, tm=512):
    s = c.shape[0]
    row = lambda w: pl.BlockSpec((tm, w), lambda i: (i, 0))
    full = lambda a: pl.BlockSpec(a.shape, lambda i: (0, 0))
    hw = MLA_HEADS * LANES
    return pl.pallas_call(
        _mla_up_kernel,
        grid=(s // tm,),
        in_specs=[row(c.shape[1]), full(gq), full(gkv), full(wqn), full(wqp), full(wkn), full(wv),
                  row(LANES), row(LANES), row(LANES)],
        out_specs=[row(MLA_HEADS * MLA_QK_PAD), row(MLA_HEADS * MLA_QK_PAD), row(hw)],
        out_shape=[jax.ShapeDtypeStruct((s, MLA_HEADS * MLA_QK_PAD), BF16),
                   jax.ShapeDtypeStruct((s, MLA_HEADS * MLA_QK_PAD), BF16),
                   jax.ShapeDtypeStruct((s, hw), BF16)],
        compiler_params=_params(1),
        name="mla_up",
    )(c, gq, gkv, wqn, wqp, wkn, wv, cos_t, sl_t, sr_t)


def _mla_attn_kernel(q_ref, k_ref, v_ref, o_ref, m_sc, l_sc, acc_sc, *, tk):
    q = q_ref[...]
    m_sc[...] = jnp.full(m_sc.shape, -jnp.inf, F32)
    l_sc[...] = jnp.zeros(l_sc.shape, F32)
    acc_sc[...] = jnp.zeros(acc_sc.shape, F32)

    def body(c, carry):
        start = pl.multiple_of(c * tk, tk)
        k = k_ref[pl.ds(start, tk), :]
        v = v_ref[pl.ds(start, tk), :]
        s = lax.dot_general(q, k, (((1,), (1,)), ((), ())), preferred_element_type=F32)
        m_prev = m_sc[...]
        m_new = jnp.maximum(m_prev, jnp.max(s, axis=1, keepdims=True))
        alpha = jnp.exp(m_prev - m_new)
        p = jnp.exp(s - m_new)
        l_sc[...] = alpha * l_sc[...] + jnp.sum(p, axis=1, keepdims=True)
        acc_sc[...] = alpha * acc_sc[...] + jnp.dot(p.astype(BF16), v, preferred_element_type=F32)
        m_sc[...] = m_new
        return carry

    lax.fori_loop(0, k_ref.shape[0] // tk, body, 0)
    o_ref[...] = (acc_sc[...] / l_sc[...]).astype(o_ref.dtype)


def _mla_attn(q, k, v, tq=512, tk=1024):
    s = q.shape[0]
    return pl.pallas_call(
        functools.partial(_mla_attn_kernel, tk=tk),
        grid=(MLA_HEADS, s // tq),
        in_specs=[pl.BlockSpec((tq, MLA_QK_PAD), lambda h, i: (i, h)),
                  pl.BlockSpec((s, MLA_QK_PAD), lambda h, i: (0, h)),
                  pl.BlockSpec((s, V_HEAD), lambda h, i: (0, h))],
        out_specs=pl.BlockSpec((tq, V_HEAD), lambda h, i: (i, h)),
        out_shape=jax.ShapeDtypeStruct((s, MLA_HEADS * V_HEAD), BF16),
        scratch_shapes=[pltpu.VMEM((tq, 1), F32), pltpu.VMEM((tq, 1), F32),
                        pltpu.VMEM((tq, V_HEAD), F32)],
        compiler_params=_params(2),
        name="mla_attn",
    )(q, k, v)


def _na_row_start(r):
    return min(max(r - NA_KH // 2, 0), ROWS - NA_KH)


def _na_win_start(blk):
    return min(max(blk * NA_Q_ROWS - NA_KH // 2, 0), ROWS - NA_WIN_ROWS)


def _na_strip_tables(rpb_l):
    c = np.arange(GRID_W)
    col_start = np.clip(c - NA_KW // 2, 0, GRID_W - NA_KW)
    col_ok = (c[None, :] >= col_start[:, None]) & (c[None, :] < col_start[:, None] + NA_KW)
    dx = np.clip(c[None, :] - c[:, None], -(NA_KW - 1), NA_KW - 1) + (NA_KW - 1)
    dy = np.arange(NA_KH)[:, None] + np.arange(NA_KH)[None, :]
    t = rpb_l.astype(F32)[:, dy[:, :, None, None], dx[None, None, :, :]]
    t = jnp.where(col_ok[None, None, None], t, -jnp.inf)
    t = t.transpose(0, 1, 3, 2, 4).reshape(NA_HEADS, NA_KH, GRID_W, NA_STRIP)
    pad = lambda lo, hi: jnp.pad(t, ((0, 0), (0, 0), (0, 0), (lo, hi)), constant_values=-jnp.inf)
    return jnp.stack([pad(0, LANES), pad(GRID_W, LANES - GRID_W)], axis=2)


def _na_attn_kernel(q_ref, k_ref, v_ref, t_ref, o_ref, bias_sc):
    blk = pl.program_id(1)
    nblk = pl.num_programs(1)

    @pl.when(blk == 0)
    def _build_bias():
        bias_sc[...] = jnp.full(bias_sc.shape, -jnp.inf, F32)
        for var, b in enumerate((0, 1, ROWS // NA_Q_ROWS - 1)):
            for ri in range(NA_Q_ROWS):
                r = b * NA_Q_ROWS + ri
                off = _na_row_start(r) - _na_win_start(b)
                dy0 = _na_row_start(r) - r + (NA_KH - 1)
                left = LANES * (off // 2)
                width = min(NA_STRIP_PAD, NA_TK - left)
                bias_sc[var, ri * GRID_W:(ri + 1) * GRID_W, left:left + width] = (
                    t_ref[0, dy0, off % 2, :, :width])

    var = jnp.where(blk == 0, 0, jnp.where(blk == nblk - 1, 2, 1))
    start = jnp.clip(blk * NA_TQ - (NA_KH // 2) * GRID_W, 0, SEQ - NA_TK)
    start = pl.multiple_of(start, (NA_KH // 2) * GRID_W)
    k = k_ref[pl.ds(start, NA_TK), :]
    v = v_ref[pl.ds(start, NA_TK), :]
    s = lax.dot_general(q_ref[...], k, (((1,), (1,)), ((), ())), preferred_element_type=F32)
    s = s * (NA_HEAD_DIM ** -0.5) + bias_sc[var]
    m = jnp.max(s, axis=1, keepdims=True)
    p = jnp.exp(s - m)
    l = jnp.sum(p, axis=1, keepdims=True)
    o = jnp.dot(p.astype(BF16), v, preferred_element_type=F32)
    o_ref[...] = (o / l).astype(o_ref.dtype)


def _na_attn(qkv, strips):
    s = qkv.shape[0]
    return pl.pallas_call(
        _na_attn_kernel,
        grid=(NA_HEADS, s // NA_TQ),
        in_specs=[pl.BlockSpec((NA_TQ, NA_HEAD_DIM), lambda h, b: (b, h)),
                  pl.BlockSpec((s, NA_HEAD_DIM), lambda h, b: (0, NA_HEADS + h)),
                  pl.BlockSpec((s, NA_HEAD_DIM), lambda h, b: (0, 2 * NA_HEADS + h)),
                  pl.BlockSpec((1, NA_KH, 2, GRID_W, NA_STRIP_PAD), lambda h, b: (h, 0, 0, 0, 0))],
        out_specs=pl.BlockSpec((NA_TQ, NA_HEAD_DIM), lambda h, b: (b, h)),
        out_shape=jax.ShapeDtypeStruct((s, NA_HEADS * NA_HEAD_DIM), BF16),
        scratch_shapes=[pltpu.VMEM((3, NA_TQ, NA_TK), F32)],
        compiler_params=_params(2),
        name="na_attn",
    )(qkv, qkv, qkv, strips)


def _merge_kernel(ya_ref, yb_ref, g_ref, woa_ref, wob_ref, o_ref):
    a = jnp.dot(ya_ref[...], woa_ref[...], preferred_element_type=F32)
    b = jnp.dot(yb_ref[...], wob_ref[...], preferred_element_type=F32)
    ga = jax.nn.sigmoid(g_ref[:, :D_MODEL])
    gb = jax.nn.sigmoid(g_ref[:, D_MODEL:])
    o_ref[...] = (ga * a + gb * b).astype(o_ref.dtype)


def _merge(ya, yb, gates, woa, wob, tm=256):
    s = ya.shape[0]
    row = lambda w: pl.BlockSpec((tm, w), lambda i: (i, 0))
    full = lambda a: pl.BlockSpec(a.shape, lambda i: (0, 0))
    return pl.pallas_call(
        _merge_kernel,
        grid=(s // tm,),
        in_specs=[row(ya.shape[1]), row(yb.shape[1]), row(gates.shape[1]), full(woa), full(wob)],
        out_specs=row(D_MODEL),
        out_shape=jax.ShapeDtypeStruct((s, D_MODEL), BF16),
        compiler_params=_params(1),
        name="merge",
    )(ya, yb, gates, woa, wob)


def _ffn_kernel(x_ref, g_ref, w1_ref, w2_ref, o_ref, u_sc):
    f = pl.program_id(1)

    @pl.when(f == 0)
    def _init():
        x = x_ref[...]
        u_sc[...] = _rms(x, g_ref[...]).astype(BF16)
        o_ref[...] = x

    h = jnp.dot(u_sc[...], w1_ref[...], preferred_element_type=F32)
    a = jnp.square(jnp.maximum(h, 0.0)).astype(BF16)
    o_ref[...] += jnp.dot(a, w2_ref[...], preferred_element_type=F32)


def _ffn(x, g, w1, w2, tm=512, tf=1024):
    s, d = x.shape
    dff = w1.shape[1]
    return pl.pallas_call(
        _ffn_kernel,
        grid=(s // tm, dff // tf),
        in_specs=[pl.BlockSpec((tm, d), lambda i, f: (i, 0)),
                  pl.BlockSpec((1, d), lambda i, f: (0, 0)),
                  pl.BlockSpec((d, tf), lambda i, f: (0, f)),
                  pl.BlockSpec((tf, d), lambda i, f: (f, 0))],
        out_specs=pl.BlockSpec((tm, d), lambda i, f: (i, 0)),
        out_shape=jax.ShapeDtypeStruct((s, d), F32),
        scratch_shapes=[pltpu.VMEM((tm, d), BF16)],
        compiler_params=_params(2),
        name="ffn",
    )(x, g.reshape(1, d), w1, w2)


def _rope_tables(s):
    pos = jnp.arange(s, dtype=F32)
    inv_freq = 1.0 / (ROPE_THETA ** (jnp.arange(0, QK_ROPE, 2, dtype=F32) / QK_ROPE))
    ang = pos[:, None] * inv_freq[None, :]
    cos, sin = jnp.cos(ang), jnp.sin(ang)
    half = QK_ROPE // 2
    z = lambda w: jnp.zeros((s, w), F32)
    cos_t = jnp.concatenate([cos, cos, z(LANES - QK_ROPE)], axis=1)
    sl_t = jnp.concatenate([-sin, z(LANES - half)], axis=1)
    sr_t = jnp.concatenate([z(half), sin, z(LANES - QK_ROPE)], axis=1)
    return cos_t, sl_t, sr_t


def kernel(x, norm_mix, w_in, norm_qa, w_uq, norm_kva, w_ukv, rpb, w_o_mla, w_o_na, w_out,
           norm_mlp, w_ff1, w_ff2, norm_final):
    b, s, d = x.shape
    assert (b, s, d) == (1, SEQ, D_MODEL)
    depth = w_in.shape[0]
    cos_t, sl_t, sr_t = _rope_tables(s)

    c_end = Q_LORA + KV_LORA + QK_ROPE
    na_w = NA_HEADS * NA_HEAD_DIM
    w_c = jnp.pad(w_in[:, :, :c_end], ((0, 0), (0, 0), (0, LANES - QK_ROPE))).astype(BF16)
    w_na = w_in[:, :, c_end:c_end + 3 * na_w].astype(BF16)
    w_g = w_in[:, :, c_end + 3 * na_w:].astype(BF16)
    uq = w_uq.reshape(depth, Q_LORA, MLA_HEADS, QK_NOPE + QK_ROPE)
    w_qn = uq[..., :QK_NOPE].reshape(depth, Q_LORA, MLA_HEADS * QK_NOPE).astype(BF16)
    w_qp = jnp.pad(uq[..., QK_NOPE:], ((0, 0), (0, 0), (0, 0), (0, LANES - QK_ROPE)))
    w_qp = w_qp.reshape(depth, Q_LORA, MLA_HEADS * LANES).astype(BF16)
    ukv = w_ukv.reshape(depth, KV_LORA, MLA_HEADS, QK_NOPE + V_HEAD)
    w_kn = ukv[..., :QK_NOPE].reshape(depth, KV_LORA, MLA_HEADS * QK_NOPE).astype(BF16)
    w_v = ukv[..., QK_NOPE:].reshape(depth, KV_LORA, MLA_HEADS * V_HEAD).astype(BF16)
    w_oa = w_o_mla.astype(BF16)
    w_ob = w_o_na.astype(BF16)
    w_o = w_out.astype(BF16)
    w_1 = w_ff1.astype(BF16)
    w_2 = w_ff2.astype(BF16)

    xs = x.reshape(s, d)
    for l in range(depth):
        u = _rmsnorm(xs, norm_mix[l], BF16)
        c = _mm(u, w_c[l], F32, 1024, w_c.shape[2], name="proj_c")
        qkv_na = _mm(u, w_na[l], BF16, 1024, 1024, name="proj_na")
        gates = _mm(u, w_g[l], F32, 1024, 1024, name="proj_gate")
        q, k, v = _mla_up(c, norm_qa[l].reshape(1, -1), norm_kva[l].reshape(1, -1),
                          w_qn[l], w_qp[l], w_kn[l], w_v[l], cos_t, sl_t, sr_t)
        y_a = _mla_attn(q, k, v)
        y_b = _na_attn(qkv_na, _na_strip_tables(rpb[l]))
        merged = _merge(y_a, y_b, gates, w_oa[l], w_ob[l])
        xs = _mm(merged, w_o[l], F32, 1024, 1024, residual=xs, name="out_proj")
        xs = _ffn(xs, norm_mlp[l], w_1[l], w_2[l])
    return _rmsnorm(xs, norm_final, F32).reshape(b, s, d)
```

```python
import functools

import jax
import jax.numpy as jnp
import numpy as np
from jax import lax
from jax.experimental import pallas as pl
from jax.experimental.pallas import tpu as pltpu

D_MODEL = 2048
SEQ = 8192
GRID_W = 64
ROWS = SEQ // GRID_W
MLA_HEADS = 8
Q_LORA = 512
KV_LORA = 512
QK_NOPE = 128
QK_ROPE = 64
V_HEAD = 128
ROPE_THETA = 10000.0
NA_HEADS = 8
NA_HEAD_DIM = 128
NA_KH = 8
NA_KW = 16
D_FF = 4 * D_MODEL
EPS = 1e-6
LOG2E = float(np.log2(np.e))

LANES = 128
MLA_QK_PAD = 2 * LANES
NA_Q_ROWS = 8
NA_WIN_ROWS = 2 * NA_KH
NA_TQ = NA_Q_ROWS * GRID_W
NA_TK = NA_WIN_ROWS * GRID_W
NA_STRIP = NA_KH * GRID_W
NA_STRIP_PAD = NA_STRIP + LANES
VMEM_LIMIT = 56 * 1024 * 1024

BF16 = jnp.bfloat16
F32 = jnp.float32


def _params(n_axes):
    return pltpu.CompilerParams(
        dimension_semantics=("arbitrary",) * n_axes, vmem_limit_bytes=VMEM_LIMIT)


def _rms(x, g):
    return x * lax.rsqrt(jnp.mean(x * x, axis=-1, keepdims=True) + EPS) * g


def _rmsnorm_kernel(x_ref, g_ref, o_ref):
    o_ref[...] = _rms(x_ref[...], g_ref[...]).astype(o_ref.dtype)


def _rmsnorm(x, g, out_dtype, tm=512):
    s, d = x.shape
    return pl.pallas_call(
        _rmsnorm_kernel,
        grid=(s // tm,),
        in_specs=[pl.BlockSpec((tm, d), lambda i: (i, 0)),
                  pl.BlockSpec((1, d), lambda i: (0, 0))],
        out_specs=pl.BlockSpec((tm, d), lambda i: (i, 0)),
        out_shape=jax.ShapeDtypeStruct((s, d), out_dtype),
        compiler_params=_params(1),
        name="rmsnorm",
    )(x, g.reshape(1, d))


def _mm_kernel(a_ref, w_ref, o_ref):
    o_ref[...] = jnp.dot(a_ref[...], w_ref[...], preferred_element_type=F32).astype(o_ref.dtype)


def _mm_res_kernel(a_ref, w_ref, r_ref, o_ref):
    o_ref[...] = r_ref[...] + jnp.dot(a_ref[...], w_ref[...], preferred_element_type=F32)


def _mm(a, w, out_dtype, tm, tn, residual=None, name="mm"):
    m, k = a.shape
    n = w.shape[1]
    in_specs = [pl.BlockSpec((tm, k), lambda i, j: (i, 0)),
                pl.BlockSpec((k, tn), lambda i, j: (0, j))]
    args = [a, w]
    body = _mm_kernel
    if residual is not None:
        in_specs.append(pl.BlockSpec((tm, tn), lambda i, j: (i, j)))
        args.append(residual)
        body = _mm_res_kernel
    return pl.pallas_call(
        body,
        grid=(m // tm, n // tn),
        in_specs=in_specs,
        out_specs=pl.BlockSpec((tm, tn), lambda i, j: (i, j)),
        out_shape=jax.ShapeDtypeStruct((m, n), out_dtype),
        compiler_params=_params(2),
        name=name,
    )(*args)


def _rope128(x, c, sl, sr):
    return x * c + pltpu.roll(x, LANES - QK_ROPE // 2, 1) * sl + pltpu.roll(x, QK_ROPE // 2, 1) * sr


def _mla_up_kernel(c_ref, gq_ref, gkv_ref, wqn_ref, wqp_ref, wkn_ref, wvt_ref,
                   cos_ref, sl_ref, sr_ref, q_ref, k_ref, vt_ref):
    scale = (QK_NOPE + QK_ROPE) ** -0.5 * LOG2E
    cq = _rms(c_ref[:, :Q_LORA], gq_ref[...]).astype(BF16)
    ckv = _rms(c_ref[:, Q_LORA:Q_LORA + KV_LORA], gkv_ref[...]).astype(BF16)
    kpe = c_ref[:, Q_LORA + KV_LORA:]
    cos, sl, sr = cos_ref[...], sl_ref[...], sr_ref[...]
    kpe_rot = _rope128(kpe, cos, sl, sr).astype(BF16)
    qn = jnp.dot(cq, wqn_ref[...], preferred_element_type=F32) * scale
    qp = jnp.dot(cq, wqp_ref[...], preferred_element_type=F32)
    kn = jnp.dot(ckv, wkn_ref[...], preferred_element_type=F32)
    vt_ref[...] = lax.dot_general(wvt_ref[...], ckv, (((1,), (1,)), ((), ())),
                                  preferred_element_type=F32).astype(BF16)
    for h in range(MLA_HEADS):
        lo = h * MLA_QK_PAD
        hs = slice(h * LANES, (h + 1) * LANES)
        q_ref[:, lo:lo + LANES] = qn[:, hs].astype(BF16)
        q_ref[:, lo + LANES:lo + 2 * LANES] = (_rope128(qp[:, hs], cos, sl, sr) * scale).astype(BF16)
        k_ref[:, lo:lo + LANES] = kn[:, hs].astype(BF16)
        k_ref[:, lo + LANES:lo + 2 * LANES] = kpe_rot


def _mla_up(c, gq, gkv, wqn, wqp, wkn, wvt, cos_t, sl_t, sr_t, tm=512):
    s = c.shape[0]
    row = lambda w: pl.BlockSpec((tm, w), lambda i: (i, 0))
    full = lambda a: pl.BlockSpec(a.shape, lambda i: (0, 0))
    hw = MLA_HEADS * V_HEAD
    return pl.pallas_call(
        _mla_up_kernel,
        grid=(s // tm,),
        in_specs=[row(c.shape[1]), full(gq), full(gkv), full(wqn), full(wqp), full(wkn), full(wvt),
                  row(LANES), row(LANES), row(LANES)],
        out_specs=[row(MLA_HEADS * MLA_QK_PAD), row(MLA_HEADS * MLA_QK_PAD),
                   pl.BlockSpec((hw, tm), lambda i: (0, i))],
        out_shape=[jax.ShapeDtypeStruct((s, MLA_HEADS * MLA_QK_PAD), BF16),
                   jax.ShapeDtypeStruct((s, MLA_HEADS * MLA_QK_PAD), BF16),
                   jax.ShapeDtypeStruct((hw, s), BF16)],
        compiler_params=_params(1),
        name="mla_up",
    )(c, gq, gkv, wqn, wqp, wkn, wvt, cos_t, sl_t, sr_t)


def _mla_attn_kernel(q_ref, k_ref, vt_ref, o_ref, m_sc, l_sc, acc_sc, sa_sc, sb_sc, *, tk):
    q = q_ref[...]
    n_chunks = k_ref.shape[0] // tk
    m_sc[...] = jnp.full(m_sc.shape, -jnp.inf, F32)
    l_sc[...] = jnp.zeros(l_sc.shape, F32)
    acc_sc[...] = jnp.zeros(acc_sc.shape, F32)

    def scores(c):
        k = k_ref[pl.ds(pl.multiple_of(c * tk, tk), tk), :]
        return lax.dot_general(k, q, (((1,), (1,)), ((), ())), preferred_element_type=F32)

    def update(c, st):
        vt = vt_ref[:, pl.ds(pl.multiple_of(c * tk, tk), tk)]
        m_prev = m_sc[...]
        m_new = jnp.maximum(m_prev, jnp.max(st, axis=0, keepdims=True))
        alpha = jnp.exp2(m_prev - m_new)
        pt = jnp.exp2(st - m_new)
        l_sc[...] = alpha * l_sc[...] + jnp.sum(pt, axis=0, keepdims=True)
        acc_sc[...] = alpha * acc_sc[...] + jnp.dot(vt, pt.astype(BF16), preferred_element_type=F32)
        m_sc[...] = m_new

    assert n_chunks % 2 == 0
    sa_sc[...] = scores(0)

    def body(j, carry):
        c = 2 * j
        sb_sc[...] = scores(c + 1)
        update(c, sa_sc[...])
        sa_sc[...] = scores(c + 2)
        update(c + 1, sb_sc[...])
        return carry

    lax.fori_loop(0, n_chunks // 2 - 1, body, 0)
    sb_sc[...] = scores(n_chunks - 1)
    update(n_chunks - 2, sa_sc[...])
    update(n_chunks - 1, sb_sc[...])
    o_ref[...] = (acc_sc[...] / l_sc[...]).T.astype(o_ref.dtype)


def _mla_attn(q, k, vt, tq=512, tk=1024):
    s = q.shape[0]
    return pl.pallas_call(
        functools.partial(_mla_attn_kernel, tk=tk),
        grid=(MLA_HEADS, s // tq),
        in_specs=[pl.BlockSpec((tq, MLA_QK_PAD), lambda h, i: (i, h)),
                  pl.BlockSpec((s, MLA_QK_PAD), lambda h, i: (0, h)),
                  pl.BlockSpec((V_HEAD, s), lambda h, i: (h, 0))],
        out_specs=pl.BlockSpec((tq, V_HEAD), lambda h, i: (i, h)),
        out_shape=jax.ShapeDtypeStruct((s, MLA_HEADS * V_HEAD), BF16),
        scratch_shapes=[pltpu.VMEM((1, tq), F32), pltpu.VMEM((1, tq), F32),
                        pltpu.VMEM((V_HEAD, tq), F32), pltpu.VMEM((tk, tq), F32), pltpu.VMEM((tk, tq), F32)],
        compiler_params=_params(2),
        name="mla_attn",
    )(q, k, vt)


def _na_row_start(r):
    return min(max(r - NA_KH // 2, 0), ROWS - NA_KH)


def _na_win_start(blk):
    return min(max(blk * NA_Q_ROWS - NA_KH // 2, 0), ROWS - NA_WIN_ROWS)


def _na_strip_tables(rpb_l):
    c = np.arange(GRID_W)
    col_start = np.clip(c - NA_KW // 2, 0, GRID_W - NA_KW)
    col_ok = (c[None, :] >= col_start[:, None]) & (c[None, :] < col_start[:, None] + NA_KW)
    dx = np.clip(c[None, :] - c[:, None], -(NA_KW - 1), NA_KW - 1) + (NA_KW - 1)
    rpb_l = rpb_l.astype(F32)
    toe = jnp.full((NA_HEADS, 2 * NA_KH - 1, GRID_W, GRID_W), -jnp.inf, F32)
    for d in range(2 * NA_KW - 1):
        toe = jnp.where((dx == d) & col_ok, rpb_l[:, :, d][:, :, None, None], toe)
    t = jnp.stack([toe[:, dy0:dy0 + NA_KH] for dy0 in range(NA_KH)], axis=1)
    t = t.transpose(0, 1, 3, 2, 4).reshape(NA_HEADS, NA_KH, GRID_W, NA_STRIP)
    pad = lambda lo, hi: jnp.pad(t, ((0, 0), (0, 0), (0, 0), (lo, hi)), constant_values=-jnp.inf)
    return jnp.stack([pad(0, LANES), pad(GRID_W, LANES - GRID_W)], axis=2)


def _na_attn_kernel(q_ref, k_ref, v_ref, t_ref, o_ref, bias_sc):
    blk = pl.program_id(1)
    nblk = pl.num_programs(1)

    @pl.when(blk == 0)
    def _build_bias():
        bias_sc[...] = jnp.full(bias_sc.shape, -jnp.inf, F32)
        for var, b in enumerate((0, 1, ROWS // NA_Q_ROWS - 1)):
            for ri in range(NA_Q_ROWS):
                r = b * NA_Q_ROWS + ri
                off = _na_row_start(r) - _na_win_start(b)
                dy0 = _na_row_start(r) - r + (NA_KH - 1)
                left = LANES * (off // 2)
                width = min(NA_STRIP_PAD, NA_TK - left)
                bias_sc[var, ri * GRID_W:(ri + 1) * GRID_W, left:left + width] = (
                    t_ref[0, dy0, off % 2, :, :width])

    var = jnp.where(blk == 0, 0, jnp.where(blk == nblk - 1, 2, 1))
    start = jnp.clip(blk * NA_TQ - (NA_KH // 2) * GRID_W, 0, SEQ - NA_TK)
    start = pl.multiple_of(start, (NA_KH // 2) * GRID_W)
    k = k_ref[pl.ds(start, NA_TK), :]
    v = v_ref[pl.ds(start, NA_TK), :]
    s = lax.dot_general(q_ref[...], k, (((1,), (1,)), ((), ())), preferred_element_type=F32)
    s = s * (NA_HEAD_DIM ** -0.5) + bias_sc[var]
    m = jnp.max(s, axis=1, keepdims=True)
    p = jnp.exp(s - m)
    l = jnp.sum(p, axis=1, keepdims=True)
    o = jnp.dot(p.astype(BF16), v, preferred_element_type=F32)
    o_ref[...] = (o / l).astype(o_ref.dtype)


def _na_attn(qkv, strips):
    s = qkv.shape[0]
    return pl.pallas_call(
        _na_attn_kernel,
        grid=(NA_HEADS, s // NA_TQ),
        in_specs=[pl.BlockSpec((NA_TQ, NA_HEAD_DIM), lambda h, b: (b, h)),
                  pl.BlockSpec((s, NA_HEAD_DIM), lambda h, b: (0, NA_HEADS + h)),
                  pl.BlockSpec((s, NA_HEAD_DIM), lambda h, b: (0, 2 * NA_HEADS + h)),
                  pl.BlockSpec((1, NA_KH, 2, GRID_W, NA_STRIP_PAD), lambda h, b: (h, 0, 0, 0, 0))],
        out_specs=pl.BlockSpec((NA_TQ, NA_HEAD_DIM), lambda h, b: (b, h)),
        out_shape=jax.ShapeDtypeStruct((s, NA_HEADS * NA_HEAD_DIM), BF16),
        scratch_shapes=[pltpu.VMEM((3, NA_TQ, NA_TK), F32)],
        compiler_params=_params(2),
        name="na_attn",
    )(qkv, qkv, qkv, strips)


def _merge_kernel(ya_ref, yb_ref, g_ref, woa_ref, wob_ref, o_ref):
    a = jnp.dot(ya_ref[...], woa_ref[...], preferred_element_type=F32)
    b = jnp.dot(yb_ref[...], wob_ref[...], preferred_element_type=F32)
    ga = jax.nn.sigmoid(g_ref[:, :D_MODEL])
    gb = jax.nn.sigmoid(g_ref[:, D_MODEL:])
    o_ref[...] = (ga * a + gb * b).astype(o_ref.dtype)


def _merge(ya, yb, gates, woa, wob, tm=256):
    s = ya.shape[0]
    row = lambda w: pl.BlockSpec((tm, w), lambda i: (i, 0))
    full = lambda a: pl.BlockSpec(a.shape, lambda i: (0, 0))
    return pl.pallas_call(
        _merge_kernel,
        grid=(s // tm,),
        in_specs=[row(ya.shape[1]), row(yb.shape[1]), row(gates.shape[1]), full(woa), full(wob)],
        out_specs=row(D_MODEL),
        out_shape=jax.ShapeDtypeStruct((s, D_MODEL), BF16),
        compiler_params=_params(1),
        name="merge",
    )(ya, yb, gates, woa, wob)


def _ffn_kernel(x_ref, g_ref, w1_ref, w2_ref, o_ref, u_sc):
    f = pl.program_id(1)

    @pl.when(f == 0)
    def _init():
        x = x_ref[...]
        u_sc[...] = _rms(x, g_ref[...]).astype(BF16)
        o_ref[...] = x

    h = jnp.dot(u_sc[...], w1_ref[...], preferred_element_type=F32)
    a = jnp.square(jnp.maximum(h, 0.0)).astype(BF16)
    o_ref[...] += jnp.dot(a, w2_ref[...], preferred_element_type=F32)


def _ffn(x, g, w1, w2, tm=512, tf=1024):
    s, d = x.shape
    dff = w1.shape[1]
    return pl.pallas_call(
        _ffn_kernel,
        grid=(s // tm, dff // tf),
        in_specs=[pl.BlockSpec((tm, d), lambda i, f: (i, 0)),
                  pl.BlockSpec((1, d), lambda i, f: (0, 0)),
                  pl.BlockSpec((d, tf), lambda i, f: (0, f)),
                  pl.BlockSpec((tf, d), lambda i, f: (f, 0))],
        out_specs=pl.BlockSpec((tm, d), lambda i, f: (i, 0)),
        out_shape=jax.ShapeDtypeStruct((s, d), F32),
        scratch_shapes=[pltpu.VMEM((tm, d), BF16)],
        compiler_params=_params(2),
        name="ffn",
    )(x, g.reshape(1, d), w1, w2)


def _rope_tables(s):
    pos = jnp.arange(s, dtype=F32)
    inv_freq = 1.0 / (ROPE_THETA ** (jnp.arange(0, QK_ROPE, 2, dtype=F32) / QK_ROPE))
    ang = pos[:, None] * inv_freq[None, :]
    cos, sin = jnp.cos(ang), jnp.sin(ang)
    half = QK_ROPE // 2
    z = lambda w: jnp.zeros((s, w), F32)
    cos_t = jnp.concatenate([cos, cos, z(LANES - QK_ROPE)], axis=1)
    sl_t = jnp.concatenate([-sin, z(LANES - half)], axis=1)
    sr_t = jnp.concatenate([z(half), sin, z(LANES - QK_ROPE)], axis=1)
    return cos_t, sl_t, sr_t


def kernel(x, norm_mix, w_in, norm_qa, w_uq, norm_kva, w_ukv, rpb, w_o_mla, w_o_na, w_out,
           norm_mlp, w_ff1, w_ff2, norm_final):
    b, s, d = x.shape
    assert (b, s, d) == (1, SEQ, D_MODEL)
    depth = w_in.shape[0]
    cos_t, sl_t, sr_t = _rope_tables(s)

    c_end = Q_LORA + KV_LORA + QK_ROPE
    na_w = NA_HEADS * NA_HEAD_DIM
    w_c = jnp.pad(w_in[:, :, :c_end], ((0, 0), (0, 0), (0, LANES - QK_ROPE))).astype(BF16)
    w_na = w_in[:, :, c_end:c_end + 3 * na_w].astype(BF16)
    w_g = w_in[:, :, c_end + 3 * na_w:].astype(BF16)
    uq = w_uq.reshape(depth, Q_LORA, MLA_HEADS, QK_NOPE + QK_ROPE)
    w_qn = uq[..., :QK_NOPE].reshape(depth, Q_LORA, MLA_HEADS * QK_NOPE).astype(BF16)
    w_qp = jnp.pad(uq[..., QK_NOPE:], ((0, 0), (0, 0), (0, 0), (0, LANES - QK_ROPE)))
    w_qp = w_qp.reshape(depth, Q_LORA, MLA_HEADS * LANES).astype(BF16)
    ukv = w_ukv.reshape(depth, KV_LORA, MLA_HEADS, QK_NOPE + V_HEAD)
    w_kn = ukv[..., :QK_NOPE].reshape(depth, KV_LORA, MLA_HEADS * QK_NOPE).astype(BF16)
    w_vt = ukv[..., QK_NOPE:].reshape(depth, KV_LORA, MLA_HEADS * V_HEAD).transpose(0, 2, 1).astype(BF16)
    w_oa = w_o_mla.astype(BF16)
    w_ob = w_o_na.astype(BF16)
    w_o = w_out.astype(BF16)
    w_1 = w_ff1.astype(BF16)
    w_2 = w_ff2.astype(BF16)

    xs = x.reshape(s, d)
    for l in range(depth):
        u = _rmsnorm(xs, norm_mix[l], BF16)
        c = _mm(u, w_c[l], F32, 1024, w_c.shape[2], name="proj_c")
        qkv_na = _mm(u, w_na[l], BF16, 1024, 1024, name="proj_na")
        gates = _mm(u, w_g[l], F32, 1024, 1024, name="proj_gate")
        q, k, v = _mla_up(c, norm_qa[l].reshape(1, -1), norm_kva[l].reshape(1, -1),
                          w_qn[l], w_qp[l], w_kn[l], w_vt[l], cos_t, sl_t, sr_t)
        y_a = _mla_attn(q, k, v)
        y_b = _na_attn(qkv_na, _na_strip_tables(rpb[l]))
        merged = _merge(y_a, y_b, gates, w_oa[l], w_ob[l])
        xs = _mm(merged, w_o[l], F32, 1024, 1024, residual=xs, name="out_proj")
        xs = _ffn(xs, norm_mlp[l], w_1[l], w_2[l])
    return _rmsnorm(xs, norm_final, F32).reshape(b, s, d)
```

```python
import functools

import jax
import jax.numpy as jnp
import numpy as np
from jax import lax
from jax.experimental import pallas as pl
from jax.experimental.pallas import tpu as pltpu

D_MODEL = 2048
SEQ = 8192
GRID_W = 64
ROWS = SEQ // GRID_W
MLA_HEADS = 8
Q_LORA = 512
KV_LORA = 512
QK_NOPE = 128
QK_ROPE = 64
V_HEAD = 128
ROPE_THETA = 10000.0
NA_HEADS = 8
NA_HEAD_DIM = 128
NA_KH = 8
NA_KW = 16
D_FF = 4 * D_MODEL
EPS = 1e-6
LOG2E = float(np.log2(np.e))

LANES = 128
MLA_QK_PAD = 2 * LANES
BF16_SUBLANES = 16
V_ROWS = V_HEAD + BF16_SUBLANES
NA_Q_ROWS = 8
NA_WIN_ROWS = 2 * NA_KH
NA_TQ = NA_Q_ROWS * GRID_W
NA_TK = NA_WIN_ROWS * GRID_W
NA_STRIP = NA_KH * GRID_W
NA_STRIP_PAD = NA_STRIP + LANES
VMEM_LIMIT = 56 * 1024 * 1024

BF16 = jnp.bfloat16
F32 = jnp.float32


def _params(n_axes):
    return pltpu.CompilerParams(
        dimension_semantics=("arbitrary",) * n_axes, vmem_limit_bytes=VMEM_LIMIT)


def _rms(x, g):
    return x * lax.rsqrt(jnp.mean(x * x, axis=-1, keepdims=True) + EPS) * g


def _rmsnorm_kernel(x_ref, g_ref, o_ref):
    o_ref[...] = _rms(x_ref[...], g_ref[...]).astype(o_ref.dtype)


def _rmsnorm(x, g, out_dtype, tm=512):
    s, d = x.shape
    return pl.pallas_call(
        _rmsnorm_kernel,
        grid=(s // tm,),
        in_specs=[pl.BlockSpec((tm, d), lambda i: (i, 0)),
                  pl.BlockSpec((1, d), lambda i: (0, 0))],
        out_specs=pl.BlockSpec((tm, d), lambda i: (i, 0)),
        out_shape=jax.ShapeDtypeStruct((s, d), out_dtype),
        compiler_params=_params(1),
        name="rmsnorm",
    )(x, g.reshape(1, d))


def _mm_kernel(a_ref, w_ref, o_ref):
    o_ref[...] = jnp.dot(a_ref[...], w_ref[...], preferred_element_type=F32).astype(o_ref.dtype)


def _mm_res_kernel(a_ref, w_ref, r_ref, o_ref):
    o_ref[...] = r_ref[...] + jnp.dot(a_ref[...], w_ref[...], preferred_element_type=F32)


def _mm(a, w, out_dtype, tm, tn, residual=None, name="mm"):
    m, k = a.shape
    n = w.shape[1]
    in_specs = [pl.BlockSpec((tm, k), lambda i, j: (i, 0)),
                pl.BlockSpec((k, tn), lambda i, j: (0, j))]
    args = [a, w]
    body = _mm_kernel
    if residual is not None:
        in_specs.append(pl.BlockSpec((tm, tn), lambda i, j: (i, j)))
        args.append(residual)
        body = _mm_res_kernel
    return pl.pallas_call(
        body,
        grid=(m // tm, n // tn),
        in_specs=in_specs,
        out_specs=pl.BlockSpec((tm, tn), lambda i, j: (i, j)),
        out_shape=jax.ShapeDtypeStruct((m, n), out_dtype),
        compiler_params=_params(2),
        name=name,
    )(*args)


def _rope128(x, c, sl, sr):
    return x * c + pltpu.roll(x, LANES - QK_ROPE // 2, 1) * sl + pltpu.roll(x, QK_ROPE // 2, 1) * sr


def _mla_up_kernel(c_ref, gq_ref, gkv_ref, wqn_ref, wqp_ref, wkn_ref, wvt_ref,
                   cos_ref, sl_ref, sr_ref, q_ref, k_ref, vt_ref):
    scale = (QK_NOPE + QK_ROPE) ** -0.5 * LOG2E
    cq = _rms(c_ref[:, :Q_LORA], gq_ref[...]).astype(BF16)
    ckv = _rms(c_ref[:, Q_LORA:Q_LORA + KV_LORA], gkv_ref[...]).astype(BF16)
    kpe = c_ref[:, Q_LORA + KV_LORA:]
    cos, sl, sr = cos_ref[...], sl_ref[...], sr_ref[...]
    kpe_rot = _rope128(kpe, cos, sl, sr).astype(BF16)
    qn = jnp.dot(cq, wqn_ref[...], preferred_element_type=F32) * scale
    qp = jnp.dot(cq, wqp_ref[...], preferred_element_type=F32)
    kn = jnp.dot(ckv, wkn_ref[...], preferred_element_type=F32)
    vt = lax.dot_general(wvt_ref[...], ckv, (((1,), (1,)), ((), ())),
                         preferred_element_type=F32).astype(BF16)
    for h in range(MLA_HEADS):
        vt_ref[h * V_ROWS:h * V_ROWS + V_HEAD, :] = vt[h * V_HEAD:(h + 1) * V_HEAD, :]
        vt_ref[h * V_ROWS + V_HEAD:(h + 1) * V_ROWS, :] = jnp.ones((V_ROWS - V_HEAD, vt.shape[1]), BF16)
    for h in range(MLA_HEADS):
        lo = h * MLA_QK_PAD
        hs = slice(h * LANES, (h + 1) * LANES)
        q_ref[:, lo:lo + LANES] = qn[:, hs].astype(BF16)
        q_ref[:, lo + LANES:lo + 2 * LANES] = (_rope128(qp[:, hs], cos, sl, sr) * scale).astype(BF16)
        k_ref[:, lo:lo + LANES] = kn[:, hs].astype(BF16)
        k_ref[:, lo + LANES:lo + 2 * LANES] = kpe_rot


def _mla_up(c, gq, gkv, wqn, wqp, wkn, wvt, cos_t, sl_t, sr_t, tm=512):
    s = c.shape[0]
    row = lambda w: pl.BlockSpec((tm, w), lambda i: (i, 0))
    full = lambda a: pl.BlockSpec(a.shape, lambda i: (0, 0))
    hw = MLA_HEADS * V_ROWS
    return pl.pallas_call(
        _mla_up_kernel,
        grid=(s // tm,),
        in_specs=[row(c.shape[1]), full(gq), full(gkv), full(wqn), full(wqp), full(wkn), full(wvt),
                  row(LANES), row(LANES), row(LANES)],
        out_specs=[row(MLA_HEADS * MLA_QK_PAD), row(MLA_HEADS * MLA_QK_PAD),
                   pl.BlockSpec((hw, tm), lambda i: (0, i))],
        out_shape=[jax.ShapeDtypeStruct((s, MLA_HEADS * MLA_QK_PAD), BF16),
                   jax.ShapeDtypeStruct((s, MLA_HEADS * MLA_QK_PAD), BF16),
                   jax.ShapeDtypeStruct((hw, s), BF16)],
        compiler_params=_params(1),
        name="mla_up",
    )(c, gq, gkv, wqn, wqp, wkn, wvt, cos_t, sl_t, sr_t)


def _mla_attn_kernel(q_ref, k_ref, vt_ref, o_ref, m_sc, acc_sc, sa_sc, sb_sc, ma_sc, mb_sc, *, tk):
    q = q_ref[...]
    n_chunks = k_ref.shape[0] // tk
    m_sc[...] = jnp.full(m_sc.shape, -jnp.inf, F32)
    acc_sc[...] = jnp.zeros(acc_sc.shape, F32)

    def scores(c, s_sc, cm_sc):
        k = k_ref[pl.ds(pl.multiple_of(c * tk, tk), tk), :]
        st = lax.dot_general(k, q, (((1,), (1,)), ((), ())), preferred_element_type=F32)
        s_sc[...] = st
        cm_sc[...] = jnp.max(st, axis=0, keepdims=True)

    def update(c, s_sc, cm_sc):
        vt = vt_ref[:, pl.ds(pl.multiple_of(c * tk, tk), tk)]
        m_prev = m_sc[...]
        m_new = jnp.maximum(m_prev, cm_sc[...])
        alpha = jnp.exp2(m_prev - m_new)
        pt = jnp.exp2(s_sc[...] - m_new)
        acc_sc[...] = alpha * acc_sc[...] + jnp.dot(vt, pt.astype(BF16), preferred_element_type=F32)
        m_sc[...] = m_new

    assert n_chunks % 2 == 0
    scores(0, sa_sc, ma_sc)

    def body(j, carry):
        c = 2 * j
        scores(c + 1, sb_sc, mb_sc)
        update(c, sa_sc, ma_sc)
        scores(c + 2, sa_sc, ma_sc)
        update(c + 1, sb_sc, mb_sc)
        return carry

    lax.fori_loop(0, n_chunks // 2 - 1, body, 0)
    scores(n_chunks - 1, sb_sc, mb_sc)
    update(n_chunks - 2, sa_sc, ma_sc)
    update(n_chunks - 1, sb_sc, mb_sc)
    o_ref[...] = (acc_sc[:V_HEAD, :] / acc_sc[V_HEAD:V_HEAD + 1, :]).T.astype(o_ref.dtype)


def _mla_attn(q, k, vt, tq=1024, tk=1024):
    s = q.shape[0]
    return pl.pallas_call(
        functools.partial(_mla_attn_kernel, tk=tk),
        grid=(MLA_HEADS, s // tq),
        in_specs=[pl.BlockSpec((tq, MLA_QK_PAD), lambda h, i: (i, h)),
                  pl.BlockSpec((s, MLA_QK_PAD), lambda h, i: (0, h)),
                  pl.BlockSpec((V_ROWS, s), lambda h, i: (h, 0))],
        out_specs=pl.BlockSpec((tq, V_HEAD), lambda h, i: (i, h)),
        out_shape=jax.ShapeDtypeStruct((s, MLA_HEADS * V_HEAD), BF16),
        scratch_shapes=[pltpu.VMEM((1, tq), F32), pltpu.VMEM((V_ROWS, tq), F32),
                        pltpu.VMEM((tk, tq), F32), pltpu.VMEM((tk, tq), F32),
                        pltpu.VMEM((1, tq), F32), pltpu.VMEM((1, tq), F32)],
        compiler_params=_params(2),
        name="mla_attn",
    )(q, k, vt)


def _na_row_start(r):
    return min(max(r - NA_KH // 2, 0), ROWS - NA_KH)


def _na_win_start(blk):
    return min(max(blk * NA_Q_ROWS - NA_KH // 2, 0), ROWS - NA_WIN_ROWS)


def _na_strip_tables(rpb_l):
    c = np.arange(GRID_W)
    col_start = np.clip(c - NA_KW // 2, 0, GRID_W - NA_KW)
    col_ok = (c[None, :] >= col_start[:, None]) & (c[None, :] < col_start[:, None] + NA_KW)
    dx = np.clip(c[None, :] - c[:, None], -(NA_KW - 1), NA_KW - 1) + (NA_KW - 1)
    rpb_l = rpb_l.astype(F32)
    toe = jnp.full((NA_HEADS, 2 * NA_KH - 1, GRID_W, GRID_W), -jnp.inf, F32)
    for d in range(2 * NA_KW - 1):
        toe = jnp.where((dx == d) & col_ok, rpb_l[:, :, d][:, :, None, None], toe)
    t = jnp.stack([toe[:, dy0:dy0 + NA_KH] for dy0 in range(NA_KH)], axis=1)
    t = t.transpose(0, 1, 3, 2, 4).reshape(NA_HEADS, NA_KH, GRID_W, NA_STRIP)
    pad = lambda lo, hi: jnp.pad(t, ((0, 0), (0, 0), (0, 0), (lo, hi)), constant_values=-jnp.inf)
    return jnp.stack([pad(0, LANES), pad(GRID_W, LANES - GRID_W)], axis=2)


def _na_attn_kernel(q_ref, k_ref, v_ref, t_ref, o_ref, bias_sc):
    blk = pl.program_id(1)
    nblk = pl.num_programs(1)

    @pl.when(blk == 0)
    def _build_bias():
        bias_sc[...] = jnp.full(bias_sc.shape, -jnp.inf, F32)
        for var, b in enumerate((0, 1, ROWS // NA_Q_ROWS - 1)):
            for ri in range(NA_Q_ROWS):
                r = b * NA_Q_ROWS + ri
                off = _na_row_start(r) - _na_win_start(b)
                dy0 = _na_row_start(r) - r + (NA_KH - 1)
                left = LANES * (off // 2)
                width = min(NA_STRIP_PAD, NA_TK - left)
                bias_sc[var, ri * GRID_W:(ri + 1) * GRID_W, left:left + width] = (
                    t_ref[0, dy0, off % 2, :, :width])

    var = jnp.where(blk == 0, 0, jnp.where(blk == nblk - 1, 2, 1))
    start = jnp.clip(blk * NA_TQ - (NA_KH // 2) * GRID_W, 0, SEQ - NA_TK)
    start = pl.multiple_of(start, (NA_KH // 2) * GRID_W)
    k = k_ref[pl.ds(start, NA_TK), :]
    v = v_ref[pl.ds(start, NA_TK), :]
    s = lax.dot_general(q_ref[...], k, (((1,), (1,)), ((), ())), preferred_element_type=F32)
    s = s * (NA_HEAD_DIM ** -0.5) + bias_sc[var]
    m = jnp.max(s, axis=1, keepdims=True)
    p = jnp.exp(s - m)
    l = jnp.sum(p, axis=1, keepdims=True)
    o = jnp.dot(p.astype(BF16), v, preferred_element_type=F32)
    o_ref[...] = (o / l).astype(o_ref.dtype)


def _na_attn(qkv, strips):
    s = qkv.shape[0]
    return pl.pallas_call(
        _na_attn_kernel,
        grid=(NA_HEADS, s // NA_TQ),
        in_specs=[pl.BlockSpec((NA_TQ, NA_HEAD_DIM), lambda h, b: (b, h)),
                  pl.BlockSpec((s, NA_HEAD_DIM), lambda h, b: (0, NA_HEADS + h)),
                  pl.BlockSpec((s, NA_HEAD_DIM), lambda h, b: (0, 2 * NA_HEADS + h)),
                  pl.BlockSpec((1, NA_KH, 2, GRID_W, NA_STRIP_PAD), lambda h, b: (h, 0, 0, 0, 0))],
        out_specs=pl.BlockSpec((NA_TQ, NA_HEAD_DIM), lambda h, b: (b, h)),
        out_shape=jax.ShapeDtypeStruct((s, NA_HEADS * NA_HEAD_DIM), BF16),
        scratch_shapes=[pltpu.VMEM((3, NA_TQ, NA_TK), F32)],
        compiler_params=_params(2),
        name="na_attn",
    )(qkv, qkv, qkv, strips)


def _merge_kernel(ya_ref, yb_ref, g_ref, woa_ref, wob_ref, o_ref):
    a = jnp.dot(ya_ref[...], woa_ref[...], preferred_element_type=F32)
    b = jnp.dot(yb_ref[...], wob_ref[...], preferred_element_type=F32)
    ga = jax.nn.sigmoid(g_ref[:, :D_MODEL])
    gb = jax.nn.sigmoid(g_ref[:, D_MODEL:])
    o_ref[...] = (ga * a + gb * b).astype(o_ref.dtype)


def _merge(ya, yb, gates, woa, wob, tm=256):
    s = ya.shape[0]
    row = lambda w: pl.BlockSpec((tm, w), lambda i: (i, 0))
    full = lambda a: pl.BlockSpec(a.shape, lambda i: (0, 0))
    return pl.pallas_call(
        _merge_kernel,
        grid=(s // tm,),
        in_specs=[row(ya.shape[1]), row(yb.shape[1]), row(gates.shape[1]), full(woa), full(wob)],
        out_specs=row(D_MODEL),
        out_shape=jax.ShapeDtypeStruct((s, D_MODEL), BF16),
        compiler_params=_params(1),
        name="merge",
    )(ya, yb, gates, woa, wob)


def _ffn_kernel(x_ref, g_ref, w1_ref, w2_ref, o_ref, u_sc):
    f = pl.program_id(1)

    @pl.when(f == 0)
    def _init():
        x = x_ref[...]
        u_sc[...] = _rms(x, g_ref[...]).astype(BF16)
        o_ref[...] = x

    h = jnp.dot(u_sc[...], w1_ref[...], preferred_element_type=F32)
    a = jnp.square(jnp.maximum(h, 0.0)).astype(BF16)
    o_ref[...] += jnp.dot(a, w2_ref[...], preferred_element_type=F32)


def _ffn(x, g, w1, w2, tm=512, tf=1024):
    s, d = x.shape
    dff = w1.shape[1]
    return pl.pallas_call(
        _ffn_kernel,
        grid=(s // tm, dff // tf),
        in_specs=[pl.BlockSpec((tm, d), lambda i, f: (i, 0)),
                  pl.BlockSpec((1, d), lambda i, f: (0, 0)),
                  pl.BlockSpec((d, tf), lambda i, f: (0, f)),
                  pl.BlockSpec((tf, d), lambda i, f: (f, 0))],
        out_specs=pl.BlockSpec((tm, d), lambda i, f: (i, 0)),
        out_shape=jax.ShapeDtypeStruct((s, d), F32),
        scratch_shapes=[pltpu.VMEM((tm, d), BF16)],
        compiler_params=_params(2),
        name="ffn",
    )(x, g.reshape(1, d), w1, w2)


def _rope_tables(s):
    pos = jnp.arange(s, dtype=F32)
    inv_freq = 1.0 / (ROPE_THETA ** (jnp.arange(0, QK_ROPE, 2, dtype=F32) / QK_ROPE))
    ang = pos[:, None] * inv_freq[None, :]
    cos, sin = jnp.cos(ang), jnp.sin(ang)
    half = QK_ROPE // 2
    z = lambda w: jnp.zeros((s, w), F32)
    cos_t = jnp.concatenate([cos, cos, z(LANES - QK_ROPE)], axis=1)
    sl_t = jnp.concatenate([-sin, z(LANES - half)], axis=1)
    sr_t = jnp.concatenate([z(half), sin, z(LANES - QK_ROPE)], axis=1)
    return cos_t, sl_t, sr_t


def kernel(x, norm_mix, w_in, norm_qa, w_uq, norm_kva, w_ukv, rpb, w_o_mla, w_o_na, w_out,
           norm_mlp, w_ff1, w_ff2, norm_final):
    b, s, d = x.shape
    assert (b, s, d) == (1, SEQ, D_MODEL)
    depth = w_in.shape[0]
    cos_t, sl_t, sr_t = _rope_tables(s)

    c_end = Q_LORA + KV_LORA + QK_ROPE
    na_w = NA_HEADS * NA_HEAD_DIM
    w_c = jnp.pad(w_in[:, :, :c_end], ((0, 0), (0, 0), (0, LANES - QK_ROPE))).astype(BF16)
    w_na = w_in[:, :, c_end:c_end + 3 * na_w].astype(BF16)
    w_g = w_in[:, :, c_end + 3 * na_w:].astype(BF16)
    uq = w_uq.reshape(depth, Q_LORA, MLA_HEADS, QK_NOPE + QK_ROPE)
    w_qn = uq[..., :QK_NOPE].reshape(depth, Q_LORA, MLA_HEADS * QK_NOPE).astype(BF16)
    w_qp = jnp.pad(uq[..., QK_NOPE:], ((0, 0), (0, 0), (0, 0), (0, LANES - QK_ROPE)))
    w_qp = w_qp.reshape(depth, Q_LORA, MLA_HEADS * LANES).astype(BF16)
    ukv = w_ukv.reshape(depth, KV_LORA, MLA_HEADS, QK_NOPE + V_HEAD)
    w_kn = ukv[..., :QK_NOPE].reshape(depth, KV_LORA, MLA_HEADS * QK_NOPE).astype(BF16)
    w_vt = ukv[..., QK_NOPE:].reshape(depth, KV_LORA, MLA_HEADS * V_HEAD).transpose(0, 2, 1).astype(BF16)
    w_oa = w_o_mla.astype(BF16)
    w_ob = w_o_na.astype(BF16)
    w_o = w_out.astype(BF16)
    w_1 = w_ff1.astype(BF16)
    w_2 = w_ff2.astype(BF16)

    xs = x.reshape(s, d)
    for l in range(depth):
        u = _rmsnorm(xs, norm_mix[l], BF16)
        c = _mm(u, w_c[l], F32, 1024, w_c.shape[2], name="proj_c")
        qkv_na = _mm(u, w_na[l], BF16, 1024, 1024, name="proj_na")
        gates = _mm(u, w_g[l], F32, 1024, 1024, name="proj_gate")
        q, k, v = _mla_up(c, norm_qa[l].reshape(1, -1), norm_kva[l].reshape(1, -1),
                          w_qn[l], w_qp[l], w_kn[l], w_vt[l], cos_t, sl_t, sr_t)
        y_a = _mla_attn(q, k, v)
        y_b = _na_attn(qkv_na, _na_strip_tables(rpb[l]))
        merged = _merge(y_a, y_b, gates, w_oa[l], w_ob[l])
        xs = _mm(merged, w_o[l], F32, 1024, 1024, residual=xs, name="out_proj")
        xs = _ffn(xs, norm_mlp[l], w_1[l], w_2[l])
    return _rmsnorm(xs, norm_final, F32).reshape(b, s, d)
```

```python
import functools

import jax
import jax.numpy as jnp
import numpy as np
from jax import lax
from jax.experimental import pallas as pl
from jax.experimental.pallas import tpu as pltpu

D_MODEL = 2048
SEQ = 8192
GRID_W = 64
ROWS = SEQ // GRID_W
MLA_HEADS = 8
Q_LORA = 512
KV_LORA = 512
QK_NOPE = 128
QK_ROPE = 64
V_HEAD = 128
ROPE_THETA = 10000.0
NA_HEADS = 8
NA_HEAD_DIM = 128
NA_KH = 8
NA_KW = 16
D_FF = 4 * D_MODEL
EPS = 1e-6
LOG2E = float(np.log2(np.e))

LANES = 128
MLA_QK_PAD = 2 * LANES
BF16_SUBLANES = 16
V_ROWS = V_HEAD + BF16_SUBLANES
NA_Q_ROWS = 8
NA_WIN_ROWS = 2 * NA_KH
NA_TQ = NA_Q_ROWS * GRID_W
NA_TK = NA_WIN_ROWS * GRID_W
NA_BLOCKS = SEQ // NA_TQ
NA_STRIP = NA_KH * GRID_W
NA_PAIR_ROWS = NA_STRIP + GRID_W
NA_PAIR_KINDS = 1 + NA_KH // 2
VMEM_LIMIT = 56 * 1024 * 1024

BF16 = jnp.bfloat16
F32 = jnp.float32
NT_DIMS = (((1,), (1,)), ((), ()))


def _params(n_axes):
    return pltpu.CompilerParams(
        dimension_semantics=("arbitrary",) * n_axes, vmem_limit_bytes=VMEM_LIMIT)


def _rms(x, g):
    return x * lax.rsqrt(jnp.mean(x * x, axis=-1, keepdims=True) + EPS) * g


def _layer_spec(shape, l, index_map):
    return pl.BlockSpec((None,) + tuple(shape), lambda *ids: (l,) + tuple(index_map(*ids)))


def _store_vt_with_ones(vt_ref, vt, heads):
    for h in range(heads):
        vt_ref[h * V_ROWS:h * V_ROWS + V_HEAD, :] = vt[h * V_HEAD:(h + 1) * V_HEAD, :]
        vt_ref[h * V_ROWS + V_HEAD:(h + 1) * V_ROWS, :] = jnp.ones((V_ROWS - V_HEAD, vt.shape[1]), BF16)


def _rmsnorm_kernel(x_ref, g_ref, o_ref):
    o_ref[...] = _rms(x_ref[...], g_ref[...]).astype(o_ref.dtype)


def _rmsnorm(x, g, out_dtype, tm=512):
    s, d = x.shape
    return pl.pallas_call(
        _rmsnorm_kernel,
        grid=(s // tm,),
        in_specs=[pl.BlockSpec((tm, d), lambda i: (i, 0)),
                  pl.BlockSpec((1, d), lambda i: (0, 0))],
        out_specs=pl.BlockSpec((tm, d), lambda i: (i, 0)),
        out_shape=jax.ShapeDtypeStruct((s, d), out_dtype),
        compiler_params=_params(1),
        name="rmsnorm",
    )(x, g.reshape(1, d))


def _mm_kernel(a_ref, w_ref, o_ref):
    o_ref[...] = jnp.dot(a_ref[...], w_ref[...], preferred_element_type=F32).astype(o_ref.dtype)


def _mm_res_kernel(a_ref, w_ref, r_ref, o_ref):
    o_ref[...] = r_ref[...] + jnp.dot(a_ref[...], w_ref[...], preferred_element_type=F32)


def _mm(a, w, l, out_dtype, tm, tn, residual=None, name="mm"):
    m, k = a.shape
    n = w.shape[2]
    in_specs = [pl.BlockSpec((tm, k), lambda i, j: (i, 0)),
                _layer_spec((k, tn), l, lambda i, j: (0, j))]
    args = [a, w]
    body = _mm_kernel
    if residual is not None:
        in_specs.append(pl.BlockSpec((tm, tn), lambda i, j: (i, j)))
        args.append(residual)
        body = _mm_res_kernel
    return pl.pallas_call(
        body,
        grid=(m // tm, n // tn),
        in_specs=in_specs,
        out_specs=pl.BlockSpec((tm, tn), lambda i, j: (i, j)),
        out_shape=jax.ShapeDtypeStruct((m, n), out_dtype),
        compiler_params=_params(2),
        name=name,
    )(*args)


def _proj_vt_kernel(u_ref, wt_ref, vt_ref):
    vt = lax.dot_general(wt_ref[...], u_ref[...], NT_DIMS, preferred_element_type=F32).astype(BF16)
    _store_vt_with_ones(vt_ref, vt, NA_HEADS)


def _proj_vt(u, wt, l, tm=1024):
    s, k = u.shape
    return pl.pallas_call(
        _proj_vt_kernel,
        grid=(s // tm,),
        in_specs=[pl.BlockSpec((tm, k), lambda i: (i, 0)),
                  _layer_spec(wt.shape[1:], l, lambda i: (0, 0))],
        out_specs=pl.BlockSpec((NA_HEADS * V_ROWS, tm), lambda i: (0, i)),
        out_shape=jax.ShapeDtypeStruct((NA_HEADS * V_ROWS, s), BF16),
        compiler_params=_params(1),
        name="proj_na_vt",
    )(u, wt)


def _rope128(x, c, sl, sr):
    return x * c + pltpu.roll(x, LANES - QK_ROPE // 2, 1) * sl + pltpu.roll(x, QK_ROPE // 2, 1) * sr


def _mla_up_kernel(c_ref, gq_ref, gkv_ref, wqn_ref, wqp_ref, wkn_ref, wvt_ref,
                   cos_ref, sl_ref, sr_ref, q_ref, k_ref, vt_ref):
    scale = (QK_NOPE + QK_ROPE) ** -0.5 * LOG2E
    cq = _rms(c_ref[:, :Q_LORA], gq_ref[...]).astype(BF16)
    ckv = _rms(c_ref[:, Q_LORA:Q_LORA + KV_LORA], gkv_ref[...]).astype(BF16)
    kpe = c_ref[:, Q_LORA + KV_LORA:]
    cos, sl, sr = cos_ref[...], sl_ref[...], sr_ref[...]
    kpe_rot = _rope128(kpe, cos, sl, sr).astype(BF16)
    qn = jnp.dot(cq, wqn_ref[...], preferred_element_type=F32) * scale
    qp = jnp.dot(cq, wqp_ref[...], preferred_element_type=F32)
    kn = jnp.dot(ckv, wkn_ref[...], preferred_element_type=F32)
    vt = lax.dot_general(wvt_ref[...], ckv, NT_DIMS, preferred_element_type=F32).astype(BF16)
    _store_vt_with_ones(vt_ref, vt, MLA_HEADS)
    for h in range(MLA_HEADS):
        lo = h * MLA_QK_PAD
        hs = slice(h * LANES, (h + 1) * LANES)
        q_ref[:, lo:lo + LANES] = qn[:, hs].astype(BF16)
        q_ref[:, lo + LANES:lo + 2 * LANES] = (_rope128(qp[:, hs], cos, sl, sr) * scale).astype(BF16)
        k_ref[:, lo:lo + LANES] = kn[:, hs].astype(BF16)
        k_ref[:, lo + LANES:lo + 2 * LANES] = kpe_rot


def _mla_up(c, gq, gkv, wqn, wqp, wkn, wvt, l, cos_t, sl_t, sr_t, tm=512):
    s = c.shape[0]
    row = lambda w: pl.BlockSpec((tm, w), lambda i: (i, 0))
    full = lambda a: _layer_spec(a.shape[1:], l, lambda i: (0, 0))
    hw = MLA_HEADS * V_ROWS
    return pl.pallas_call(
        _mla_up_kernel,
        grid=(s // tm,),
        in_specs=[row(c.shape[1]), full(gq), full(gkv), full(wqn), full(wqp), full(wkn), full(wvt),
                  row(LANES), row(LANES), row(LANES)],
        out_specs=[row(MLA_HEADS * MLA_QK_PAD), row(MLA_HEADS * MLA_QK_PAD),
                   pl.BlockSpec((hw, tm), lambda i: (0, i))],
        out_shape=[jax.ShapeDtypeStruct((s, MLA_HEADS * MLA_QK_PAD), BF16),
                   jax.ShapeDtypeStruct((s, MLA_HEADS * MLA_QK_PAD), BF16),
                   jax.ShapeDtypeStruct((hw, s), BF16)],
        compiler_params=_params(1),
        name="mla_up",
    )(c, gq, gkv, wqn, wqp, wkn, wvt, cos_t, sl_t, sr_t)


def _mla_attn_kernel(q_ref, k_ref, vt_ref, o_ref, m_sc, acc_sc, sa_sc, sb_sc, ma_sc, mb_sc, *, tk):
    q = q_ref[...]
    n_chunks = k_ref.shape[0] // tk
    m_sc[...] = jnp.full(m_sc.shape, -jnp.inf, F32)
    acc_sc[...] = jnp.zeros(acc_sc.shape, F32)

    def scores(c, s_sc, cm_sc):
        k = k_ref[pl.ds(pl.multiple_of(c * tk, tk), tk), :]
        st = lax.dot_general(k, q, NT_DIMS, preferred_element_type=F32)
        s_sc[...] = st
        cm_sc[...] = jnp.max(st, axis=0, keepdims=True)

    def update(c, s_sc, cm_sc):
        vt = vt_ref[:, pl.ds(pl.multiple_of(c * tk, tk), tk)]
        m_prev = m_sc[...]
        m_new = jnp.maximum(m_prev, cm_sc[...])
        alpha = jnp.exp2(m_prev - m_new)
        pt = jnp.exp2(s_sc[...] - m_new)
        acc_sc[...] = alpha * acc_sc[...] + jnp.dot(vt, pt.astype(BF16), preferred_element_type=F32)
        m_sc[...] = m_new

    assert n_chunks % 2 == 0
    scores(0, sa_sc, ma_sc)

    def body(j, carry):
        c = 2 * j
        scores(c + 1, sb_sc, mb_sc)
        update(c, sa_sc, ma_sc)
        scores(c + 2, sa_sc, ma_sc)
        update(c + 1, sb_sc, mb_sc)
        return carry

    lax.fori_loop(0, n_chunks // 2 - 1, body, 0)
    scores(n_chunks - 1, sb_sc, mb_sc)
    update(n_chunks - 2, sa_sc, ma_sc)
    update(n_chunks - 1, sb_sc, mb_sc)
    o_ref[...] = (acc_sc[:V_HEAD, :] / acc_sc[V_HEAD:V_HEAD + 1, :]).T.astype(o_ref.dtype)


def _mla_attn(q, k, vt, tq=1024, tk=1024):
    s = q.shape[0]
    return pl.pallas_call(
        functools.partial(_mla_attn_kernel, tk=tk),
        grid=(MLA_HEADS, s // tq),
        in_specs=[pl.BlockSpec((tq, MLA_QK_PAD), lambda h, i: (i, h)),
                  pl.BlockSpec((s, MLA_QK_PAD), lambda h, i: (0, h)),
                  pl.BlockSpec((V_ROWS, s), lambda h, i: (h, 0))],
        out_specs=pl.BlockSpec((tq, V_HEAD), lambda h, i: (i, h)),
        out_shape=jax.ShapeDtypeStruct((s, MLA_HEADS * V_HEAD), BF16),
        scratch_shapes=[pltpu.VMEM((1, tq), F32), pltpu.VMEM((V_ROWS, tq), F32),
                        pltpu.VMEM((tk, tq), F32), pltpu.VMEM((tk, tq), F32),
                        pltpu.VMEM((1, tq), F32), pltpu.VMEM((1, tq), F32)],
        compiler_params=_params(2),
        name="mla_attn",
    )(q, k, vt)


def _na_row_start(r):
    return min(max(r - NA_KH // 2, 0), ROWS - NA_KH)


def _na_win_start(blk):
    return min(max(blk * NA_Q_ROWS - NA_KH // 2, 0), ROWS - NA_WIN_ROWS)


def _na_pair_placement(blk, pair):
    r_e = blk * NA_Q_ROWS + 2 * pair
    r_o = r_e + 1
    off_e = _na_row_start(r_e) - _na_win_start(blk)
    off_o = _na_row_start(r_o) - _na_win_start(blk)
    dy_e = _na_row_start(r_e) - r_e + (NA_KH - 1)
    dy_o = _na_row_start(r_o) - r_o + (NA_KH - 1)
    if off_o == off_e + 1:
        assert dy_e == dy_o == NA_KH // 2 - 1
        return 0, off_e * GRID_W, NA_PAIR_ROWS
    assert off_o == off_e and dy_e == dy_o + 1 and dy_e % 2 == 1
    return 1 + (NA_KH - 1 - dy_e) // 2, off_e * GRID_W, NA_STRIP


def _na_pair_tables(rpb_l):
    c = np.arange(GRID_W)
    col_start = np.clip(c - NA_KW // 2, 0, GRID_W - NA_KW)
    col_ok = (c[None, :] >= col_start[:, None]) & (c[None, :] < col_start[:, None] + NA_KW)
    dx = np.clip(c[None, :] - c[:, None], -(NA_KW - 1), NA_KW - 1) + (NA_KW - 1)
    rpb_l = rpb_l.astype(F32)
    toe = jnp.full((NA_HEADS, 2 * NA_KH - 1, GRID_W, GRID_W), -jnp.inf, F32)
    for d in range(2 * NA_KW - 1):
        toe = jnp.where(((dx == d) & col_ok).T, rpb_l[:, :, d][:, :, None, None], toe)

    def strip(dy0, lo, hi):
        t = toe[:, dy0:dy0 + NA_KH].reshape(NA_HEADS, NA_STRIP, GRID_W)
        return jnp.pad(t, ((0, 0), (lo, hi), (0, 0)), constant_values=-jnp.inf)

    mid = NA_KH // 2 - 1
    kinds = [jnp.concatenate([strip(mid, 0, GRID_W), strip(mid, GRID_W, 0)], axis=2)]
    for j in range(NA_KH // 2):
        dy_e = NA_KH - 1 - 2 * j
        kinds.append(jnp.concatenate([strip(dy_e, 0, GRID_W), strip(dy_e - 1, 0, GRID_W)], axis=2))
    return jnp.stack(kinds, axis=1)


def _na_attn_kernel(q_ref, k_ref, vt_ref, t_ref, o_ref, bias_sc, sa_sc, sb_sc, ma_sc, mb_sc):
    last = NA_BLOCKS - 1
    bias_sc[...] = jnp.full(bias_sc.shape, -jnp.inf, F32)
    for var, blk in enumerate((0, 1, last)):
        for pair in range(NA_Q_ROWS // 2):
            kind, top, rows = _na_pair_placement(blk, pair)
            bias_sc[var, top:top + rows, pair * LANES:(pair + 1) * LANES] = t_ref[0, kind, :rows, :] * LOG2E

    def win_start(blk):
        start = jnp.clip(blk * NA_TQ - (NA_KH // 2) * GRID_W, 0, SEQ - NA_TK)
        return pl.multiple_of(start, (NA_KH // 2) * GRID_W)

    def scores(blk, var, s_sc, cm_sc):
        q = q_ref[pl.ds(pl.multiple_of(blk * NA_TQ, NA_TQ), NA_TQ), :]
        k = k_ref[pl.ds(win_start(blk), NA_TK), :]
        st = lax.dot_general(k, q, NT_DIMS, preferred_element_type=F32)
        st = st * (NA_HEAD_DIM ** -0.5 * LOG2E) + bias_sc[var]
        s_sc[...] = st
        cm_sc[...] = jnp.max(st, axis=0, keepdims=True)

    def finish(blk, s_sc, cm_sc):
        vt = vt_ref[:, pl.ds(win_start(blk), NA_TK)]
        pt = jnp.exp2(s_sc[...] - cm_sc[...])
        acc = jnp.dot(vt, pt.astype(BF16), preferred_element_type=F32)
        o = (acc[:V_HEAD, :] / acc[V_HEAD:V_HEAD + 1, :]).T
        o_ref[pl.ds(pl.multiple_of(blk * NA_TQ, NA_TQ), NA_TQ), :] = o.astype(o_ref.dtype)

    assert NA_BLOCKS % 2 == 0
    scores(0, 0, sa_sc, ma_sc)

    def body(j, carry):
        b = 2 * j
        scores(b + 1, 1, sb_sc, mb_sc)
        finish(b, sa_sc, ma_sc)
        scores(b + 2, 1, sa_sc, ma_sc)
        finish(b + 1, sb_sc, mb_sc)
        return carry

    lax.fori_loop(0, NA_BLOCKS // 2 - 1, body, 0)
    scores(last, 2, sb_sc, mb_sc)
    finish(last - 1, sa_sc, ma_sc)
    finish(last, sb_sc, mb_sc)


def _na_attn(qk, vt, tables):
    s = qk.shape[0]
    return pl.pallas_call(
        _na_attn_kernel,
        grid=(NA_HEADS,),
        in_specs=[pl.BlockSpec((s, NA_HEAD_DIM), lambda h: (0, h)),
                  pl.BlockSpec((s, NA_HEAD_DIM), lambda h: (0, NA_HEADS + h)),
                  pl.BlockSpec((V_ROWS, s), lambda h: (h, 0)),
                  pl.BlockSpec((1, NA_PAIR_KINDS, NA_PAIR_ROWS, LANES), lambda h: (h, 0, 0, 0))],
        out_specs=pl.BlockSpec((s, NA_HEAD_DIM), lambda h: (0, h)),
        out_shape=jax.ShapeDtypeStruct((s, NA_HEADS * NA_HEAD_DIM), BF16),
        scratch_shapes=[pltpu.VMEM((3, NA_TK, NA_TQ), F32),
                        pltpu.VMEM((NA_TK, NA_TQ), F32), pltpu.VMEM((NA_TK, NA_TQ), F32),
                        pltpu.VMEM((1, NA_TQ), F32), pltpu.VMEM((1, NA_TQ), F32)],
        compiler_params=_params(1),
        name="na_attn",
    )(qk, qk, vt, tables)


def _merge_kernel(ya_ref, yb_ref, g_ref, woa_ref, wob_ref, o_ref):
    a = jnp.dot(ya_ref[...], woa_ref[...], preferred_element_type=F32)
    b = jnp.dot(yb_ref[...], wob_ref[...], preferred_element_type=F32)
    ga = jax.nn.sigmoid(g_ref[:, :D_MODEL])
    gb = jax.nn.sigmoid(g_ref[:, D_MODEL:])
    o_ref[...] = (ga * a + gb * b).astype(o_ref.dtype)


def _merge(ya, yb, gates, woa, wob, l, tm=256):
    s = ya.shape[0]
    row = lambda w: pl.BlockSpec((tm, w), lambda i: (i, 0))
    full = lambda a: _layer_spec(a.shape[1:], l, lambda i: (0, 0))
    return pl.pallas_call(
        _merge_kernel,
        grid=(s // tm,),
        in_specs=[row(ya.shape[1]), row(yb.shape[1]), row(gates.shape[1]), full(woa), full(wob)],
        out_specs=row(D_MODEL),
        out_shape=jax.ShapeDtypeStruct((s, D_MODEL), BF16),
        compiler_params=_params(1),
        name="merge",
    )(ya, yb, gates, woa, wob)


def _ffn_kernel(x_ref, g_ref, w1_ref, w2_ref, o_ref, u_sc):
    f = pl.program_id(1)

    @pl.when(f == 0)
    def _init():
        x = x_ref[...]
        u_sc[...] = _rms(x, g_ref[...]).astype(BF16)
        o_ref[...] = x

    h = jnp.dot(u_sc[...], w1_ref[...], preferred_element_type=F32)
    a = jnp.square(jnp.maximum(h, 0.0)).astype(BF16)
    o_ref[...] += jnp.dot(a, w2_ref[...], preferred_element_type=F32)


def _ffn(x, g, w1, w2, l, tm=512, tf=1024):
    s, d = x.shape
    return pl.pallas_call(
        _ffn_kernel,
        grid=(s // tm, w1.shape[2] // tf),
        in_specs=[pl.BlockSpec((tm, d), lambda i, f: (i, 0)),
                  _layer_spec((1, d), l, lambda i, f: (0, 0)),
                  _layer_spec((d, tf), l, lambda i, f: (0, f)),
                  _layer_spec((tf, d), l, lambda i, f: (f, 0))],
        out_specs=pl.BlockSpec((tm, d), lambda i, f: (i, 0)),
        out_shape=jax.ShapeDtypeStruct((s, d), F32),
        scratch_shapes=[pltpu.VMEM((tm, d), BF16)],
        compiler_params=_params(2),
        name="ffn",
    )(x, g, w1, w2)


def _rope_tables(s):
    pos = jnp.arange(s, dtype=F32)
    inv_freq = 1.0 / (ROPE_THETA ** (jnp.arange(0, QK_ROPE, 2, dtype=F32) / QK_ROPE))
    ang = pos[:, None] * inv_freq[None, :]
    cos, sin = jnp.cos(ang), jnp.sin(ang)
    half = QK_ROPE // 2
    z = lambda w: jnp.zeros((s, w), F32)
    cos_t = jnp.concatenate([cos, cos, z(LANES - QK_ROPE)], axis=1)
    sl_t = jnp.concatenate([-sin, z(LANES - half)], axis=1)
    sr_t = jnp.concatenate([z(half), sin, z(LANES - QK_ROPE)], axis=1)
    return cos_t, sl_t, sr_t


def kernel(x, norm_mix, w_in, norm_qa, w_uq, norm_kva, w_ukv, rpb, w_o_mla, w_o_na, w_out,
           norm_mlp, w_ff1, w_ff2, norm_final):
    b, s, d = x.shape
    assert (b, s, d) == (1, SEQ, D_MODEL)
    depth = w_in.shape[0]
    cos_t, sl_t, sr_t = _rope_tables(s)

    c_end = Q_LORA + KV_LORA + QK_ROPE
    na_w = NA_HEADS * NA_HEAD_DIM
    w_c = jnp.pad(w_in[:, :, :c_end], ((0, 0), (0, 0), (0, LANES - QK_ROPE))).astype(BF16)
    w_na_qk = w_in[:, :, c_end:c_end + 2 * na_w].astype(BF16)
    w_na_vt = w_in[:, :, c_end + 2 * na_w:c_end + 3 * na_w].transpose(0, 2, 1).astype(BF16)
    w_g = w_in[:, :, c_end + 3 * na_w:].astype(BF16)
    uq = w_uq.reshape(depth, Q_LORA, MLA_HEADS, QK_NOPE + QK_ROPE)
    w_qn = uq[..., :QK_NOPE].reshape(depth, Q_LORA, MLA_HEADS * QK_NOPE).astype(BF16)
    w_qp = jnp.pad(uq[..., QK_NOPE:], ((0, 0), (0, 0), (0, 0), (0, LANES - QK_ROPE)))
    w_qp = w_qp.reshape(depth, Q_LORA, MLA_HEADS * LANES).astype(BF16)
    ukv = w_ukv.reshape(depth, KV_LORA, MLA_HEADS, QK_NOPE + V_HEAD)
    w_kn = ukv[..., :QK_NOPE].reshape(depth, KV_LORA, MLA_HEADS * QK_NOPE).astype(BF16)
    w_vt = ukv[..., QK_NOPE:].reshape(depth, KV_LORA, MLA_HEADS * V_HEAD).transpose(0, 2, 1).astype(BF16)
    w_oa = w_o_mla.astype(BF16)
    w_ob = w_o_na.astype(BF16)
    w_o = w_out.astype(BF16)
    w_1 = w_ff1.astype(BF16)
    w_2 = w_ff2.astype(BF16)
    g_qa = norm_qa.reshape(depth, 1, Q_LORA)
    g_kva = norm_kva.reshape(depth, 1, KV_LORA)
    g_mlp = norm_mlp.reshape(depth, 1, d)

    xs = x.reshape(s, d)
    for l in range(depth):
        u = _rmsnorm(xs, norm_mix[l], BF16)
        c = _mm(u, w_c, l, F32, 1024, w_c.shape[2], name="proj_c")
        qk_na = _mm(u, w_na_qk, l, BF16, 1024, 1024, name="proj_na_qk")
        vt_na = _proj_vt(u, w_na_vt, l)
        gates = _mm(u, w_g, l, F32, 1024, 1024, name="proj_gate")
        q, k, vt = _mla_up(c, g_qa, g_kva, w_qn, w_qp, w_kn, w_vt, l, cos_t, sl_t, sr_t)
        y_a = _mla_attn(q, k, vt)
        y_b = _na_attn(qk_na, vt_na, _na_pair_tables(rpb[l]))
        merged = _merge(y_a, y_b, gates, w_oa, w_ob, l)
        xs = _mm(merged, w_o, l, F32, 1024, 1024, residual=xs, name="out_proj")
        xs = _ffn(xs, g_mlp, w_1, w_2, l)
    return _rmsnorm(xs, norm_final, F32).reshape(b, s, d)
```

```python
import functools

import jax
import jax.numpy as jnp
import numpy as np
from jax import lax
from jax.experimental import pallas as pl
from jax.experimental.pallas import tpu as pltpu

D_MODEL = 2048
SEQ = 8192
GRID_W = 64
ROWS = SEQ // GRID_W
MLA_HEADS = 8
Q_LORA = 512
KV_LORA = 512
QK_NOPE = 128
QK_ROPE = 64
V_HEAD = 128
ROPE_THETA = 10000.0
NA_HEADS = 8
NA_HEAD_DIM = 128
NA_KH = 8
NA_KW = 16
D_FF = 4 * D_MODEL
EPS = 1e-6
LOG2E = float(np.log2(np.e))

LANES = 128
MLA_QK_PAD = 2 * LANES
BF16_SUBLANES = 16
V_ROWS = V_HEAD + BF16_SUBLANES
NA_Q_ROWS = 8
NA_WIN_ROWS = 2 * NA_KH
NA_TQ = NA_Q_ROWS * GRID_W
NA_TK = NA_WIN_ROWS * GRID_W
NA_BLOCKS = SEQ // NA_TQ
NA_STRIP = NA_KH * GRID_W
NA_PAIR_ROWS = NA_STRIP + GRID_W
NA_PAIR_KINDS = 1 + NA_KH // 2
VMEM_LIMIT = 56 * 1024 * 1024

BF16 = jnp.bfloat16
F32 = jnp.float32
NT_DIMS = (((1,), (1,)), ((), ()))


def _params(n_axes):
    return pltpu.CompilerParams(
        dimension_semantics=("arbitrary",) * n_axes, vmem_limit_bytes=VMEM_LIMIT)


def _rms(x, g):
    return x * lax.rsqrt(jnp.mean(x * x, axis=-1, keepdims=True) + EPS) * g


def _layer_spec(shape, l, index_map):
    return pl.BlockSpec((None,) + tuple(shape), lambda *ids: (l,) + tuple(index_map(*ids)))


def _store_vt_with_ones(vt_ref, vt, heads):
    for h in range(heads):
        vt_ref[h * V_ROWS:h * V_ROWS + V_HEAD, :] = vt[h * V_HEAD:(h + 1) * V_HEAD, :]
        vt_ref[h * V_ROWS + V_HEAD:(h + 1) * V_ROWS, :] = jnp.ones((V_ROWS - V_HEAD, vt.shape[1]), BF16)


def _mm_kernel(a_ref, w_ref, o_ref):
    o_ref[...] = jnp.dot(a_ref[...], w_ref[...], preferred_element_type=F32).astype(o_ref.dtype)


def _mm(a, w, l, out_dtype, tm, tn, name):
    m, k = a.shape
    n = w.shape[2]
    return pl.pallas_call(
        _mm_kernel,
        grid=(m // tm, n // tn),
        in_specs=[pl.BlockSpec((tm, k), lambda i, j: (i, 0)),
                  _layer_spec((k, tn), l, lambda i, j: (0, j))],
        out_specs=pl.BlockSpec((tm, tn), lambda i, j: (i, j)),
        out_shape=jax.ShapeDtypeStruct((m, n), out_dtype),
        compiler_params=_params(2),
        name=name,
    )(a, w)


def _norm_proj_kernel(x_ref, g_ref, w_ref, u_ref, c_ref):
    u = _rms(x_ref[...], g_ref[...]).astype(BF16)
    u_ref[...] = u
    c_ref[...] = jnp.dot(u, w_ref[...], preferred_element_type=F32)


def _norm_proj(x, g, w, l, tm=512):
    s, d = x.shape
    n = w.shape[2]
    return pl.pallas_call(
        _norm_proj_kernel,
        grid=(s // tm,),
        in_specs=[pl.BlockSpec((tm, d), lambda i: (i, 0)),
                  _layer_spec((1, d), l, lambda i: (0, 0)),
                  _layer_spec((d, n), l, lambda i: (0, 0))],
        out_specs=[pl.BlockSpec((tm, d), lambda i: (i, 0)), pl.BlockSpec((tm, n), lambda i: (i, 0))],
        out_shape=[jax.ShapeDtypeStruct((s, d), BF16), jax.ShapeDtypeStruct((s, n), F32)],
        compiler_params=_params(1),
        name="norm_proj_c",
    )(x, g, w)


def _proj_vt_kernel(u_ref, wt_ref, vt_ref):
    vt = lax.dot_general(wt_ref[...], u_ref[...], NT_DIMS, preferred_element_type=F32).astype(BF16)
    _store_vt_with_ones(vt_ref, vt, NA_HEADS)


def _proj_vt(u, wt, l, tm=1024):
    s, k = u.shape
    return pl.pallas_call(
        _proj_vt_kernel,
        grid=(s // tm,),
        in_specs=[pl.BlockSpec((tm, k), lambda i: (i, 0)),
                  _layer_spec(wt.shape[1:], l, lambda i: (0, 0))],
        out_specs=pl.BlockSpec((NA_HEADS * V_ROWS, tm), lambda i: (0, i)),
        out_shape=jax.ShapeDtypeStruct((NA_HEADS * V_ROWS, s), BF16),
        compiler_params=_params(1),
        name="proj_na_vt",
    )(u, wt)


def _rope128(x, c, sl, sr):
    return x * c + pltpu.roll(x, LANES - QK_ROPE // 2, 1) * sl + pltpu.roll(x, QK_ROPE // 2, 1) * sr


def _mla_up_kernel(c_ref, gq_ref, gkv_ref, wqn_ref, wqp_ref, wkn_ref, wvt_ref,
                   cos_ref, sl_ref, sr_ref, q_ref, k_ref, vt_ref):
    scale = (QK_NOPE + QK_ROPE) ** -0.5 * LOG2E
    cq = _rms(c_ref[:, :Q_LORA], gq_ref[...]).astype(BF16)
    ckv = _rms(c_ref[:, Q_LORA:Q_LORA + KV_LORA], gkv_ref[...]).astype(BF16)
    kpe = c_ref[:, Q_LORA + KV_LORA:]
    cos, sl, sr = cos_ref[...], sl_ref[...], sr_ref[...]
    kpe_rot = _rope128(kpe, cos, sl, sr).astype(BF16)
    qn = jnp.dot(cq, wqn_ref[...], preferred_element_type=F32) * scale
    qp = jnp.dot(cq, wqp_ref[...], preferred_element_type=F32)
    kn = jnp.dot(ckv, wkn_ref[...], preferred_element_type=F32)
    vt = lax.dot_general(wvt_ref[...], ckv, NT_DIMS, preferred_element_type=F32).astype(BF16)
    _store_vt_with_ones(vt_ref, vt, MLA_HEADS)
    for h in range(MLA_HEADS):
        lo = h * MLA_QK_PAD
        hs = slice(h * LANES, (h + 1) * LANES)
        q_ref[:, lo:lo + LANES] = qn[:, hs].astype(BF16)
        q_ref[:, lo + LANES:lo + 2 * LANES] = (_rope128(qp[:, hs], cos, sl, sr) * scale).astype(BF16)
        k_ref[:, lo:lo + LANES] = kn[:, hs].astype(BF16)
        k_ref[:, lo + LANES:lo + 2 * LANES] = kpe_rot


def _mla_up(c, gq, gkv, wqn, wqp, wkn, wvt, l, cos_t, sl_t, sr_t, tm=512):
    s = c.shape[0]
    row = lambda w: pl.BlockSpec((tm, w), lambda i: (i, 0))
    full = lambda a: _layer_spec(a.shape[1:], l, lambda i: (0, 0))
    hw = MLA_HEADS * V_ROWS
    return pl.pallas_call(
        _mla_up_kernel,
        grid=(s // tm,),
        in_specs=[row(c.shape[1]), full(gq), full(gkv), full(wqn), full(wqp), full(wkn), full(wvt),
                  row(LANES), row(LANES), row(LANES)],
        out_specs=[row(MLA_HEADS * MLA_QK_PAD), row(MLA_HEADS * MLA_QK_PAD),
                   pl.BlockSpec((hw, tm), lambda i: (0, i))],
        out_shape=[jax.ShapeDtypeStruct((s, MLA_HEADS * MLA_QK_PAD), BF16),
                   jax.ShapeDtypeStruct((s, MLA_HEADS * MLA_QK_PAD), BF16),
                   jax.ShapeDtypeStruct((hw, s), BF16)],
        compiler_params=_params(1),
        name="mla_up",
    )(c, gq, gkv, wqn, wqp, wkn, wvt, cos_t, sl_t, sr_t)


def _mla_attn_kernel(q_ref, k_ref, vt_ref, o_ref, m_sc, acc_sc, sa_sc, sb_sc, ma_sc, mb_sc, *, tk):
    q = q_ref[...]
    n_chunks = k_ref.shape[0] // tk
    m_sc[...] = jnp.full(m_sc.shape, -jnp.inf, F32)
    acc_sc[...] = jnp.zeros(acc_sc.shape, F32)

    def scores(c, s_sc, cm_sc):
        k = k_ref[pl.ds(pl.multiple_of(c * tk, tk), tk), :]
        st = lax.dot_general(k, q, NT_DIMS, preferred_element_type=F32)
        s_sc[...] = st
        cm_sc[...] = jnp.max(st, axis=0, keepdims=True)

    def update(c, s_sc, cm_sc):
        vt = vt_ref[:, pl.ds(pl.multiple_of(c * tk, tk), tk)]
        m_prev = m_sc[...]
        m_new = jnp.maximum(m_prev, cm_sc[...])
        alpha = jnp.exp2(m_prev - m_new)
        pt = jnp.exp2(s_sc[...] - m_new)
        acc_sc[...] = alpha * acc_sc[...] + jnp.dot(vt, pt.astype(BF16), preferred_element_type=F32)
        m_sc[...] = m_new

    assert n_chunks % 2 == 0
    scores(0, sa_sc, ma_sc)

    def body(j, carry):
        c = 2 * j
        scores(c + 1, sb_sc, mb_sc)
        update(c, sa_sc, ma_sc)
        scores(c + 2, sa_sc, ma_sc)
        update(c + 1, sb_sc, mb_sc)
        return carry

    lax.fori_loop(0, n_chunks // 2 - 1, body, 0)
    scores(n_chunks - 1, sb_sc, mb_sc)
    update(n_chunks - 2, sa_sc, ma_sc)
    update(n_chunks - 1, sb_sc, mb_sc)
    o_ref[...] = (acc_sc[:V_HEAD, :] / acc_sc[V_HEAD:V_HEAD + 1, :]).T.astype(o_ref.dtype)


def _mla_attn(q, k, vt, tq=2048, tk=1024):
    s = q.shape[0]
    return pl.pallas_call(
        functools.partial(_mla_attn_kernel, tk=tk),
        grid=(MLA_HEADS, s // tq),
        in_specs=[pl.BlockSpec((tq, MLA_QK_PAD), lambda h, i: (i, h)),
                  pl.BlockSpec((s, MLA_QK_PAD), lambda h, i: (0, h)),
                  pl.BlockSpec((V_ROWS, s), lambda h, i: (h, 0))],
        out_specs=pl.BlockSpec((tq, V_HEAD), lambda h, i: (i, h)),
        out_shape=jax.ShapeDtypeStruct((s, MLA_HEADS * V_HEAD), BF16),
        scratch_shapes=[pltpu.VMEM((1, tq), F32), pltpu.VMEM((V_ROWS, tq), F32),
                        pltpu.VMEM((tk, tq), F32), pltpu.VMEM((tk, tq), F32),
                        pltpu.VMEM((1, tq), F32), pltpu.VMEM((1, tq), F32)],
        compiler_params=_params(2),
        name="mla_attn",
    )(q, k, vt)


def _na_row_start(r):
    return min(max(r - NA_KH // 2, 0), ROWS - NA_KH)


def _na_win_start(blk):
    return min(max(blk * NA_Q_ROWS - NA_KH // 2, 0), ROWS - NA_WIN_ROWS)


def _na_pair_placement(blk, pair):
    r_e = blk * NA_Q_ROWS + 2 * pair
    r_o = r_e + 1
    off_e = _na_row_start(r_e) - _na_win_start(blk)
    off_o = _na_row_start(r_o) - _na_win_start(blk)
    dy_e = _na_row_start(r_e) - r_e + (NA_KH - 1)
    dy_o = _na_row_start(r_o) - r_o + (NA_KH - 1)
    if off_o == off_e + 1:
        assert dy_e == dy_o == NA_KH // 2 - 1
        return 0, off_e * GRID_W, NA_PAIR_ROWS
    assert off_o == off_e and dy_e == dy_o + 1 and dy_e % 2 == 1
    return 1 + (NA_KH - 1 - dy_e) // 2, off_e * GRID_W, NA_STRIP


def _na_pair_tables(rpb_l):
    c = np.arange(GRID_W)
    col_start = np.clip(c - NA_KW // 2, 0, GRID_W - NA_KW)
    col_ok = (c[None, :] >= col_start[:, None]) & (c[None, :] < col_start[:, None] + NA_KW)
    dx = np.clip(c[None, :] - c[:, None], -(NA_KW - 1), NA_KW - 1) + (NA_KW - 1)
    rpb_l = rpb_l.astype(F32)
    toe = jnp.full((NA_HEADS, 2 * NA_KH - 1, GRID_W, GRID_W), -jnp.inf, F32)
    for d in range(2 * NA_KW - 1):
        toe = jnp.where(((dx == d) & col_ok).T, rpb_l[:, :, d][:, :, None, None], toe)

    def strip(dy0, lo, hi):
        t = toe[:, dy0:dy0 + NA_KH].reshape(NA_HEADS, NA_STRIP, GRID_W)
        return jnp.pad(t, ((0, 0), (lo, hi), (0, 0)), constant_values=-jnp.inf)

    mid = NA_KH // 2 - 1
    kinds = [jnp.concatenate([strip(mid, 0, GRID_W), strip(mid, GRID_W, 0)], axis=2)]
    for j in range(NA_KH // 2):
        dy_e = NA_KH - 1 - 2 * j
        kinds.append(jnp.concatenate([strip(dy_e, 0, GRID_W), strip(dy_e - 1, 0, GRID_W)], axis=2))
    return jnp.stack(kinds, axis=1)


def _na_attn_kernel(q_ref, k_ref, vt_ref, t_ref, o_ref, bias_sc, sa_sc, sb_sc, ma_sc, mb_sc):
    last = NA_BLOCKS - 1
    bias_sc[...] = jnp.full(bias_sc.shape, -jnp.inf, F32)
    for var, blk in enumerate((0, 1, last)):
        for pair in range(NA_Q_ROWS // 2):
            kind, top, rows = _na_pair_placement(blk, pair)
            bias_sc[var, top:top + rows, pair * LANES:(pair + 1) * LANES] = t_ref[0, kind, :rows, :] * LOG2E

    def win_start(blk):
        start = jnp.clip(blk * NA_TQ - (NA_KH // 2) * GRID_W, 0, SEQ - NA_TK)
        return pl.multiple_of(start, (NA_KH // 2) * GRID_W)

    def scores(blk, var, s_sc, cm_sc):
        q = q_ref[pl.ds(pl.multiple_of(blk * NA_TQ, NA_TQ), NA_TQ), :]
        k = k_ref[pl.ds(win_start(blk), NA_TK), :]
        st = lax.dot_general(k, q, NT_DIMS, preferred_element_type=F32)
        st = st * (NA_HEAD_DIM ** -0.5 * LOG2E) + bias_sc[var]
        s_sc[...] = st
        cm_sc[...] = jnp.max(st, axis=0, keepdims=True)

    def finish(blk, s_sc, cm_sc):
        vt = vt_ref[:, pl.ds(win_start(blk), NA_TK)]
        pt = jnp.exp2(s_sc[...] - cm_sc[...])
        acc = jnp.dot(vt, pt.astype(BF16), preferred_element_type=F32)
        o = (acc[:V_HEAD, :] / acc[V_HEAD:V_HEAD + 1, :]).T
        o_ref[pl.ds(pl.multiple_of(blk * NA_TQ, NA_TQ), NA_TQ), :] = o.astype(o_ref.dtype)

    assert NA_BLOCKS % 2 == 0
    scores(0, 0, sa_sc, ma_sc)

    def body(j, carry):
        b = 2 * j
        scores(b + 1, 1, sb_sc, mb_sc)
        finish(b, sa_sc, ma_sc)
        scores(b + 2, 1, sa_sc, ma_sc)
        finish(b + 1, sb_sc, mb_sc)
        return carry

    lax.fori_loop(0, NA_BLOCKS // 2 - 1, body, 0)
    scores(last, 2, sb_sc, mb_sc)
    finish(last - 1, sa_sc, ma_sc)
    finish(last, sb_sc, mb_sc)


def _na_attn(qk, vt, tables):
    s = qk.shape[0]
    return pl.pallas_call(
        _na_attn_kernel,
        grid=(NA_HEADS,),
        in_specs=[pl.BlockSpec((s, NA_HEAD_DIM), lambda h: (0, h)),
                  pl.BlockSpec((s, NA_HEAD_DIM), lambda h: (0, NA_HEADS + h)),
                  pl.BlockSpec((V_ROWS, s), lambda h: (h, 0)),
                  pl.BlockSpec((1, NA_PAIR_KINDS, NA_PAIR_ROWS, LANES), lambda h: (h, 0, 0, 0))],
        out_specs=pl.BlockSpec((s, NA_HEAD_DIM), lambda h: (0, h)),
        out_shape=jax.ShapeDtypeStruct((s, NA_HEADS * NA_HEAD_DIM), BF16),
        scratch_shapes=[pltpu.VMEM((3, NA_TK, NA_TQ), F32),
                        pltpu.VMEM((NA_TK, NA_TQ), F32), pltpu.VMEM((NA_TK, NA_TQ), F32),
                        pltpu.VMEM((1, NA_TQ), F32), pltpu.VMEM((1, NA_TQ), F32)],
        compiler_params=_params(1),
        name="na_attn",
    )(qk, qk, vt, tables)


def _merge_kernel(ya_ref, yb_ref, g_ref, x_ref, woa_ref, wob_ref, wo_ref, o_ref):
    a = jnp.dot(ya_ref[...], woa_ref[...], preferred_element_type=F32)
    b = jnp.dot(yb_ref[...], wob_ref[...], preferred_element_type=F32)
    ga = jax.nn.sigmoid(g_ref[:, :D_MODEL])
    gb = jax.nn.sigmoid(g_ref[:, D_MODEL:])
    merged = (ga * a + gb * b).astype(BF16)
    o_ref[...] = x_ref[...] + jnp.dot(merged, wo_ref[...], preferred_element_type=F32)


def _merge(ya, yb, gates, x, woa, wob, wo, l, tm=256):
    s = ya.shape[0]
    row = lambda w: pl.BlockSpec((tm, w), lambda i: (i, 0))
    full = lambda a: _layer_spec(a.shape[1:], l, lambda i: (0, 0))
    return pl.pallas_call(
        _merge_kernel,
        grid=(s // tm,),
        in_specs=[row(ya.shape[1]), row(yb.shape[1]), row(gates.shape[1]), row(D_MODEL),
                  full(woa), full(wob), full(wo)],
        out_specs=row(D_MODEL),
        out_shape=jax.ShapeDtypeStruct((s, D_MODEL), F32),
        compiler_params=_params(1),
        name="merge_out",
    )(ya, yb, gates, x, woa, wob, wo)


def _ffn_kernel(x_ref, g_ref, w1_ref, w2_ref, gf_ref, o_ref, u_sc, *, final_norm):
    f = pl.program_id(1)

    @pl.when(f == 0)
    def _init():
        x = x_ref[...]
        u_sc[...] = _rms(x, g_ref[...]).astype(BF16)
        o_ref[...] = x

    h = jnp.dot(u_sc[...], w1_ref[...], preferred_element_type=F32)
    a = jnp.square(jnp.maximum(h, 0.0)).astype(BF16)
    o_ref[...] += jnp.dot(a, w2_ref[...], preferred_element_type=F32)

    if final_norm:
        @pl.when(f == pl.num_programs(1) - 1)
        def _final():
            o_ref[...] = _rms(o_ref[...], gf_ref[...])


def _ffn(x, g, w1, w2, l, g_final, final_norm, tm=512, tf=1024):
    s, d = x.shape
    return pl.pallas_call(
        functools.partial(_ffn_kernel, final_norm=final_norm),
        grid=(s // tm, w1.shape[2] // tf),
        in_specs=[pl.BlockSpec((tm, d), lambda i, f: (i, 0)),
                  _layer_spec((1, d), l, lambda i, f: (0, 0)),
                  _layer_spec((d, tf), l, lambda i, f: (0, f)),
                  _layer_spec((tf, d), l, lambda i, f: (f, 0)),
                  pl.BlockSpec((1, d), lambda i, f: (0, 0))],
        out_specs=pl.BlockSpec((tm, d), lambda i, f: (i, 0)),
        out_shape=jax.ShapeDtypeStruct((s, d), F32),
        scratch_shapes=[pltpu.VMEM((tm, d), BF16)],
        compiler_params=_params(2),
        name="ffn",
    )(x, g, w1, w2, g_final)


def _rope_tables(s):
    pos = jnp.arange(s, dtype=F32)
    inv_freq = 1.0 / (ROPE_THETA ** (jnp.arange(0, QK_ROPE, 2, dtype=F32) / QK_ROPE))
    ang = pos[:, None] * inv_freq[None, :]
    cos, sin = jnp.cos(ang), jnp.sin(ang)
    half = QK_ROPE // 2
    z = lambda w: jnp.zeros((s, w), F32)
    cos_t = jnp.concatenate([cos, cos, z(LANES - QK_ROPE)], axis=1)
    sl_t = jnp.concatenate([-sin, z(LANES - half)], axis=1)
    sr_t = jnp.concatenate([z(half), sin, z(LANES - QK_ROPE)], axis=1)
    return cos_t, sl_t, sr_t


def kernel(x, norm_mix, w_in, norm_qa, w_uq, norm_kva, w_ukv, rpb, w_o_mla, w_o_na, w_out,
           norm_mlp, w_ff1, w_ff2, norm_final):
    b, s, d = x.shape
    assert (b, s, d) == (1, SEQ, D_MODEL)
    depth = w_in.shape[0]
    cos_t, sl_t, sr_t = _rope_tables(s)

    c_end = Q_LORA + KV_LORA + QK_ROPE
    na_w = NA_HEADS * NA_HEAD_DIM
    w_c = jnp.pad(w_in[:, :, :c_end], ((0, 0), (0, 0), (0, LANES - QK_ROPE))).astype(BF16)
    w_na_qk = w_in[:, :, c_end:c_end + 2 * na_w].astype(BF16)
    w_na_vt = w_in[:, :, c_end + 2 * na_w:c_end + 3 * na_w].transpose(0, 2, 1).astype(BF16)
    w_g = w_in[:, :, c_end + 3 * na_w:].astype(BF16)
    uq = w_uq.reshape(depth, Q_LORA, MLA_HEADS, QK_NOPE + QK_ROPE)
    w_qn = uq[..., :QK_NOPE].reshape(depth, Q_LORA, MLA_HEADS * QK_NOPE).astype(BF16)
    w_qp = jnp.pad(uq[..., QK_NOPE:], ((0, 0), (0, 0), (0, 0), (0, LANES - QK_ROPE)))
    w_qp = w_qp.reshape(depth, Q_LORA, MLA_HEADS * LANES).astype(BF16)
    ukv = w_ukv.reshape(depth, KV_LORA, MLA_HEADS, QK_NOPE + V_HEAD)
    w_kn = ukv[..., :QK_NOPE].reshape(depth, KV_LORA, MLA_HEADS * QK_NOPE).astype(BF16)
    w_vt = ukv[..., QK_NOPE:].reshape(depth, KV_LORA, MLA_HEADS * V_HEAD).transpose(0, 2, 1).astype(BF16)
    w_oa = w_o_mla.astype(BF16)
    w_ob = w_o_na.astype(BF16)
    w_o = w_out.astype(BF16)
    w_1 = w_ff1.astype(BF16)
    w_2 = w_ff2.astype(BF16)
    g_qa = norm_qa.reshape(depth, 1, Q_LORA)
    g_kva = norm_kva.reshape(depth, 1, KV_LORA)
    g_mlp = norm_mlp.reshape(depth, 1, d)
    g_mix = norm_mix.reshape(depth, 1, d)
    g_final = norm_final.reshape(1, d)

    xs = x.reshape(s, d)
    for l in range(depth):
        u, c = _norm_proj(xs, g_mix, w_c, l)
        qk_na = _mm(u, w_na_qk, l, BF16, 1024, 1024, name="proj_na_qk")
        vt_na = _proj_vt(u, w_na_vt, l)
        gates = _mm(u, w_g, l, F32, 1024, 1024, name="proj_gate")
        q, k, vt = _mla_up(c, g_qa, g_kva, w_qn, w_qp, w_kn, w_vt, l, cos_t, sl_t, sr_t)
        y_a = _mla_attn(q, k, vt)
        y_b = _na_attn(qk_na, vt_na, _na_pair_tables(rpb[l]))
        xs = _merge(y_a, y_b, gates, xs, w_oa, w_ob, w_o, l)
        xs = _ffn(xs, g_mlp, w_1, w_2, l, g_final, final_norm=(l == depth - 1))
    return xs.reshape(b, s, d)
```

```python
import functools

import jax
import jax.numpy as jnp
import numpy as np
from jax import lax
from jax.experimental import pallas as pl
from jax.experimental.pallas import tpu as pltpu

D_MODEL = 2048
SEQ = 8192
GRID_W = 64
ROWS = SEQ // GRID_W
MLA_HEADS = 8
Q_LORA = 512
KV_LORA = 512
QK_NOPE = 128
QK_ROPE = 64
V_HEAD = 128
ROPE_THETA = 10000.0
NA_HEADS = 8
NA_HEAD_DIM = 128
NA_KH = 8
NA_KW = 16
D_FF = 4 * D_MODEL
EPS = 1e-6
LOG2E = float(np.log2(np.e))

LANES = 128
MLA_QK_PAD = 2 * LANES
BF16_SUBLANES = 16
V_ROWS = V_HEAD + BF16_SUBLANES
NA_Q_ROWS = 8
NA_WIN_ROWS = 2 * NA_KH
NA_TQ = NA_Q_ROWS * GRID_W
NA_TK = NA_WIN_ROWS * GRID_W
NA_BLOCKS = SEQ // NA_TQ
NA_STRIP = NA_KH * GRID_W
NA_PAIR_ROWS = NA_STRIP + GRID_W
NA_PAIR_KINDS = 1 + NA_KH // 2
VMEM_LIMIT = 56 * 1024 * 1024

BF16 = jnp.bfloat16
F32 = jnp.float32
NT_DIMS = (((1,), (1,)), ((), ()))


def _params(n_axes):
    return pltpu.CompilerParams(
        dimension_semantics=("arbitrary",) * n_axes, vmem_limit_bytes=VMEM_LIMIT)


def _rms(x, g):
    return x * lax.rsqrt(jnp.mean(x * x, axis=-1, keepdims=True) + EPS) * g


def _layer_spec(shape, l, index_map):
    return pl.BlockSpec((None,) + tuple(shape), lambda *ids: (l,) + tuple(index_map(*ids)))


def _store_vt_with_ones(vt_ref, vt, heads):
    for h in range(heads):
        vt_ref[h * V_ROWS:h * V_ROWS + V_HEAD, :] = vt[h * V_HEAD:(h + 1) * V_HEAD, :]
        vt_ref[h * V_ROWS + V_HEAD:(h + 1) * V_ROWS, :] = jnp.ones((V_ROWS - V_HEAD, vt.shape[1]), BF16)


C_END = Q_LORA + KV_LORA + QK_ROPE
C_PAD = C_END + LANES - QK_ROPE
NA_W = NA_HEADS * NA_HEAD_DIM


def _split_w_in_kernel(w_ref, c_ref, qk_ref, vt_ref, g_ref):
    lane = lax.broadcasted_iota(jnp.int32, c_ref.shape, 1)
    c_ref[...] = jnp.where(lane < C_END, w_ref[:, :C_PAD], 0.0).astype(BF16)
    qk_ref[...] = w_ref[:, C_END:C_END + 2 * NA_W].astype(BF16)
    vt_ref[...] = w_ref[:, C_END + 2 * NA_W:C_END + 3 * NA_W].T.astype(BF16)
    g_ref[...] = w_ref[:, C_END + 3 * NA_W:].astype(BF16)


def _split_w_in(w_in, tk=256):
    depth, k, n = w_in.shape
    blk = lambda w: pl.BlockSpec((None, tk, w), lambda l, i: (l, i, 0))
    return pl.pallas_call(
        _split_w_in_kernel,
        grid=(depth, k // tk),
        in_specs=[blk(n)],
        out_specs=[blk(C_PAD), blk(2 * NA_W), pl.BlockSpec((None, NA_W, tk), lambda l, i: (l, 0, i)),
                   blk(2 * D_MODEL)],
        out_shape=[jax.ShapeDtypeStruct((depth, k, C_PAD), BF16),
                   jax.ShapeDtypeStruct((depth, k, 2 * NA_W), BF16),
                   jax.ShapeDtypeStruct((depth, NA_W, k), BF16),
                   jax.ShapeDtypeStruct((depth, k, 2 * D_MODEL), BF16)],
        compiler_params=_params(2),
        name="split_w_in",
    )(w_in)


def _mm_kernel(a_ref, w_ref, o_ref):
    o_ref[...] = jnp.dot(a_ref[...], w_ref[...], preferred_element_type=F32).astype(o_ref.dtype)


def _mm(a, w, l, out_dtype, tm, tn, name):
    m, k = a.shape
    n = w.shape[2]
    return pl.pallas_call(
        _mm_kernel,
        grid=(m // tm, n // tn),
        in_specs=[pl.BlockSpec((tm, k), lambda i, j: (i, 0)),
                  _layer_spec((k, tn), l, lambda i, j: (0, j))],
        out_specs=pl.BlockSpec((tm, tn), lambda i, j: (i, j)),
        out_shape=jax.ShapeDtypeStruct((m, n), out_dtype),
        compiler_params=_params(2),
        name=name,
    )(a, w)


def _norm_proj_kernel(x_ref, g_ref, w_ref, u_ref, c_ref):
    u = _rms(x_ref[...], g_ref[...]).astype(BF16)
    u_ref[...] = u
    c_ref[...] = jnp.dot(u, w_ref[...], preferred_element_type=F32)


def _norm_proj(x, g, w, l, tm=512):
    s, d = x.shape
    n = w.shape[2]
    return pl.pallas_call(
        _norm_proj_kernel,
        grid=(s // tm,),
        in_specs=[pl.BlockSpec((tm, d), lambda i: (i, 0)),
                  _layer_spec((1, d), l, lambda i: (0, 0)),
                  _layer_spec((d, n), l, lambda i: (0, 0))],
        out_specs=[pl.BlockSpec((tm, d), lambda i: (i, 0)), pl.BlockSpec((tm, n), lambda i: (i, 0))],
        out_shape=[jax.ShapeDtypeStruct((s, d), BF16), jax.ShapeDtypeStruct((s, n), F32)],
        compiler_params=_params(1),
        name="norm_proj_c",
    )(x, g, w)


def _proj_vt_kernel(u_ref, wt_ref, vt_ref):
    vt = lax.dot_general(wt_ref[...], u_ref[...], NT_DIMS, preferred_element_type=F32).astype(BF16)
    _store_vt_with_ones(vt_ref, vt, NA_HEADS)


def _proj_vt(u, wt, l, tm=1024):
    s, k = u.shape
    return pl.pallas_call(
        _proj_vt_kernel,
        grid=(s // tm,),
        in_specs=[pl.BlockSpec((tm, k), lambda i: (i, 0)),
                  _layer_spec(wt.shape[1:], l, lambda i: (0, 0))],
        out_specs=pl.BlockSpec((NA_HEADS * V_ROWS, tm), lambda i: (0, i)),
        out_shape=jax.ShapeDtypeStruct((NA_HEADS * V_ROWS, s), BF16),
        compiler_params=_params(1),
        name="proj_na_vt",
    )(u, wt)


def _rope128(x, c, sl, sr):
    return x * c + pltpu.roll(x, LANES - QK_ROPE // 2, 1) * sl + pltpu.roll(x, QK_ROPE // 2, 1) * sr


def _mla_up_kernel(c_ref, gq_ref, gkv_ref, wqn_ref, wqp_ref, wkn_ref, wvt_ref,
                   cos_ref, sl_ref, sr_ref, q_ref, k_ref, vt_ref):
    scale = (QK_NOPE + QK_ROPE) ** -0.5 * LOG2E
    cq = _rms(c_ref[:, :Q_LORA], gq_ref[...]).astype(BF16)
    ckv = _rms(c_ref[:, Q_LORA:Q_LORA + KV_LORA], gkv_ref[...]).astype(BF16)
    kpe = c_ref[:, Q_LORA + KV_LORA:]
    cos, sl, sr = cos_ref[...], sl_ref[...], sr_ref[...]
    kpe_rot = _rope128(kpe, cos, sl, sr).astype(BF16)
    qn = jnp.dot(cq, wqn_ref[...], preferred_element_type=F32) * scale
    qp = jnp.dot(cq, wqp_ref[...], preferred_element_type=F32)
    kn = jnp.dot(ckv, wkn_ref[...], preferred_element_type=F32)
    vt = lax.dot_general(wvt_ref[...], ckv, NT_DIMS, preferred_element_type=F32).astype(BF16)
    _store_vt_with_ones(vt_ref, vt, MLA_HEADS)
    for h in range(MLA_HEADS):
        lo = h * MLA_QK_PAD
        hs = slice(h * LANES, (h + 1) * LANES)
        q_ref[:, lo:lo + LANES] = qn[:, hs].astype(BF16)
        q_ref[:, lo + LANES:lo + 2 * LANES] = (_rope128(qp[:, hs], cos, sl, sr) * scale).astype(BF16)
        k_ref[:, lo:lo + LANES] = kn[:, hs].astype(BF16)
        k_ref[:, lo + LANES:lo + 2 * LANES] = kpe_rot


def _mla_up(c, gq, gkv, wqn, wqp, wkn, wvt, l, cos_t, sl_t, sr_t, tm=512):
    s = c.shape[0]
    row = lambda w: pl.BlockSpec((tm, w), lambda i: (i, 0))
    full = lambda a: _layer_spec(a.shape[1:], l, lambda i: (0, 0))
    hw = MLA_HEADS * V_ROWS
    return pl.pallas_call(
        _mla_up_kernel,
        grid=(s // tm,),
        in_specs=[row(c.shape[1]), full(gq), full(gkv), full(wqn), full(wqp), full(wkn), full(wvt),
                  row(LANES), row(LANES), row(LANES)],
        out_specs=[row(MLA_HEADS * MLA_QK_PAD), row(MLA_HEADS * MLA_QK_PAD),
                   pl.BlockSpec((hw, tm), lambda i: (0, i))],
        out_shape=[jax.ShapeDtypeStruct((s, MLA_HEADS * MLA_QK_PAD), BF16),
                   jax.ShapeDtypeStruct((s, MLA_HEADS * MLA_QK_PAD), BF16),
                   jax.ShapeDtypeStruct((hw, s), BF16)],
        compiler_params=_params(1),
        name="mla_up",
    )(c, gq, gkv, wqn, wqp, wkn, wvt, cos_t, sl_t, sr_t)


def _mla_attn_kernel(q_ref, k_ref, vt_ref, o_ref, m_sc, acc_sc, sa_sc, sb_sc, ma_sc, mb_sc, *, tk):
    q = q_ref[...]
    n_chunks = k_ref.shape[0] // tk
    m_sc[...] = jnp.full(m_sc.shape, -jnp.inf, F32)
    acc_sc[...] = jnp.zeros(acc_sc.shape, F32)

    def scores(c, s_sc, cm_sc):
        k = k_ref[pl.ds(pl.multiple_of(c * tk, tk), tk), :]
        st = lax.dot_general(k, q, NT_DIMS, preferred_element_type=F32)
        s_sc[...] = st
        cm_sc[...] = jnp.max(st, axis=0, keepdims=True)

    def update(c, s_sc, cm_sc):
        vt = vt_ref[:, pl.ds(pl.multiple_of(c * tk, tk), tk)]
        m_prev = m_sc[...]
        m_new = jnp.maximum(m_prev, cm_sc[...])
        alpha = jnp.exp2(m_prev - m_new)
        pt = jnp.exp2(s_sc[...] - m_new)
        acc_sc[...] = alpha * acc_sc[...] + jnp.dot(vt, pt.astype(BF16), preferred_element_type=F32)
        m_sc[...] = m_new

    assert n_chunks % 2 == 0
    scores(0, sa_sc, ma_sc)

    def body(j, carry):
        c = 2 * j
        scores(c + 1, sb_sc, mb_sc)
        update(c, sa_sc, ma_sc)
        scores(c + 2, sa_sc, ma_sc)
        update(c + 1, sb_sc, mb_sc)
        return carry

    lax.fori_loop(0, n_chunks // 2 - 1, body, 0)
    scores(n_chunks - 1, sb_sc, mb_sc)
    update(n_chunks - 2, sa_sc, ma_sc)
    update(n_chunks - 1, sb_sc, mb_sc)
    o_ref[...] = (acc_sc[:V_HEAD, :] / acc_sc[V_HEAD:V_HEAD + 1, :]).T.astype(o_ref.dtype)


def _mla_attn(q, k, vt, tq=2048, tk=1024):
    s = q.shape[0]
    return pl.pallas_call(
        functools.partial(_mla_attn_kernel, tk=tk),
        grid=(MLA_HEADS, s // tq),
        in_specs=[pl.BlockSpec((tq, MLA_QK_PAD), lambda h, i: (i, h)),
                  pl.BlockSpec((s, MLA_QK_PAD), lambda h, i: (0, h)),
                  pl.BlockSpec((V_ROWS, s), lambda h, i: (h, 0))],
        out_specs=pl.BlockSpec((tq, V_HEAD), lambda h, i: (i, h)),
        out_shape=jax.ShapeDtypeStruct((s, MLA_HEADS * V_HEAD), BF16),
        scratch_shapes=[pltpu.VMEM((1, tq), F32), pltpu.VMEM((V_ROWS, tq), F32),
                        pltpu.VMEM((tk, tq), F32), pltpu.VMEM((tk, tq), F32),
                        pltpu.VMEM((1, tq), F32), pltpu.VMEM((1, tq), F32)],
        compiler_params=_params(2),
        name="mla_attn",
    )(q, k, vt)


def _na_row_start(r):
    return min(max(r - NA_KH // 2, 0), ROWS - NA_KH)


def _na_win_start(blk):
    return min(max(blk * NA_Q_ROWS - NA_KH // 2, 0), ROWS - NA_WIN_ROWS)


def _na_pair_placement(blk, pair):
    r_e = blk * NA_Q_ROWS + 2 * pair
    r_o = r_e + 1
    off_e = _na_row_start(r_e) - _na_win_start(blk)
    off_o = _na_row_start(r_o) - _na_win_start(blk)
    dy_e = _na_row_start(r_e) - r_e + (NA_KH - 1)
    dy_o = _na_row_start(r_o) - r_o + (NA_KH - 1)
    if off_o == off_e + 1:
        assert dy_e == dy_o == NA_KH // 2 - 1
        return 0, off_e * GRID_W, NA_PAIR_ROWS
    assert off_o == off_e and dy_e == dy_o + 1 and dy_e % 2 == 1
    return 1 + (NA_KH - 1 - dy_e) // 2, off_e * GRID_W, NA_STRIP


def _na_pair_tables(rpb_l):
    c = np.arange(GRID_W)
    col_start = np.clip(c - NA_KW // 2, 0, GRID_W - NA_KW)
    col_ok = (c[None, :] >= col_start[:, None]) & (c[None, :] < col_start[:, None] + NA_KW)
    dx = np.clip(c[None, :] - c[:, None], -(NA_KW - 1), NA_KW - 1) + (NA_KW - 1)
    rpb_l = rpb_l.astype(F32)
    toe = jnp.full((NA_HEADS, 2 * NA_KH - 1, GRID_W, GRID_W), -jnp.inf, F32)
    for d in range(2 * NA_KW - 1):
        toe = jnp.where(((dx == d) & col_ok).T, rpb_l[:, :, d][:, :, None, None], toe)

    def strip(dy0, lo, hi):
        t = toe[:, dy0:dy0 + NA_KH].reshape(NA_HEADS, NA_STRIP, GRID_W)
        return jnp.pad(t, ((0, 0), (lo, hi), (0, 0)), constant_values=-jnp.inf)

    mid = NA_KH // 2 - 1
    kinds = [jnp.concatenate([strip(mid, 0, GRID_W), strip(mid, GRID_W, 0)], axis=2)]
    for j in range(NA_KH // 2):
        dy_e = NA_KH - 1 - 2 * j
        kinds.append(jnp.concatenate([strip(dy_e, 0, GRID_W), strip(dy_e - 1, 0, GRID_W)], axis=2))
    return jnp.stack(kinds, axis=1)


def _na_attn_kernel(q_ref, k_ref, vt_ref, t_ref, o_ref, bias_sc, sa_sc, sb_sc, ma_sc, mb_sc):
    last = NA_BLOCKS - 1
    bias_sc[...] = jnp.full(bias_sc.shape, -jnp.inf, F32)
    for var, blk in enumerate((0, 1, last)):
        for pair in range(NA_Q_ROWS // 2):
            kind, top, rows = _na_pair_placement(blk, pair)
            bias_sc[var, top:top + rows, pair * LANES:(pair + 1) * LANES] = t_ref[0, kind, :rows, :] * LOG2E

    def win_start(blk):
        start = jnp.clip(blk * NA_TQ - (NA_KH // 2) * GRID_W, 0, SEQ - NA_TK)
        return pl.multiple_of(start, (NA_KH // 2) * GRID_W)

    def scores(blk, var, s_sc, cm_sc):
        q = q_ref[pl.ds(pl.multiple_of(blk * NA_TQ, NA_TQ), NA_TQ), :]
        k = k_ref[pl.ds(win_start(blk), NA_TK), :]
        st = lax.dot_general(k, q, NT_DIMS, preferred_element_type=F32)
        st = st * (NA_HEAD_DIM ** -0.5 * LOG2E) + bias_sc[var]
        s_sc[...] = st
        cm_sc[...] = jnp.max(st, axis=0, keepdims=True)

    def finish(blk, s_sc, cm_sc):
        vt = vt_ref[:, pl.ds(win_start(blk), NA_TK)]
        pt = jnp.exp2(s_sc[...] - cm_sc[...])
        acc = jnp.dot(vt, pt.astype(BF16), preferred_element_type=F32)
        o = (acc[:V_HEAD, :] / acc[V_HEAD:V_HEAD + 1, :]).T
        o_ref[pl.ds(pl.multiple_of(blk * NA_TQ, NA_TQ), NA_TQ), :] = o.astype(o_ref.dtype)

    assert NA_BLOCKS % 2 == 0
    scores(0, 0, sa_sc, ma_sc)

    def body(j, carry):
        b = 2 * j
        scores(b + 1, 1, sb_sc, mb_sc)
        finish(b, sa_sc, ma_sc)
        scores(b + 2, 1, sa_sc, ma_sc)
        finish(b + 1, sb_sc, mb_sc)
        return carry

    lax.fori_loop(0, NA_BLOCKS // 2 - 1, body, 0)
    scores(last, 2, sb_sc, mb_sc)
    finish(last - 1, sa_sc, ma_sc)
    finish(last, sb_sc, mb_sc)


def _na_attn(qk, vt, tables):
    s = qk.shape[0]
    return pl.pallas_call(
        _na_attn_kernel,
        grid=(NA_HEADS,),
        in_specs=[pl.BlockSpec((s, NA_HEAD_DIM), lambda h: (0, h)),
                  pl.BlockSpec((s, NA_HEAD_DIM), lambda h: (0, NA_HEADS + h)),
                  pl.BlockSpec((V_ROWS, s), lambda h: (h, 0)),
                  pl.BlockSpec((1, NA_PAIR_KINDS, NA_PAIR_ROWS, LANES), lambda h: (h, 0, 0, 0))],
        out_specs=pl.BlockSpec((s, NA_HEAD_DIM), lambda h: (0, h)),
        out_shape=jax.ShapeDtypeStruct((s, NA_HEADS * NA_HEAD_DIM), BF16),
        scratch_shapes=[pltpu.VMEM((3, NA_TK, NA_TQ), F32),
                        pltpu.VMEM((NA_TK, NA_TQ), F32), pltpu.VMEM((NA_TK, NA_TQ), F32),
                        pltpu.VMEM((1, NA_TQ), F32), pltpu.VMEM((1, NA_TQ), F32)],
        compiler_params=_params(1),
        name="na_attn",
    )(qk, qk, vt, tables)


def _merge_kernel(ya_ref, yb_ref, g_ref, x_ref, woa_ref, wob_ref, wo_ref, o_ref):
    a = jnp.dot(ya_ref[...], woa_ref[...], preferred_element_type=F32)
    b = jnp.dot(yb_ref[...], wob_ref[...], preferred_element_type=F32)
    ga = jax.nn.sigmoid(g_ref[:, :D_MODEL])
    gb = jax.nn.sigmoid(g_ref[:, D_MODEL:])
    merged = (ga * a + gb * b).astype(BF16)
    o_ref[...] = x_ref[...] + jnp.dot(merged, wo_ref[...], preferred_element_type=F32)


def _merge(ya, yb, gates, x, woa, wob, wo, l, tm=256):
    s = ya.shape[0]
    row = lambda w: pl.BlockSpec((tm, w), lambda i: (i, 0))
    full = lambda a: _layer_spec(a.shape[1:], l, lambda i: (0, 0))
    return pl.pallas_call(
        _merge_kernel,
        grid=(s // tm,),
        in_specs=[row(ya.shape[1]), row(yb.shape[1]), row(gates.shape[1]), row(D_MODEL),
                  full(woa), full(wob), full(wo)],
        out_specs=row(D_MODEL),
        out_shape=jax.ShapeDtypeStruct((s, D_MODEL), F32),
        compiler_params=_params(1),
        name="merge_out",
    )(ya, yb, gates, x, woa, wob, wo)


def _ffn_kernel(x_ref, g_ref, w1_ref, w2_ref, gf_ref, o_ref, u_sc, *, final_norm):
    f = pl.program_id(1)

    @pl.when(f == 0)
    def _init():
        x = x_ref[...]
        u_sc[...] = _rms(x, g_ref[...]).astype(BF16)
        o_ref[...] = x

    h = jnp.dot(u_sc[...], w1_ref[...], preferred_element_type=F32)
    a = jnp.square(jnp.maximum(h, 0.0)).astype(BF16)
    o_ref[...] += jnp.dot(a, w2_ref[...], preferred_element_type=F32)

    if final_norm:
        @pl.when(f == pl.num_programs(1) - 1)
        def _final():
            o_ref[...] = _rms(o_ref[...], gf_ref[...])


def _ffn(x, g, w1, w2, l, g_final, final_norm, tm=512, tf=1024):
    s, d = x.shape
    return pl.pallas_call(
        functools.partial(_ffn_kernel, final_norm=final_norm),
        grid=(s // tm, w1.shape[2] // tf),
        in_specs=[pl.BlockSpec((tm, d), lambda i, f: (i, 0)),
                  _layer_spec((1, d), l, lambda i, f: (0, 0)),
                  _layer_spec((d, tf), l, lambda i, f: (0, f)),
                  _layer_spec((tf, d), l, lambda i, f: (f, 0)),
                  pl.BlockSpec((1, d), lambda i, f: (0, 0))],
        out_specs=pl.BlockSpec((tm, d), lambda i, f: (i, 0)),
        out_shape=jax.ShapeDtypeStruct((s, d), F32),
        scratch_shapes=[pltpu.VMEM((tm, d), BF16)],
        compiler_params=_params(2),
        name="ffn",
    )(x, g, w1, w2, g_final)


def _rope_tables(s):
    pos = jnp.arange(s, dtype=F32)
    inv_freq = 1.0 / (ROPE_THETA ** (jnp.arange(0, QK_ROPE, 2, dtype=F32) / QK_ROPE))
    ang = pos[:, None] * inv_freq[None, :]
    cos, sin = jnp.cos(ang), jnp.sin(ang)
    half = QK_ROPE // 2
    z = lambda w: jnp.zeros((s, w), F32)
    cos_t = jnp.concatenate([cos, cos, z(LANES - QK_ROPE)], axis=1)
    sl_t = jnp.concatenate([-sin, z(LANES - half)], axis=1)
    sr_t = jnp.concatenate([z(half), sin, z(LANES - QK_ROPE)], axis=1)
    return cos_t, sl_t, sr_t


def kernel(x, norm_mix, w_in, norm_qa, w_uq, norm_kva, w_ukv, rpb, w_o_mla, w_o_na, w_out,
           norm_mlp, w_ff1, w_ff2, norm_final):
    b, s, d = x.shape
    assert (b, s, d) == (1, SEQ, D_MODEL)
    depth = w_in.shape[0]
    cos_t, sl_t, sr_t = _rope_tables(s)

    w_c, w_na_qk, w_na_vt, w_g = _split_w_in(w_in)
    uq = w_uq.reshape(depth, Q_LORA, MLA_HEADS, QK_NOPE + QK_ROPE)
    w_qn = uq[..., :QK_NOPE].reshape(depth, Q_LORA, MLA_HEADS * QK_NOPE).astype(BF16)
    w_qp = jnp.pad(uq[..., QK_NOPE:], ((0, 0), (0, 0), (0, 0), (0, LANES - QK_ROPE)))
    w_qp = w_qp.reshape(depth, Q_LORA, MLA_HEADS * LANES).astype(BF16)
    ukv = w_ukv.reshape(depth, KV_LORA, MLA_HEADS, QK_NOPE + V_HEAD)
    w_kn = ukv[..., :QK_NOPE].reshape(depth, KV_LORA, MLA_HEADS * QK_NOPE).astype(BF16)
    w_vt = ukv[..., QK_NOPE:].reshape(depth, KV_LORA, MLA_HEADS * V_HEAD).transpose(0, 2, 1).astype(BF16)
    w_oa = w_o_mla.astype(BF16)
    w_ob = w_o_na.astype(BF16)
    w_o = w_out.astype(BF16)
    w_1 = w_ff1.astype(BF16)
    w_2 = w_ff2.astype(BF16)
    g_qa = norm_qa.reshape(depth, 1, Q_LORA)
    g_kva = norm_kva.reshape(depth, 1, KV_LORA)
    g_mlp = norm_mlp.reshape(depth, 1, d)
    g_mix = norm_mix.reshape(depth, 1, d)
    g_final = norm_final.reshape(1, d)

    xs = x.reshape(s, d)
    for l in range(depth):
        u, c = _norm_proj(xs, g_mix, w_c, l)
        qk_na = _mm(u, w_na_qk, l, BF16, 1024, 1024, name="proj_na_qk")
        vt_na = _proj_vt(u, w_na_vt, l)
        gates = _mm(u, w_g, l, F32, 1024, 1024, name="proj_gate")
        q, k, vt = _mla_up(c, g_qa, g_kva, w_qn, w_qp, w_kn, w_vt, l, cos_t, sl_t, sr_t)
        y_a = _mla_attn(q, k, vt)
        y_b = _na_attn(qk_na, vt_na, _na_pair_tables(rpb[l]))
        xs = _merge(y_a, y_b, gates, xs, w_oa, w_ob, w_o, l)
        xs = _ffn(xs, g_mlp, w_1, w_2, l, g_final, final_norm=(l == depth - 1))
    return xs.reshape(b, s, d)
```

```python
import functools

import jax
import jax.numpy as jnp
import numpy as np
from jax import lax
from jax.experimental import pallas as pl
from jax.experimental.pallas import tpu as pltpu

D_MODEL = 2048
SEQ = 8192
GRID_W = 64
ROWS = SEQ // GRID_W
MLA_HEADS = 8
Q_LORA = 512
KV_LORA = 512
QK_NOPE = 128
QK_ROPE = 64
V_HEAD = 128
ROPE_THETA = 10000.0
NA_HEADS = 8
NA_HEAD_DIM = 128
NA_KH = 8
NA_KW = 16
D_FF = 4 * D_MODEL
EPS = 1e-6
LOG2E = float(np.log2(np.e))

LANES = 128
MLA_QK_PAD = 2 * LANES
BF16_SUBLANES = 16
V_ROWS = V_HEAD + BF16_SUBLANES
NA_Q_ROWS = 8
NA_WIN_ROWS = 2 * NA_KH
NA_TQ = NA_Q_ROWS * GRID_W
NA_TK = NA_WIN_ROWS * GRID_W
NA_BLOCKS = SEQ // NA_TQ
NA_STRIP = NA_KH * GRID_W
NA_PAIR_ROWS = NA_STRIP + GRID_W
NA_PAIR_KINDS = 1 + NA_KH // 2
VMEM_LIMIT = 56 * 1024 * 1024

BF16 = jnp.bfloat16
F32 = jnp.float32
NT_DIMS = (((1,), (1,)), ((), ()))


def _params(n_axes):
    return pltpu.CompilerParams(
        dimension_semantics=("arbitrary",) * n_axes, vmem_limit_bytes=VMEM_LIMIT)


def _rms(x, g):
    return x * lax.rsqrt(jnp.mean(x * x, axis=-1, keepdims=True) + EPS) * g


def _layer_spec(shape, l, index_map):
    return pl.BlockSpec((None,) + tuple(shape), lambda *ids: (l,) + tuple(index_map(*ids)))


def _store_vt_with_ones(vt_ref, vt, heads):
    for h in range(heads):
        vt_ref[h * V_ROWS:h * V_ROWS + V_HEAD, :] = vt[h * V_HEAD:(h + 1) * V_HEAD, :]
        vt_ref[h * V_ROWS + V_HEAD:(h + 1) * V_ROWS, :] = jnp.ones((V_ROWS - V_HEAD, vt.shape[1]), BF16)


C_END = Q_LORA + KV_LORA + QK_ROPE
C_PAD = C_END + LANES - QK_ROPE
NA_W = NA_HEADS * NA_HEAD_DIM
ROW_QK = C_END
ROW_V = ROW_QK + 2 * NA_W
ROW_G = ROW_V + NA_W
PROJ_TM = 1024
PROJ_TN = 1024


def _wt_spec(rows, k, l, row_map):
    return pl.BlockSpec((pl.Element(1), pl.Element(rows), pl.Element(k)),
                        lambda *ids: (l, row_map(*ids), 0))


def _mm_nt_kernel(a_ref, wt_ref, o_ref, wb_sc):
    @pl.when(pl.program_id(1) == 0)
    def _cast():
        wb_sc[...] = wt_ref[0].astype(BF16)

    o_ref[...] = lax.dot_general(a_ref[...], wb_sc[...], NT_DIMS,
                                 preferred_element_type=F32).astype(o_ref.dtype)


def _mm_nt(a, wt, l, row0, n, out_dtype, tm, tn, name):
    m, k = a.shape
    return pl.pallas_call(
        _mm_nt_kernel,
        grid=(n // tn, m // tm),
        in_specs=[pl.BlockSpec((tm, k), lambda j, i: (i, 0)),
                  _wt_spec(tn, k, l, lambda j, i: pl.multiple_of(row0 + j * tn, QK_ROPE))],
        out_specs=pl.BlockSpec((tm, tn), lambda j, i: (i, j)),
        out_shape=jax.ShapeDtypeStruct((m, n), out_dtype),
        scratch_shapes=[pltpu.VMEM((tn, k), BF16)],
        compiler_params=_params(2),
        name=name,
    )(a, wt)


def _norm_proj_kernel(x_ref, g_ref, wt_ref, u_ref, c_ref, wb_sc):
    @pl.when(pl.program_id(0) == 0)
    def _cast():
        wb_sc[...] = wt_ref[0].astype(BF16)

    u = _rms(x_ref[...], g_ref[...]).astype(BF16)
    u_ref[...] = u
    c_ref[...] = lax.dot_general(u, wb_sc[...], NT_DIMS, preferred_element_type=F32)


def _norm_proj(x, g, wt, l, tm=512):
    s, d = x.shape
    return pl.pallas_call(
        _norm_proj_kernel,
        grid=(s // tm,),
        in_specs=[pl.BlockSpec((tm, d), lambda i: (i, 0)),
                  _layer_spec((1, d), l, lambda i: (0, 0)),
                  _wt_spec(C_PAD, d, l, lambda i: 0)],
        out_specs=[pl.BlockSpec((tm, d), lambda i: (i, 0)), pl.BlockSpec((tm, C_PAD), lambda i: (i, 0))],
        out_shape=[jax.ShapeDtypeStruct((s, d), BF16), jax.ShapeDtypeStruct((s, C_PAD), F32)],
        scratch_shapes=[pltpu.VMEM((C_PAD, d), BF16)],
        compiler_params=_params(1),
        name="norm_proj_c",
    )(x, g, wt)


def _proj_vt_kernel(u_ref, wt_ref, vt_ref, wb_sc):
    @pl.when(pl.program_id(0) == 0)
    def _cast():
        wb_sc[...] = wt_ref[0].astype(BF16)

    vt = lax.dot_general(wb_sc[...], u_ref[...], NT_DIMS, preferred_element_type=F32).astype(BF16)
    _store_vt_with_ones(vt_ref, vt, NA_HEADS)


def _proj_vt(u, wt, l, tm=1024):
    s, k = u.shape
    return pl.pallas_call(
        _proj_vt_kernel,
        grid=(s // tm,),
        in_specs=[pl.BlockSpec((tm, k), lambda i: (i, 0)),
                  _wt_spec(NA_W, k, l, lambda i: ROW_V)],
        out_specs=pl.BlockSpec((NA_HEADS * V_ROWS, tm), lambda i: (0, i)),
        out_shape=jax.ShapeDtypeStruct((NA_HEADS * V_ROWS, s), BF16),
        scratch_shapes=[pltpu.VMEM((NA_W, k), BF16)],
        compiler_params=_params(1),
        name="proj_na_vt",
    )(u, wt)


def _rope128(x, c, sl, sr):
    return x * c + pltpu.roll(x, LANES - QK_ROPE // 2, 1) * sl + pltpu.roll(x, QK_ROPE // 2, 1) * sr


def _mla_up_kernel(c_ref, gq_ref, gkv_ref, wqn_ref, wqp_ref, wkn_ref, wvt_ref,
                   cos_ref, sl_ref, sr_ref, q_ref, k_ref, vt_ref):
    scale = (QK_NOPE + QK_ROPE) ** -0.5 * LOG2E
    cq = _rms(c_ref[:, :Q_LORA], gq_ref[...]).astype(BF16)
    ckv = _rms(c_ref[:, Q_LORA:Q_LORA + KV_LORA], gkv_ref[...]).astype(BF16)
    kpe = c_ref[:, Q_LORA + KV_LORA:]
    kpe = jnp.where(lax.broadcasted_iota(jnp.int32, kpe.shape, 1) < QK_ROPE, kpe, 0.0)
    cos, sl, sr = cos_ref[...], sl_ref[...], sr_ref[...]
    kpe_rot = _rope128(kpe, cos, sl, sr).astype(BF16)
    qn = jnp.dot(cq, wqn_ref[...], preferred_element_type=F32) * scale
    qp = jnp.dot(cq, wqp_ref[...], preferred_element_type=F32)
    kn = jnp.dot(ckv, wkn_ref[...], preferred_element_type=F32)
    vt = lax.dot_general(wvt_ref[...], ckv, NT_DIMS, preferred_element_type=F32).astype(BF16)
    _store_vt_with_ones(vt_ref, vt, MLA_HEADS)
    for h in range(MLA_HEADS):
        lo = h * MLA_QK_PAD
        hs = slice(h * LANES, (h + 1) * LANES)
        q_ref[:, lo:lo + LANES] = qn[:, hs].astype(BF16)
        q_ref[:, lo + LANES:lo + 2 * LANES] = (_rope128(qp[:, hs], cos, sl, sr) * scale).astype(BF16)
        k_ref[:, lo:lo + LANES] = kn[:, hs].astype(BF16)
        k_ref[:, lo + LANES:lo + 2 * LANES] = kpe_rot


def _mla_up(c, gq, gkv, wqn, wqp, wkn, wvt, l, cos_t, sl_t, sr_t, tm=512):
    s = c.shape[0]
    row = lambda w: pl.BlockSpec((tm, w), lambda i: (i, 0))
    full = lambda a: _layer_spec(a.shape[1:], l, lambda i: (0, 0))
    hw = MLA_HEADS * V_ROWS
    return pl.pallas_call(
        _mla_up_kernel,
        grid=(s // tm,),
        in_specs=[row(c.shape[1]), full(gq), full(gkv), full(wqn), full(wqp), full(wkn), full(wvt),
                  row(LANES), row(LANES), row(LANES)],
        out_specs=[row(MLA_HEADS * MLA_QK_PAD), row(MLA_HEADS * MLA_QK_PAD),
                   pl.BlockSpec((hw, tm), lambda i: (0, i))],
        out_shape=[jax.ShapeDtypeStruct((s, MLA_HEADS * MLA_QK_PAD), BF16),
                   jax.ShapeDtypeStruct((s, MLA_HEADS * MLA_QK_PAD), BF16),
                   jax.ShapeDtypeStruct((hw, s), BF16)],
        compiler_params=_params(1),
        name="mla_up",
    )(c, gq, gkv, wqn, wqp, wkn, wvt, cos_t, sl_t, sr_t)


def _mla_attn_kernel(q_ref, k_ref, vt_ref, o_ref, m_sc, acc_sc, sa_sc, sb_sc, ma_sc, mb_sc, *, tk):
    q = q_ref[...]
    n_chunks = k_ref.shape[0] // tk
    m_sc[...] = jnp.full(m_sc.shape, -jnp.inf, F32)
    acc_sc[...] = jnp.zeros(acc_sc.shape, F32)

    def scores(c, s_sc, cm_sc):
        k = k_ref[pl.ds(pl.multiple_of(c * tk, tk), tk), :]
        st = lax.dot_general(k, q, NT_DIMS, preferred_element_type=F32)
        s_sc[...] = st
        cm_sc[...] = jnp.max(st, axis=0, keepdims=True)

    def update(c, s_sc, cm_sc):
        vt = vt_ref[:, pl.ds(pl.multiple_of(c * tk, tk), tk)]
        m_prev = m_sc[...]
        m_new = jnp.maximum(m_prev, cm_sc[...])
        alpha = jnp.exp2(m_prev - m_new)
        pt = jnp.exp2(s_sc[...] - m_new)
        acc_sc[...] = alpha * acc_sc[...] + jnp.dot(vt, pt.astype(BF16), preferred_element_type=F32)
        m_sc[...] = m_new

    assert n_chunks % 2 == 0
    scores(0, sa_sc, ma_sc)

    def body(j, carry):
        c = 2 * j
        scores(c + 1, sb_sc, mb_sc)
        update(c, sa_sc, ma_sc)
        scores(c + 2, sa_sc, ma_sc)
        update(c + 1, sb_sc, mb_sc)
        return carry

    lax.fori_loop(0, n_chunks // 2 - 1, body, 0)
    scores(n_chunks - 1, sb_sc, mb_sc)
    update(n_chunks - 2, sa_sc, ma_sc)
    update(n_chunks - 1, sb_sc, mb_sc)
    o_ref[...] = (acc_sc[:V_HEAD, :] / acc_sc[V_HEAD:V_HEAD + 1, :]).T.astype(o_ref.dtype)


def _mla_attn(q, k, vt, tq=2048, tk=1024):
    s = q.shape[0]
    return pl.pallas_call(
        functools.partial(_mla_attn_kernel, tk=tk),
        grid=(MLA_HEADS, s // tq),
        in_specs=[pl.BlockSpec((tq, MLA_QK_PAD), lambda h, i: (i, h)),
                  pl.BlockSpec((s, MLA_QK_PAD), lambda h, i: (0, h)),
                  pl.BlockSpec((V_ROWS, s), lambda h, i: (h, 0))],
        out_specs=pl.BlockSpec((tq, V_HEAD), lambda h, i: (i, h)),
        out_shape=jax.ShapeDtypeStruct((s, MLA_HEADS * V_HEAD), BF16),
        scratch_shapes=[pltpu.VMEM((1, tq), F32), pltpu.VMEM((V_ROWS, tq), F32),
                        pltpu.VMEM((tk, tq), F32), pltpu.VMEM((tk, tq), F32),
                        pltpu.VMEM((1, tq), F32), pltpu.VMEM((1, tq), F32)],
        compiler_params=_params(2),
        name="mla_attn",
    )(q, k, vt)


def _na_row_start(r):
    return min(max(r - NA_KH // 2, 0), ROWS - NA_KH)


def _na_win_start(blk):
    return min(max(blk * NA_Q_ROWS - NA_KH // 2, 0), ROWS - NA_WIN_ROWS)


def _na_pair_placement(blk, pair):
    r_e = blk * NA_Q_ROWS + 2 * pair
    r_o = r_e + 1
    off_e = _na_row_start(r_e) - _na_win_start(blk)
    off_o = _na_row_start(r_o) - _na_win_start(blk)
    dy_e = _na_row_start(r_e) - r_e + (NA_KH - 1)
    dy_o = _na_row_start(r_o) - r_o + (NA_KH - 1)
    if off_o == off_e + 1:
        assert dy_e == dy_o == NA_KH // 2 - 1
        return 0, off_e * GRID_W, NA_PAIR_ROWS
    assert off_o == off_e and dy_e == dy_o + 1 and dy_e % 2 == 1
    return 1 + (NA_KH - 1 - dy_e) // 2, off_e * GRID_W, NA_STRIP


def _na_pair_tables(rpb_l):
    c = np.arange(GRID_W)
    col_start = np.clip(c - NA_KW // 2, 0, GRID_W - NA_KW)
    col_ok = (c[None, :] >= col_start[:, None]) & (c[None, :] < col_start[:, None] + NA_KW)
    dx = np.clip(c[None, :] - c[:, None], -(NA_KW - 1), NA_KW - 1) + (NA_KW - 1)
    rpb_l = rpb_l.astype(F32)
    toe = jnp.full((NA_HEADS, 2 * NA_KH - 1, GRID_W, GRID_W), -jnp.inf, F32)
    for d in range(2 * NA_KW - 1):
        toe = jnp.where(((dx == d) & col_ok).T, rpb_l[:, :, d][:, :, None, None], toe)

    def strip(dy0, lo, hi):
        t = toe[:, dy0:dy0 + NA_KH].reshape(NA_HEADS, NA_STRIP, GRID_W)
        return jnp.pad(t, ((0, 0), (lo, hi), (0, 0)), constant_values=-jnp.inf)

    mid = NA_KH // 2 - 1
    kinds = [jnp.concatenate([strip(mid, 0, GRID_W), strip(mid, GRID_W, 0)], axis=2)]
    for j in range(NA_KH // 2):
        dy_e = NA_KH - 1 - 2 * j
        kinds.append(jnp.concatenate([strip(dy_e, 0, GRID_W), strip(dy_e - 1, 0, GRID_W)], axis=2))
    return jnp.stack(kinds, axis=1)


def _na_attn_kernel(q_ref, k_ref, vt_ref, t_ref, o_ref, bias_sc, sa_sc, sb_sc, ma_sc, mb_sc):
    last = NA_BLOCKS - 1
    bias_sc[...] = jnp.full(bias_sc.shape, -jnp.inf, F32)
    for var, blk in enumerate((0, 1, last)):
        for pair in range(NA_Q_ROWS // 2):
            kind, top, rows = _na_pair_placement(blk, pair)
            bias_sc[var, top:top + rows, pair * LANES:(pair + 1) * LANES] = t_ref[0, kind, :rows, :] * LOG2E

    def win_start(blk):
        start = jnp.clip(blk * NA_TQ - (NA_KH // 2) * GRID_W, 0, SEQ - NA_TK)
        return pl.multiple_of(start, (NA_KH // 2) * GRID_W)

    def scores(blk, var, s_sc, cm_sc):
        q = q_ref[pl.ds(pl.multiple_of(blk * NA_TQ, NA_TQ), NA_TQ), :]
        k = k_ref[pl.ds(win_start(blk), NA_TK), :]
        st = lax.dot_general(k, q, NT_DIMS, preferred_element_type=F32)
        st = st * (NA_HEAD_DIM ** -0.5 * LOG2E) + bias_sc[var]
        s_sc[...] = st
        cm_sc[...] = jnp.max(st, axis=0, keepdims=True)

    def finish(blk, s_sc, cm_sc):
        vt = vt_ref[:, pl.ds(win_start(blk), NA_TK)]
        pt = jnp.exp2(s_sc[...] - cm_sc[...])
        acc = jnp.dot(vt, pt.astype(BF16), preferred_element_type=F32)
        o = (acc[:V_HEAD, :] / acc[V_HEAD:V_HEAD + 1, :]).T
        o_ref[pl.ds(pl.multiple_of(blk * NA_TQ, NA_TQ), NA_TQ), :] = o.astype(o_ref.dtype)

    assert NA_BLOCKS % 2 == 0
    scores(0, 0, sa_sc, ma_sc)

    def body(j, carry):
        b = 2 * j
        scores(b + 1, 1, sb_sc, mb_sc)
        finish(b, sa_sc, ma_sc)
        scores(b + 2, 1, sa_sc, ma_sc)
        finish(b + 1, sb_sc, mb_sc)
        return carry

    lax.fori_loop(0, NA_BLOCKS // 2 - 1, body, 0)
    scores(last, 2, sb_sc, mb_sc)
    finish(last - 1, sa_sc, ma_sc)
    finish(last, sb_sc, mb_sc)


def _na_attn(qk, vt, tables):
    s = qk.shape[0]
    return pl.pallas_call(
        _na_attn_kernel,
        grid=(NA_HEADS,),
        in_specs=[pl.BlockSpec((s, NA_HEAD_DIM), lambda h: (0, h)),
                  pl.BlockSpec((s, NA_HEAD_DIM), lambda h: (0, NA_HEADS + h)),
                  pl.BlockSpec((V_ROWS, s), lambda h: (h, 0)),
                  pl.BlockSpec((1, NA_PAIR_KINDS, NA_PAIR_ROWS, LANES), lambda h: (h, 0, 0, 0))],
        out_specs=pl.BlockSpec((s, NA_HEAD_DIM), lambda h: (0, h)),
        out_shape=jax.ShapeDtypeStruct((s, NA_HEADS * NA_HEAD_DIM), BF16),
        scratch_shapes=[pltpu.VMEM((3, NA_TK, NA_TQ), F32),
                        pltpu.VMEM((NA_TK, NA_TQ), F32), pltpu.VMEM((NA_TK, NA_TQ), F32),
                        pltpu.VMEM((1, NA_TQ), F32), pltpu.VMEM((1, NA_TQ), F32)],
        compiler_params=_params(1),
        name="na_attn",
    )(qk, qk, vt, tables)


def _merge_kernel(ya_ref, yb_ref, g_ref, x_ref, woa_ref, wob_ref, wo_ref, o_ref):
    a = jnp.dot(ya_ref[...], woa_ref[...], preferred_element_type=F32)
    b = jnp.dot(yb_ref[...], wob_ref[...], preferred_element_type=F32)
    ga = jax.nn.sigmoid(g_ref[:, :D_MODEL])
    gb = jax.nn.sigmoid(g_ref[:, D_MODEL:])
    merged = (ga * a + gb * b).astype(BF16)
    o_ref[...] = x_ref[...] + jnp.dot(merged, wo_ref[...], preferred_element_type=F32)


def _merge(ya, yb, gates, x, woa, wob, wo, l, tm=256):
    s = ya.shape[0]
    row = lambda w: pl.BlockSpec((tm, w), lambda i: (i, 0))
    full = lambda a: _layer_spec(a.shape[1:], l, lambda i: (0, 0))
    return pl.pallas_call(
        _merge_kernel,
        grid=(s // tm,),
        in_specs=[row(ya.shape[1]), row(yb.shape[1]), row(gates.shape[1]), row(D_MODEL),
                  full(woa), full(wob), full(wo)],
        out_specs=row(D_MODEL),
        out_shape=jax.ShapeDtypeStruct((s, D_MODEL), F32),
        compiler_params=_params(1),
        name="merge_out",
    )(ya, yb, gates, x, woa, wob, wo)


def _ffn_kernel(x_ref, g_ref, w1_ref, w2_ref, gf_ref, o_ref, u_sc, *, final_norm):
    f = pl.program_id(1)

    @pl.when(f == 0)
    def _init():
        x = x_ref[...]
        u_sc[...] = _rms(x, g_ref[...]).astype(BF16)
        o_ref[...] = x

    h = jnp.dot(u_sc[...], w1_ref[...], preferred_element_type=F32)
    a = jnp.square(jnp.maximum(h, 0.0)).astype(BF16)
    o_ref[...] += jnp.dot(a, w2_ref[...], preferred_element_type=F32)

    if final_norm:
        @pl.when(f == pl.num_programs(1) - 1)
        def _final():
            o_ref[...] = _rms(o_ref[...], gf_ref[...])


def _ffn(x, g, w1, w2, l, g_final, final_norm, tm=512, tf=1024):
    s, d = x.shape
    return pl.pallas_call(
        functools.partial(_ffn_kernel, final_norm=final_norm),
        grid=(s // tm, w1.shape[2] // tf),
        in_specs=[pl.BlockSpec((tm, d), lambda i, f: (i, 0)),
                  _layer_spec((1, d), l, lambda i, f: (0, 0)),
                  _layer_spec((d, tf), l, lambda i, f: (0, f)),
                  _layer_spec((tf, d), l, lambda i, f: (f, 0)),
                  pl.BlockSpec((1, d), lambda i, f: (0, 0))],
        out_specs=pl.BlockSpec((tm, d), lambda i, f: (i, 0)),
        out_shape=jax.ShapeDtypeStruct((s, d), F32),
        scratch_shapes=[pltpu.VMEM((tm, d), BF16)],
        compiler_params=_params(2),
        name="ffn",
    )(x, g, w1, w2, g_final)


def _rope_tables(s):
    pos = jnp.arange(s, dtype=F32)
    inv_freq = 1.0 / (ROPE_THETA ** (jnp.arange(0, QK_ROPE, 2, dtype=F32) / QK_ROPE))
    ang = pos[:, None] * inv_freq[None, :]
    cos, sin = jnp.cos(ang), jnp.sin(ang)
    half = QK_ROPE // 2
    z = lambda w: jnp.zeros((s, w), F32)
    cos_t = jnp.concatenate([cos, cos, z(LANES - QK_ROPE)], axis=1)
    sl_t = jnp.concatenate([-sin, z(LANES - half)], axis=1)
    sr_t = jnp.concatenate([z(half), sin, z(LANES - QK_ROPE)], axis=1)
    return cos_t, sl_t, sr_t


def kernel(x, norm_mix, w_in, norm_qa, w_uq, norm_kva, w_ukv, rpb, w_o_mla, w_o_na, w_out,
           norm_mlp, w_ff1, w_ff2, norm_final):
    b, s, d = x.shape
    assert (b, s, d) == (1, SEQ, D_MODEL)
    depth = w_in.shape[0]
    cos_t, sl_t, sr_t = _rope_tables(s)

    w_in_t = w_in.transpose(0, 2, 1)
    uq = w_uq.reshape(depth, Q_LORA, MLA_HEADS, QK_NOPE + QK_ROPE)
    w_qn = uq[..., :QK_NOPE].reshape(depth, Q_LORA, MLA_HEADS * QK_NOPE).astype(BF16)
    w_qp = jnp.pad(uq[..., QK_NOPE:], ((0, 0), (0, 0), (0, 0), (0, LANES - QK_ROPE)))
    w_qp = w_qp.reshape(depth, Q_LORA, MLA_HEADS * LANES).astype(BF16)
    ukv = w_ukv.reshape(depth, KV_LORA, MLA_HEADS, QK_NOPE + V_HEAD)
    w_kn = ukv[..., :QK_NOPE].reshape(depth, KV_LORA, MLA_HEADS * QK_NOPE).astype(BF16)
    w_vt = ukv[..., QK_NOPE:].reshape(depth, KV_LORA, MLA_HEADS * V_HEAD).transpose(0, 2, 1).astype(BF16)
    w_oa = w_o_mla.astype(BF16)
    w_ob = w_o_na.astype(BF16)
    w_o = w_out.astype(BF16)
    w_1 = w_ff1.astype(BF16)
    w_2 = w_ff2.astype(BF16)
    g_qa = norm_qa.reshape(depth, 1, Q_LORA)
    g_kva = norm_kva.reshape(depth, 1, KV_LORA)
    g_mlp = norm_mlp.reshape(depth, 1, d)
    g_mix = norm_mix.reshape(depth, 1, d)
    g_final = norm_final.reshape(1, d)

    xs = x.reshape(s, d)
    for l in range(depth):
        u, c = _norm_proj(xs, g_mix, w_in_t, l)
        qk_na = _mm_nt(u, w_in_t, l, ROW_QK, 2 * NA_W, BF16, PROJ_TM, PROJ_TN, name="proj_na_qk")
        vt_na = _proj_vt(u, w_in_t, l)
        gates = _mm_nt(u, w_in_t, l, ROW_G, 2 * D_MODEL, F32, PROJ_TM, PROJ_TN, name="proj_gate")
        q, k, vt = _mla_up(c, g_qa, g_kva, w_qn, w_qp, w_kn, w_vt, l, cos_t, sl_t, sr_t)
        y_a = _mla_attn(q, k, vt)
        y_b = _na_attn(qk_na, vt_na, _na_pair_tables(rpb[l]))
        xs = _merge(y_a, y_b, gates, xs, w_oa, w_ob, w_o, l)
        xs = _ffn(xs, g_mlp, w_1, w_2, l, g_final, final_norm=(l == depth - 1))
    return xs.reshape(b, s, d)
```

```python
import functools

import jax
import jax.numpy as jnp
import numpy as np
from jax import lax
from jax.experimental import pallas as pl
from jax.experimental.pallas import tpu as pltpu

D_MODEL = 2048
SEQ = 8192
GRID_W = 64
ROWS = SEQ // GRID_W
MLA_HEADS = 8
Q_LORA = 512
KV_LORA = 512
QK_NOPE = 128
QK_ROPE = 64
V_HEAD = 128
ROPE_THETA = 10000.0
NA_HEADS = 8
NA_HEAD_DIM = 128
NA_KH = 8
NA_KW = 16
D_FF = 4 * D_MODEL
EPS = 1e-6
LOG2E = float(np.log2(np.e))

LANES = 128
MLA_QK_PAD = 2 * LANES
BF16_SUBLANES = 16
V_ROWS = V_HEAD + BF16_SUBLANES
MLA_REF_KEYS = 128
MLA_MAX_GAP = 60.0
NA_Q_ROWS = 8
NA_WIN_ROWS = 2 * NA_KH
NA_TQ = NA_Q_ROWS * GRID_W
NA_TK = NA_WIN_ROWS * GRID_W
NA_BLOCKS = SEQ // NA_TQ
NA_STRIP = NA_KH * GRID_W
NA_PAIR_ROWS = NA_STRIP + GRID_W
NA_PAIR_KINDS = 1 + NA_KH // 2
VMEM_LIMIT = 56 * 1024 * 1024

BF16 = jnp.bfloat16
F32 = jnp.float32
NT_DIMS = (((1,), (1,)), ((), ()))


def _params(n_axes):
    return pltpu.CompilerParams(
        dimension_semantics=("arbitrary",) * n_axes, vmem_limit_bytes=VMEM_LIMIT)


def _rms(x, g):
    return x * lax.rsqrt(jnp.mean(x * x, axis=-1, keepdims=True) + EPS) * g


def _layer_spec(shape, l, index_map):
    return pl.BlockSpec((None,) + tuple(shape), lambda *ids: (l,) + tuple(index_map(*ids)))


def _store_vt_with_ones(vt_ref, vt, heads):
    for h in range(heads):
        vt_ref[h * V_ROWS:h * V_ROWS + V_HEAD, :] = vt[h * V_HEAD:(h + 1) * V_HEAD, :]
        vt_ref[h * V_ROWS + V_HEAD:(h + 1) * V_ROWS, :] = jnp.ones((V_ROWS - V_HEAD, vt.shape[1]), BF16)


C_END = Q_LORA + KV_LORA + QK_ROPE
C_PAD = C_END + LANES - QK_ROPE
NA_W = NA_HEADS * NA_HEAD_DIM
ROW_QK = C_END
ROW_V = ROW_QK + 2 * NA_W
ROW_G = ROW_V + NA_W
PROJ_TM = 1024
PROJ_TN = 1024


def _wt_spec(rows, k, l, row_map):
    return pl.BlockSpec((pl.Element(1), pl.Element(rows), pl.Element(k)),
                        lambda *ids: (l, row_map(*ids), 0))


def _mm_nt_kernel(a_ref, wt_ref, o_ref, wb_sc):
    @pl.when(pl.program_id(1) == 0)
    def _cast():
        wb_sc[...] = wt_ref[0].astype(BF16)

    o_ref[...] = lax.dot_general(a_ref[...], wb_sc[...], NT_DIMS,
                                 preferred_element_type=F32).astype(o_ref.dtype)


def _mm_nt(a, wt, l, row0, n, out_dtype, tm, tn, name):
    m, k = a.shape
    return pl.pallas_call(
        _mm_nt_kernel,
        grid=(n // tn, m // tm),
        in_specs=[pl.BlockSpec((tm, k), lambda j, i: (i, 0)),
                  _wt_spec(tn, k, l, lambda j, i: pl.multiple_of(row0 + j * tn, QK_ROPE))],
        out_specs=pl.BlockSpec((tm, tn), lambda j, i: (i, j)),
        out_shape=jax.ShapeDtypeStruct((m, n), out_dtype),
        scratch_shapes=[pltpu.VMEM((tn, k), BF16)],
        compiler_params=_params(2),
        name=name,
    )(a, wt)


def _norm_proj_kernel(x_ref, g_ref, wt_ref, u_ref, c_ref, wb_sc):
    @pl.when(pl.program_id(0) == 0)
    def _cast():
        wb_sc[...] = wt_ref[0].astype(BF16)

    u = _rms(x_ref[...], g_ref[...]).astype(BF16)
    u_ref[...] = u
    c_ref[...] = lax.dot_general(u, wb_sc[...], NT_DIMS, preferred_element_type=F32)


def _norm_proj(x, g, wt, l, tm=512):
    s, d = x.shape
    return pl.pallas_call(
        _norm_proj_kernel,
        grid=(s // tm,),
        in_specs=[pl.BlockSpec((tm, d), lambda i: (i, 0)),
                  _layer_spec((1, d), l, lambda i: (0, 0)),
                  _wt_spec(C_PAD, d, l, lambda i: 0)],
        out_specs=[pl.BlockSpec((tm, d), lambda i: (i, 0)), pl.BlockSpec((tm, C_PAD), lambda i: (i, 0))],
        out_shape=[jax.ShapeDtypeStruct((s, d), BF16), jax.ShapeDtypeStruct((s, C_PAD), F32)],
        scratch_shapes=[pltpu.VMEM((C_PAD, d), BF16)],
        compiler_params=_params(1),
        name="norm_proj_c",
    )(x, g, wt)


def _proj_vt_kernel(u_ref, wt_ref, vt_ref, wb_sc):
    @pl.when(pl.program_id(0) == 0)
    def _cast():
        wb_sc[...] = wt_ref[0].astype(BF16)

    vt = lax.dot_general(wb_sc[...], u_ref[...], NT_DIMS, preferred_element_type=F32).astype(BF16)
    _store_vt_with_ones(vt_ref, vt, NA_HEADS)


def _proj_vt(u, wt, l, tm=1024):
    s, k = u.shape
    return pl.pallas_call(
        _proj_vt_kernel,
        grid=(s // tm,),
        in_specs=[pl.BlockSpec((tm, k), lambda i: (i, 0)),
                  _wt_spec(NA_W, k, l, lambda i: ROW_V)],
        out_specs=pl.BlockSpec((NA_HEADS * V_ROWS, tm), lambda i: (0, i)),
        out_shape=jax.ShapeDtypeStruct((NA_HEADS * V_ROWS, s), BF16),
        scratch_shapes=[pltpu.VMEM((NA_W, k), BF16)],
        compiler_params=_params(1),
        name="proj_na_vt",
    )(u, wt)


def _rope128(x, c, sl, sr):
    return x * c + pltpu.roll(x, LANES - QK_ROPE // 2, 1) * sl + pltpu.roll(x, QK_ROPE // 2, 1) * sr


def _mla_up_kernel(c_ref, gq_ref, gkv_ref, wqn_ref, wqp_ref, wkn_ref, wvt_ref,
                   cos_ref, sl_ref, sr_ref, q_ref, k_ref, vt_ref):
    scale = (QK_NOPE + QK_ROPE) ** -0.5 * LOG2E
    cq = _rms(c_ref[:, :Q_LORA], gq_ref[...]).astype(BF16)
    ckv = _rms(c_ref[:, Q_LORA:Q_LORA + KV_LORA], gkv_ref[...]).astype(BF16)
    kpe = c_ref[:, Q_LORA + KV_LORA:]
    kpe = jnp.where(lax.broadcasted_iota(jnp.int32, kpe.shape, 1) < QK_ROPE, kpe, 0.0)
    cos, sl, sr = cos_ref[...], sl_ref[...], sr_ref[...]
    kpe_rot = _rope128(kpe, cos, sl, sr).astype(BF16)
    qn = jnp.dot(cq, wqn_ref[...], preferred_element_type=F32) * scale
    qp = jnp.dot(cq, wqp_ref[...], preferred_element_type=F32)
    kn = jnp.dot(ckv, wkn_ref[...], preferred_element_type=F32)
    vt = lax.dot_general(wvt_ref[...], ckv, NT_DIMS, preferred_element_type=F32).astype(BF16)
    _store_vt_with_ones(vt_ref, vt, MLA_HEADS)
    for h in range(MLA_HEADS):
        lo = h * MLA_QK_PAD
        hs = slice(h * LANES, (h + 1) * LANES)
        q_ref[:, lo:lo + LANES] = qn[:, hs].astype(BF16)
        q_ref[:, lo + LANES:lo + 2 * LANES] = (_rope128(qp[:, hs], cos, sl, sr) * scale).astype(BF16)
        k_ref[:, lo:lo + LANES] = kn[:, hs].astype(BF16)
        k_ref[:, lo + LANES:lo + 2 * LANES] = kpe_rot


def _mla_up(c, gq, gkv, wqn, wqp, wkn, wvt, l, cos_t, sl_t, sr_t, tm=512):
    s = c.shape[0]
    row = lambda w: pl.BlockSpec((tm, w), lambda i: (i, 0))
    full = lambda a: _layer_spec(a.shape[1:], l, lambda i: (0, 0))
    hw = MLA_HEADS * V_ROWS
    return pl.pallas_call(
        _mla_up_kernel,
        grid=(s // tm,),
        in_specs=[row(c.shape[1]), full(gq), full(gkv), full(wqn), full(wqp), full(wkn), full(wvt),
                  row(LANES), row(LANES), row(LANES)],
        out_specs=[row(MLA_HEADS * MLA_QK_PAD), row(MLA_HEADS * MLA_QK_PAD),
                   pl.BlockSpec((hw, tm), lambda i: (0, i))],
        out_shape=[jax.ShapeDtypeStruct((s, MLA_HEADS * MLA_QK_PAD), BF16),
                   jax.ShapeDtypeStruct((s, MLA_HEADS * MLA_QK_PAD), BF16),
                   jax.ShapeDtypeStruct((hw, s), BF16)],
        compiler_params=_params(1),
        name="mla_up",
    )(c, gq, gkv, wqn, wqp, wkn, wvt, cos_t, sl_t, sr_t)


def _mla_attn_kernel(q_ref, k_ref, vt_ref, o_ref, r_sc, mx_sc, m_sc, acc_sc, sa_sc, sb_sc, ma_sc, mb_sc,
                     *, tk):
    q = q_ref[...]
    n_chunks = k_ref.shape[0] // tk
    assert n_chunks % 2 == 0

    def scores(c):
        k = k_ref[pl.ds(pl.multiple_of(c * tk, tk), tk), :]
        return lax.dot_general(k, q, NT_DIMS, preferred_element_type=F32)

    def values(c, pt):
        vt = vt_ref[:, pl.ds(pl.multiple_of(c * tk, tk), tk)]
        return jnp.dot(vt, pt.astype(BF16), preferred_element_type=F32)

    def write_out():
        o_ref[...] = (acc_sc[:V_HEAD, :] / acc_sc[V_HEAD:V_HEAD + 1, :]).T.astype(o_ref.dtype)

    s_ref = lax.dot_general(k_ref[:MLA_REF_KEYS, :], q, NT_DIMS, preferred_element_type=F32)
    r = jnp.max(s_ref, axis=0, keepdims=True)
    r_sc[...] = r
    mx_sc[...] = r
    acc_sc[...] = jnp.zeros(acc_sc.shape, F32)

    def step(c):
        st = scores(c)
        mx_sc[...] = jnp.maximum(mx_sc[...], jnp.max(st, axis=0, keepdims=True))
        acc_sc[...] += values(c, jnp.exp2(st - r_sc[...]))

    def single_pass_body(j, carry):
        step(2 * j)
        step(2 * j + 1)
        return carry

    lax.fori_loop(0, n_chunks // 2, single_pass_body, 0)
    in_range = jnp.max(mx_sc[...] - r_sc[...]) <= MLA_MAX_GAP

    @pl.when(in_range)
    def _accept():
        write_out()

    @pl.when(jnp.logical_not(in_range))
    def _online_softmax():
        m_sc[...] = jnp.full(m_sc.shape, -jnp.inf, F32)
        acc_sc[...] = jnp.zeros(acc_sc.shape, F32)

        def stage(c, s_sc, cm_sc):
            st = scores(c)
            s_sc[...] = st
            cm_sc[...] = jnp.max(st, axis=0, keepdims=True)

        def update(c, s_sc, cm_sc):
            m_prev = m_sc[...]
            m_new = jnp.maximum(m_prev, cm_sc[...])
            alpha = jnp.exp2(m_prev - m_new)
            acc_sc[...] = alpha * acc_sc[...] + values(c, jnp.exp2(s_sc[...] - m_new))
            m_sc[...] = m_new

        stage(0, sa_sc, ma_sc)

        def body(j, carry):
            c = 2 * j
            stage(c + 1, sb_sc, mb_sc)
            update(c, sa_sc, ma_sc)
            stage(c + 2, sa_sc, ma_sc)
            update(c + 1, sb_sc, mb_sc)
            return carry

        lax.fori_loop(0, n_chunks // 2 - 1, body, 0)
        stage(n_chunks - 1, sb_sc, mb_sc)
        update(n_chunks - 2, sa_sc, ma_sc)
        update(n_chunks - 1, sb_sc, mb_sc)
        write_out()


def _mla_attn(q, k, vt, tq=2048, tk=1024):
    s = q.shape[0]
    return pl.pallas_call(
        functools.partial(_mla_attn_kernel, tk=tk),
        grid=(MLA_HEADS, s // tq),
        in_specs=[pl.BlockSpec((tq, MLA_QK_PAD), lambda h, i: (i, h)),
                  pl.BlockSpec((s, MLA_QK_PAD), lambda h, i: (0, h)),
                  pl.BlockSpec((V_ROWS, s), lambda h, i: (h, 0))],
        out_specs=pl.BlockSpec((tq, V_HEAD), lambda h, i: (i, h)),
        out_shape=jax.ShapeDtypeStruct((s, MLA_HEADS * V_HEAD), BF16),
        scratch_shapes=[pltpu.VMEM((1, tq), F32), pltpu.VMEM((1, tq), F32), pltpu.VMEM((1, tq), F32),
                        pltpu.VMEM((V_ROWS, tq), F32),
                        pltpu.VMEM((tk, tq), F32), pltpu.VMEM((tk, tq), F32),
                        pltpu.VMEM((1, tq), F32), pltpu.VMEM((1, tq), F32)],
        compiler_params=_params(2),
        name="mla_attn",
    )(q, k, vt)


def _na_row_start(r):
    return min(max(r - NA_KH // 2, 0), ROWS - NA_KH)


def _na_win_start(blk):
    return min(max(blk * NA_Q_ROWS - NA_KH // 2, 0), ROWS - NA_WIN_ROWS)


def _na_pair_placement(blk, pair):
    r_e = blk * NA_Q_ROWS + 2 * pair
    r_o = r_e + 1
    off_e = _na_row_start(r_e) - _na_win_start(blk)
    off_o = _na_row_start(r_o) - _na_win_start(blk)
    dy_e = _na_row_start(r_e) - r_e + (NA_KH - 1)
    dy_o = _na_row_start(r_o) - r_o + (NA_KH - 1)
    if off_o == off_e + 1:
        assert dy_e == dy_o == NA_KH // 2 - 1
        return 0, off_e * GRID_W, NA_PAIR_ROWS
    assert off_o == off_e and dy_e == dy_o + 1 and dy_e % 2 == 1
    return 1 + (NA_KH - 1 - dy_e) // 2, off_e * GRID_W, NA_STRIP


def _na_pair_tables(rpb_l):
    c = np.arange(GRID_W)
    col_start = np.clip(c - NA_KW // 2, 0, GRID_W - NA_KW)
    col_ok = (c[None, :] >= col_start[:, None]) & (c[None, :] < col_start[:, None] + NA_KW)
    dx = np.clip(c[None, :] - c[:, None], -(NA_KW - 1), NA_KW - 1) + (NA_KW - 1)
    rpb_l = rpb_l.astype(F32)
    toe = jnp.full((NA_HEADS, 2 * NA_KH - 1, GRID_W, GRID_W), -jnp.inf, F32)
    for d in range(2 * NA_KW - 1):
        toe = jnp.where(((dx == d) & col_ok).T, rpb_l[:, :, d][:, :, None, None], toe)

    def strip(dy0, lo, hi):
        t = toe[:, dy0:dy0 + NA_KH].reshape(NA_HEADS, NA_STRIP, GRID_W)
        return jnp.pad(t, ((0, 0), (lo, hi), (0, 0)), constant_values=-jnp.inf)

    mid = NA_KH // 2 - 1
    kinds = [jnp.concatenate([strip(mid, 0, GRID_W), strip(mid, GRID_W, 0)], axis=2)]
    for j in range(NA_KH // 2):
        dy_e = NA_KH - 1 - 2 * j
        kinds.append(jnp.concatenate([strip(dy_e, 0, GRID_W), strip(dy_e - 1, 0, GRID_W)], axis=2))
    return jnp.stack(kinds, axis=1)


def _na_attn_kernel(q_ref, k_ref, vt_ref, t_ref, o_ref, bias_sc, sa_sc, sb_sc, ma_sc, mb_sc):
    last = NA_BLOCKS - 1
    bias_sc[...] = jnp.full(bias_sc.shape, -jnp.inf, F32)
    for var, blk in enumerate((0, 1, last)):
        for pair in range(NA_Q_ROWS // 2):
            kind, top, rows = _na_pair_placement(blk, pair)
            bias_sc[var, top:top + rows, pair * LANES:(pair + 1) * LANES] = t_ref[0, kind, :rows, :] * LOG2E

    def win_start(blk):
        start = jnp.clip(blk * NA_TQ - (NA_KH // 2) * GRID_W, 0, SEQ - NA_TK)
        return pl.multiple_of(start, (NA_KH // 2) * GRID_W)

    def scores(blk, var, s_sc, cm_sc):
        q = q_ref[pl.ds(pl.multiple_of(blk * NA_TQ, NA_TQ), NA_TQ), :]
        k = k_ref[pl.ds(win_start(blk), NA_TK), :]
        st = lax.dot_general(k, q, NT_DIMS, preferred_element_type=F32)
        st = st * (NA_HEAD_DIM ** -0.5 * LOG2E) + bias_sc[var]
        s_sc[...] = st
        cm_sc[...] = jnp.max(st, axis=0, keepdims=True)

    def finish(blk, s_sc, cm_sc):
        vt = vt_ref[:, pl.ds(win_start(blk), NA_TK)]
        pt = jnp.exp2(s_sc[...] - cm_sc[...])
        acc = jnp.dot(vt, pt.astype(BF16), preferred_element_type=F32)
        o = (acc[:V_HEAD, :] / acc[V_HEAD:V_HEAD + 1, :]).T
        o_ref[pl.ds(pl.multiple_of(blk * NA_TQ, NA_TQ), NA_TQ), :] = o.astype(o_ref.dtype)

    assert NA_BLOCKS % 2 == 0
    scores(0, 0, sa_sc, ma_sc)

    def body(j, carry):
        b = 2 * j
        scores(b + 1, 1, sb_sc, mb_sc)
        finish(b, sa_sc, ma_sc)
        scores(b + 2, 1, sa_sc, ma_sc)
        finish(b + 1, sb_sc, mb_sc)
        return carry

    lax.fori_loop(0, NA_BLOCKS // 2 - 1, body, 0)
    scores(last, 2, sb_sc, mb_sc)
    finish(last - 1, sa_sc, ma_sc)
    finish(last, sb_sc, mb_sc)


def _na_attn(qk, vt, tables):
    s = qk.shape[0]
    return pl.pallas_call(
        _na_attn_kernel,
        grid=(NA_HEADS,),
        in_specs=[pl.BlockSpec((s, NA_HEAD_DIM), lambda h: (0, h)),
                  pl.BlockSpec((s, NA_HEAD_DIM), lambda h: (0, NA_HEADS + h)),
                  pl.BlockSpec((V_ROWS, s), lambda h: (h, 0)),
                  pl.BlockSpec((1, NA_PAIR_KINDS, NA_PAIR_ROWS, LANES), lambda h: (h, 0, 0, 0))],
        out_specs=pl.BlockSpec((s, NA_HEAD_DIM), lambda h: (0, h)),
        out_shape=jax.ShapeDtypeStruct((s, NA_HEADS * NA_HEAD_DIM), BF16),
        scratch_shapes=[pltpu.VMEM((3, NA_TK, NA_TQ), F32),
                        pltpu.VMEM((NA_TK, NA_TQ), F32), pltpu.VMEM((NA_TK, NA_TQ), F32),
                        pltpu.VMEM((1, NA_TQ), F32), pltpu.VMEM((1, NA_TQ), F32)],
        compiler_params=_params(1),
        name="na_attn",
    )(qk, qk, vt, tables)


def _merge_kernel(ya_ref, yb_ref, g_ref, x_ref, woa_ref, wob_ref, wo_ref, o_ref):
    a = jnp.dot(ya_ref[...], woa_ref[...], preferred_element_type=F32)
    b = jnp.dot(yb_ref[...], wob_ref[...], preferred_element_type=F32)
    ga = jax.nn.sigmoid(g_ref[:, :D_MODEL])
    gb = jax.nn.sigmoid(g_ref[:, D_MODEL:])
    merged = (ga * a + gb * b).astype(BF16)
    o_ref[...] = x_ref[...] + jnp.dot(merged, wo_ref[...], preferred_element_type=F32)


def _merge(ya, yb, gates, x, woa, wob, wo, l, tm=256):
    s = ya.shape[0]
    row = lambda w: pl.BlockSpec((tm, w), lambda i: (i, 0))
    full = lambda a: _layer_spec(a.shape[1:], l, lambda i: (0, 0))
    return pl.pallas_call(
        _merge_kernel,
        grid=(s // tm,),
        in_specs=[row(ya.shape[1]), row(yb.shape[1]), row(gates.shape[1]), row(D_MODEL),
                  full(woa), full(wob), full(wo)],
        out_specs=row(D_MODEL),
        out_shape=jax.ShapeDtypeStruct((s, D_MODEL), F32),
        compiler_params=_params(1),
        name="merge_out",
    )(ya, yb, gates, x, woa, wob, wo)


def _ffn_kernel(x_ref, g_ref, w1_ref, w2_ref, gf_ref, o_ref, u_sc, *, final_norm):
    f = pl.program_id(1)

    @pl.when(f == 0)
    def _init():
        x = x_ref[...]
        u_sc[...] = _rms(x, g_ref[...]).astype(BF16)
        o_ref[...] = x

    h = jnp.dot(u_sc[...], w1_ref[...], preferred_element_type=F32)
    a = jnp.square(jnp.maximum(h, 0.0)).astype(BF16)
    o_ref[...] += jnp.dot(a, w2_ref[...], preferred_element_type=F32)

    if final_norm:
        @pl.when(f == pl.num_programs(1) - 1)
        def _final():
            o_ref[...] = _rms(o_ref[...], gf_ref[...])


def _ffn(x, g, w1, w2, l, g_final, final_norm, tm=512, tf=1024):
    s, d = x.shape
    return pl.pallas_call(
        functools.partial(_ffn_kernel, final_norm=final_norm),
        grid=(s // tm, w1.shape[2] // tf),
        in_specs=[pl.BlockSpec((tm, d), lambda i, f: (i, 0)),
                  _layer_spec((1, d), l, lambda i, f: (0, 0)),
                  _layer_spec((d, tf), l, lambda i, f: (0, f)),
                  _layer_spec((tf, d), l, lambda i, f: (f, 0)),
                  pl.BlockSpec((1, d), lambda i, f: (0, 0))],
        out_specs=pl.BlockSpec((tm, d), lambda i, f: (i, 0)),
        out_shape=jax.ShapeDtypeStruct((s, d), F32),
        scratch_shapes=[pltpu.VMEM((tm, d), BF16)],
        compiler_params=_params(2),
        name="ffn",
    )(x, g, w1, w2, g_final)


def _rope_tables(s):
    pos = jnp.arange(s, dtype=F32)
    inv_freq = 1.0 / (ROPE_THETA ** (jnp.arange(0, QK_ROPE, 2, dtype=F32) / QK_ROPE))
    ang = pos[:, None] * inv_freq[None, :]
    cos, sin = jnp.cos(ang), jnp.sin(ang)
    half = QK_ROPE // 2
    z = lambda w: jnp.zeros((s, w), F32)
    cos_t = jnp.concatenate([cos, cos, z(LANES - QK_ROPE)], axis=1)
    sl_t = jnp.concatenate([-sin, z(LANES - half)], axis=1)
    sr_t = jnp.concatenate([z(half), sin, z(LANES - QK_ROPE)], axis=1)
    return cos_t, sl_t, sr_t


def kernel(x, norm_mix, w_in, norm_qa, w_uq, norm_kva, w_ukv, rpb, w_o_mla, w_o_na, w_out,
           norm_mlp, w_ff1, w_ff2, norm_final):
    b, s, d = x.shape
    assert (b, s, d) == (1, SEQ, D_MODEL)
    depth = w_in.shape[0]
    cos_t, sl_t, sr_t = _rope_tables(s)

    w_in_t = w_in.transpose(0, 2, 1)
    uq = w_uq.reshape(depth, Q_LORA, MLA_HEADS, QK_NOPE + QK_ROPE)
    w_qn = uq[..., :QK_NOPE].reshape(depth, Q_LORA, MLA_HEADS * QK_NOPE).astype(BF16)
    w_qp = jnp.pad(uq[..., QK_NOPE:], ((0, 0), (0, 0), (0, 0), (0, LANES - QK_ROPE)))
    w_qp = w_qp.reshape(depth, Q_LORA, MLA_HEADS * LANES).astype(BF16)
    ukv = w_ukv.reshape(depth, KV_LORA, MLA_HEADS, QK_NOPE + V_HEAD)
    w_kn = ukv[..., :QK_NOPE].reshape(depth, KV_LORA, MLA_HEADS * QK_NOPE).astype(BF16)
    w_vt = ukv[..., QK_NOPE:].reshape(depth, KV_LORA, MLA_HEADS * V_HEAD).transpose(0, 2, 1).astype(BF16)
    w_oa = w_o_mla.astype(BF16)
    w_ob = w_o_na.astype(BF16)
    w_o = w_out.astype(BF16)
    w_1 = w_ff1.astype(BF16)
    w_2 = w_ff2.astype(BF16)
    g_qa = norm_qa.reshape(depth, 1, Q_LORA)
    g_kva = norm_kva.reshape(depth, 1, KV_LORA)
    g_mlp = norm_mlp.reshape(depth, 1, d)
    g_mix = norm_mix.reshape(depth, 1, d)
    g_final = norm_final.reshape(1, d)

    xs = x.reshape(s, d)
    for l in range(depth):
        u, c = _norm_proj(xs, g_mix, w_in_t, l)
        qk_na = _mm_nt(u, w_in_t, l, ROW_QK, 2 * NA_W, BF16, PROJ_TM, PROJ_TN, name="proj_na_qk")
        vt_na = _proj_vt(u, w_in_t, l)
        gates = _mm_nt(u, w_in_t, l, ROW_G, 2 * D_MODEL, F32, PROJ_TM, PROJ_TN, name="proj_gate")
        q, k, vt = _mla_up(c, g_qa, g_kva, w_qn, w_qp, w_kn, w_vt, l, cos_t, sl_t, sr_t)
        y_a = _mla_attn(q, k, vt)
        y_b = _na_attn(qk_na, vt_na, _na_pair_tables(rpb[l]))
        xs = _merge(y_a, y_b, gates, xs, w_oa, w_ob, w_o, l)
        xs = _ffn(xs, g_mlp, w_1, w_2, l, g_final, final_norm=(l == depth - 1))
    return xs.reshape(b, s, d)
```

```python
import functools

import jax
import jax.numpy as jnp
import numpy as np
from jax import lax
from jax.experimental import pallas as pl
from jax.experimental.pallas import tpu as pltpu

D_MODEL = 2048
SEQ = 8192
GRID_W = 64
ROWS = SEQ // GRID_W
MLA_HEADS = 8
Q_LORA = 512
KV_LORA = 512
QK_NOPE = 128
QK_ROPE = 64
V_HEAD = 128
ROPE_THETA = 10000.0
NA_HEADS = 8
NA_HEAD_DIM = 128
NA_KH = 8
NA_KW = 16
D_FF = 4 * D_MODEL
EPS = 1e-6
LOG2E = float(np.log2(np.e))

LANES = 128
MLA_QK_PAD = 2 * LANES
BF16_SUBLANES = 16
V_ROWS = V_HEAD + BF16_SUBLANES
MLA_REF_KEYS = 128
MLA_MAX_GAP = 60.0
NA_Q_ROWS = 8
NA_WIN_ROWS = 2 * NA_KH
NA_TQ = NA_Q_ROWS * GRID_W
NA_TK = NA_WIN_ROWS * GRID_W
NA_BLOCKS = SEQ // NA_TQ
NA_STRIP = NA_KH * GRID_W
NA_PAIR_ROWS = NA_STRIP + GRID_W
NA_PAIR_KINDS = 1 + NA_KH // 2
VMEM_LIMIT = 56 * 1024 * 1024

BF16 = jnp.bfloat16
F32 = jnp.float32
NT_DIMS = (((1,), (1,)), ((), ()))


def _params(n_axes):
    return pltpu.CompilerParams(
        dimension_semantics=("arbitrary",) * n_axes, vmem_limit_bytes=VMEM_LIMIT)


def _rms(x, g):
    return x * lax.rsqrt(jnp.mean(x * x, axis=-1, keepdims=True) + EPS) * g


def _layer_spec(shape, l, index_map):
    return pl.BlockSpec((None,) + tuple(shape), lambda *ids: (l,) + tuple(index_map(*ids)))


def _store_vt_with_ones(vt_ref, vt, heads):
    for h in range(heads):
        vt_ref[h * V_ROWS:h * V_ROWS + V_HEAD, :] = vt[h * V_HEAD:(h + 1) * V_HEAD, :]
        vt_ref[h * V_ROWS + V_HEAD:(h + 1) * V_ROWS, :] = jnp.ones((V_ROWS - V_HEAD, vt.shape[1]), BF16)


C_END = Q_LORA + KV_LORA + QK_ROPE
C_PAD = C_END + LANES - QK_ROPE
NA_W = NA_HEADS * NA_HEAD_DIM
ROW_QK = C_END
ROW_V = ROW_QK + 2 * NA_W
ROW_G = ROW_V + NA_W
PROJ_TM = 1024
PROJ_TN = 1024


def _wt_spec(rows, k, l, row_map):
    return pl.BlockSpec((pl.Element(1), pl.Element(rows), pl.Element(k)),
                        lambda *ids: (l, row_map(*ids), 0))


def _mm_nt_kernel(a_ref, wt_ref, o_ref, wb_sc):
    @pl.when(pl.program_id(1) == 0)
    def _cast():
        wb_sc[...] = wt_ref[0].astype(BF16)

    o_ref[...] = lax.dot_general(a_ref[...], wb_sc[...], NT_DIMS,
                                 preferred_element_type=F32).astype(o_ref.dtype)


def _mm_nt(a, wt, l, row0, n, out_dtype, tm, tn, name):
    m, k = a.shape
    return pl.pallas_call(
        _mm_nt_kernel,
        grid=(n // tn, m // tm),
        in_specs=[pl.BlockSpec((tm, k), lambda j, i: (i, 0)),
                  _wt_spec(tn, k, l, lambda j, i: pl.multiple_of(row0 + j * tn, QK_ROPE))],
        out_specs=pl.BlockSpec((tm, tn), lambda j, i: (i, j)),
        out_shape=jax.ShapeDtypeStruct((m, n), out_dtype),
        scratch_shapes=[pltpu.VMEM((tn, k), BF16)],
        compiler_params=_params(2),
        name=name,
    )(a, wt)


def _norm_proj_kernel(x_ref, g_ref, wt_ref, u_ref, c_ref, wb_sc):
    @pl.when(pl.program_id(0) == 0)
    def _cast():
        wb_sc[...] = wt_ref[0].astype(BF16)

    u = _rms(x_ref[...], g_ref[...]).astype(BF16)
    u_ref[...] = u
    c_ref[...] = lax.dot_general(u, wb_sc[...], NT_DIMS, preferred_element_type=F32)


def _norm_proj(x, g, wt, l, tm=512):
    s, d = x.shape
    return pl.pallas_call(
        _norm_proj_kernel,
        grid=(s // tm,),
        in_specs=[pl.BlockSpec((tm, d), lambda i: (i, 0)),
                  _layer_spec((1, d), l, lambda i: (0, 0)),
                  _wt_spec(C_PAD, d, l, lambda i: 0)],
        out_specs=[pl.BlockSpec((tm, d), lambda i: (i, 0)), pl.BlockSpec((tm, C_PAD), lambda i: (i, 0))],
        out_shape=[jax.ShapeDtypeStruct((s, d), BF16), jax.ShapeDtypeStruct((s, C_PAD), F32)],
        scratch_shapes=[pltpu.VMEM((C_PAD, d), BF16)],
        compiler_params=_params(1),
        name="norm_proj_c",
    )(x, g, wt)


def _proj_vt_kernel(u_ref, wt_ref, vt_ref, wb_sc):
    @pl.when(pl.program_id(0) == 0)
    def _cast():
        wb_sc[...] = wt_ref[0].astype(BF16)

    vt = lax.dot_general(wb_sc[...], u_ref[...], NT_DIMS, preferred_element_type=F32).astype(BF16)
    _store_vt_with_ones(vt_ref, vt, NA_HEADS)


def _proj_vt(u, wt, l, tm=1024):
    s, k = u.shape
    return pl.pallas_call(
        _proj_vt_kernel,
        grid=(s // tm,),
        in_specs=[pl.BlockSpec((tm, k), lambda i: (i, 0)),
                  _wt_spec(NA_W, k, l, lambda i: ROW_V)],
        out_specs=pl.BlockSpec((NA_HEADS * V_ROWS, tm), lambda i: (0, i)),
        out_shape=jax.ShapeDtypeStruct((NA_HEADS * V_ROWS, s), BF16),
        scratch_shapes=[pltpu.VMEM((NA_W, k), BF16)],
        compiler_params=_params(1),
        name="proj_na_vt",
    )(u, wt)


def _rope128(x, c, sl, sr):
    return x * c + pltpu.roll(x, LANES - QK_ROPE // 2, 1) * sl + pltpu.roll(x, QK_ROPE // 2, 1) * sr


def _mla_up_kernel(c_ref, gq_ref, gkv_ref, wqn_ref, wqp_ref, wkn_ref, wvt_ref,
                   cos_ref, sl_ref, sr_ref, q_ref, k_ref, vt_ref):
    scale = (QK_NOPE + QK_ROPE) ** -0.5 * LOG2E
    cq = _rms(c_ref[:, :Q_LORA], gq_ref[...]).astype(BF16)
    ckv = _rms(c_ref[:, Q_LORA:Q_LORA + KV_LORA], gkv_ref[...]).astype(BF16)
    kpe = c_ref[:, Q_LORA + KV_LORA:]
    kpe = jnp.where(lax.broadcasted_iota(jnp.int32, kpe.shape, 1) < QK_ROPE, kpe, 0.0)
    cos, sl, sr = cos_ref[...], sl_ref[...], sr_ref[...]
    kpe_rot = _rope128(kpe, cos, sl, sr).astype(BF16)
    qn = jnp.dot(cq, wqn_ref[...], preferred_element_type=F32) * scale
    qp = jnp.dot(cq, wqp_ref[...], preferred_element_type=F32)
    kn = jnp.dot(ckv, wkn_ref[...], preferred_element_type=F32)
    vt = lax.dot_general(wvt_ref[...], ckv, NT_DIMS, preferred_element_type=F32).astype(BF16)
    _store_vt_with_ones(vt_ref, vt, MLA_HEADS)
    for h in range(MLA_HEADS):
        lo = h * MLA_QK_PAD
        hs = slice(h * LANES, (h + 1) * LANES)
        q_ref[:, lo:lo + LANES] = qn[:, hs].astype(BF16)
        q_ref[:, lo + LANES:lo + 2 * LANES] = (_rope128(qp[:, hs], cos, sl, sr) * scale).astype(BF16)
        k_ref[:, lo:lo + LANES] = kn[:, hs].astype(BF16)
        k_ref[:, lo + LANES:lo + 2 * LANES] = kpe_rot


def _mla_up(c, gq, gkv, wqn, wqp, wkn, wvt, l, cos_t, sl_t, sr_t, tm=512):
    s = c.shape[0]
    row = lambda w: pl.BlockSpec((tm, w), lambda i: (i, 0))
    full = lambda a: _layer_spec(a.shape[1:], l, lambda i: (0, 0))
    hw = MLA_HEADS * V_ROWS
    return pl.pallas_call(
        _mla_up_kernel,
        grid=(s // tm,),
        in_specs=[row(c.shape[1]), full(gq), full(gkv), full(wqn), full(wqp), full(wkn), full(wvt),
                  row(LANES), row(LANES), row(LANES)],
        out_specs=[row(MLA_HEADS * MLA_QK_PAD), row(MLA_HEADS * MLA_QK_PAD),
                   pl.BlockSpec((hw, tm), lambda i: (0, i))],
        out_shape=[jax.ShapeDtypeStruct((s, MLA_HEADS * MLA_QK_PAD), BF16),
                   jax.ShapeDtypeStruct((s, MLA_HEADS * MLA_QK_PAD), BF16),
                   jax.ShapeDtypeStruct((hw, s), BF16)],
        compiler_params=_params(1),
        name="mla_up",
    )(c, gq, gkv, wqn, wqp, wkn, wvt, cos_t, sl_t, sr_t)


def _mla_attn_kernel(*refs, tk, n_cast):
    q_ref, k_ref, vt_ref = refs[:3]
    cast_src = refs[3:3 + n_cast]
    o_ref = refs[3 + n_cast]
    cast_dst = refs[4 + n_cast:4 + 2 * n_cast]
    r_sc, mx_sc, m_sc, acc_sc, sa_sc, sb_sc, ma_sc, mb_sc = refs[4 + 2 * n_cast:]

    for src, dst in zip(cast_src, cast_dst):
        dst[...] = src[...].astype(BF16)

    q = q_ref[...]
    n_chunks = k_ref.shape[0] // tk
    assert n_chunks % 2 == 0

    def scores(c):
        k = k_ref[pl.ds(pl.multiple_of(c * tk, tk), tk), :]
        return lax.dot_general(k, q, NT_DIMS, preferred_element_type=F32)

    def values(c, pt):
        vt = vt_ref[:, pl.ds(pl.multiple_of(c * tk, tk), tk)]
        return jnp.dot(vt, pt.astype(BF16), preferred_element_type=F32)

    def write_out():
        o_ref[...] = (acc_sc[:V_HEAD, :] / acc_sc[V_HEAD:V_HEAD + 1, :]).T.astype(o_ref.dtype)

    s_ref = lax.dot_general(k_ref[:MLA_REF_KEYS, :], q, NT_DIMS, preferred_element_type=F32)
    r = jnp.max(s_ref, axis=0, keepdims=True)
    r_sc[...] = r
    mx_sc[...] = r
    acc_sc[...] = jnp.zeros(acc_sc.shape, F32)

    def step(c):
        st = scores(c)
        mx_sc[...] = jnp.maximum(mx_sc[...], jnp.max(st, axis=0, keepdims=True))
        acc_sc[...] += values(c, jnp.exp2(st - r_sc[...]))

    def single_pass_body(j, carry):
        step(2 * j)
        step(2 * j + 1)
        return carry

    lax.fori_loop(0, n_chunks // 2, single_pass_body, 0)
    in_range = jnp.max(mx_sc[...] - r_sc[...]) <= MLA_MAX_GAP

    @pl.when(in_range)
    def _accept():
        write_out()

    @pl.when(jnp.logical_not(in_range))
    def _online_softmax():
        m_sc[...] = jnp.full(m_sc.shape, -jnp.inf, F32)
        acc_sc[...] = jnp.zeros(acc_sc.shape, F32)

        def stage(c, s_sc, cm_sc):
            st = scores(c)
            s_sc[...] = st
            cm_sc[...] = jnp.max(st, axis=0, keepdims=True)

        def update(c, s_sc, cm_sc):
            m_prev = m_sc[...]
            m_new = jnp.maximum(m_prev, cm_sc[...])
            alpha = jnp.exp2(m_prev - m_new)
            acc_sc[...] = alpha * acc_sc[...] + values(c, jnp.exp2(s_sc[...] - m_new))
            m_sc[...] = m_new

        stage(0, sa_sc, ma_sc)

        def body(j, carry):
            c = 2 * j
            stage(c + 1, sb_sc, mb_sc)
            update(c, sa_sc, ma_sc)
            stage(c + 2, sa_sc, ma_sc)
            update(c + 1, sb_sc, mb_sc)
            return carry

        lax.fori_loop(0, n_chunks // 2 - 1, body, 0)
        stage(n_chunks - 1, sb_sc, mb_sc)
        update(n_chunks - 2, sa_sc, ma_sc)
        update(n_chunks - 1, sb_sc, mb_sc)
        write_out()


def _mla_attn(q, k, vt, casts=(), l=0, tq=2048, tk=1024):
    s = q.shape[0]
    nq = s // tq
    steps = MLA_HEADS * nq
    share = lambda w: (w.shape[1] // steps, w.shape[2])
    step_row = lambda h, i: (h * nq + i, 0)
    for w in casts:
        assert w.shape[1] % (steps * BF16_SUBLANES) == 0
    return pl.pallas_call(
        functools.partial(_mla_attn_kernel, tk=tk, n_cast=len(casts)),
        grid=(MLA_HEADS, nq),
        in_specs=[pl.BlockSpec((tq, MLA_QK_PAD), lambda h, i: (i, h)),
                  pl.BlockSpec((s, MLA_QK_PAD), lambda h, i: (0, h)),
                  pl.BlockSpec((V_ROWS, s), lambda h, i: (h, 0))]
                 + [_layer_spec(share(w), l, step_row) for w in casts],
        out_specs=[pl.BlockSpec((tq, V_HEAD), lambda h, i: (i, h))]
                  + [pl.BlockSpec(share(w), step_row) for w in casts],
        out_shape=[jax.ShapeDtypeStruct((s, MLA_HEADS * V_HEAD), BF16)]
                  + [jax.ShapeDtypeStruct(w.shape[1:], BF16) for w in casts],
        scratch_shapes=[pltpu.VMEM((1, tq), F32), pltpu.VMEM((1, tq), F32), pltpu.VMEM((1, tq), F32),
                        pltpu.VMEM((V_ROWS, tq), F32),
                        pltpu.VMEM((tk, tq), F32), pltpu.VMEM((tk, tq), F32),
                        pltpu.VMEM((1, tq), F32), pltpu.VMEM((1, tq), F32)],
        compiler_params=_params(2),
        name="mla_attn",
    )(q, k, vt, *casts)


def _na_row_start(r):
    return min(max(r - NA_KH // 2, 0), ROWS - NA_KH)


def _na_win_start(blk):
    return min(max(blk * NA_Q_ROWS - NA_KH // 2, 0), ROWS - NA_WIN_ROWS)


def _na_pair_placement(blk, pair):
    r_e = blk * NA_Q_ROWS + 2 * pair
    r_o = r_e + 1
    off_e = _na_row_start(r_e) - _na_win_start(blk)
    off_o = _na_row_start(r_o) - _na_win_start(blk)
    dy_e = _na_row_start(r_e) - r_e + (NA_KH - 1)
    dy_o = _na_row_start(r_o) - r_o + (NA_KH - 1)
    if off_o == off_e + 1:
        assert dy_e == dy_o == NA_KH // 2 - 1
        return 0, off_e * GRID_W, NA_PAIR_ROWS
    assert off_o == off_e and dy_e == dy_o + 1 and dy_e % 2 == 1
    return 1 + (NA_KH - 1 - dy_e) // 2, off_e * GRID_W, NA_STRIP


def _na_pair_tables(rpb_l):
    c = np.arange(GRID_W)
    col_start = np.clip(c - NA_KW // 2, 0, GRID_W - NA_KW)
    col_ok = (c[None, :] >= col_start[:, None]) & (c[None, :] < col_start[:, None] + NA_KW)
    dx = np.clip(c[None, :] - c[:, None], -(NA_KW - 1), NA_KW - 1) + (NA_KW - 1)
    rpb_l = rpb_l.astype(F32)
    toe = jnp.full((NA_HEADS, 2 * NA_KH - 1, GRID_W, GRID_W), -jnp.inf, F32)
    for d in range(2 * NA_KW - 1):
        toe = jnp.where(((dx == d) & col_ok).T, rpb_l[:, :, d][:, :, None, None], toe)

    def strip(dy0, lo, hi):
        t = toe[:, dy0:dy0 + NA_KH].reshape(NA_HEADS, NA_STRIP, GRID_W)
        return jnp.pad(t, ((0, 0), (lo, hi), (0, 0)), constant_values=-jnp.inf)

    mid = NA_KH // 2 - 1
    kinds = [jnp.concatenate([strip(mid, 0, GRID_W), strip(mid, GRID_W, 0)], axis=2)]
    for j in range(NA_KH // 2):
        dy_e = NA_KH - 1 - 2 * j
        kinds.append(jnp.concatenate([strip(dy_e, 0, GRID_W), strip(dy_e - 1, 0, GRID_W)], axis=2))
    return jnp.stack(kinds, axis=1)


def _na_attn_kernel(q_ref, k_ref, vt_ref, t_ref, o_ref, bias_sc, sa_sc, sb_sc, ma_sc, mb_sc):
    last = NA_BLOCKS - 1
    bias_sc[...] = jnp.full(bias_sc.shape, -jnp.inf, F32)
    for var, blk in enumerate((0, 1, last)):
        for pair in range(NA_Q_ROWS // 2):
            kind, top, rows = _na_pair_placement(blk, pair)
            bias_sc[var, top:top + rows, pair * LANES:(pair + 1) * LANES] = t_ref[0, kind, :rows, :] * LOG2E

    def win_start(blk):
        start = jnp.clip(blk * NA_TQ - (NA_KH // 2) * GRID_W, 0, SEQ - NA_TK)
        return pl.multiple_of(start, (NA_KH // 2) * GRID_W)

    def scores(blk, var, s_sc, cm_sc):
        q = q_ref[pl.ds(pl.multiple_of(blk * NA_TQ, NA_TQ), NA_TQ), :]
        k = k_ref[pl.ds(win_start(blk), NA_TK), :]
        st = lax.dot_general(k, q, NT_DIMS, preferred_element_type=F32)
        st = st * (NA_HEAD_DIM ** -0.5 * LOG2E) + bias_sc[var]
        s_sc[...] = st
        cm_sc[...] = jnp.max(st, axis=0, keepdims=True)

    def finish(blk, s_sc, cm_sc):
        vt = vt_ref[:, pl.ds(win_start(blk), NA_TK)]
        pt = jnp.exp2(s_sc[...] - cm_sc[...])
        acc = jnp.dot(vt, pt.astype(BF16), preferred_element_type=F32)
        o = (acc[:V_HEAD, :] / acc[V_HEAD:V_HEAD + 1, :]).T
        o_ref[pl.ds(pl.multiple_of(blk * NA_TQ, NA_TQ), NA_TQ), :] = o.astype(o_ref.dtype)

    assert NA_BLOCKS % 2 == 0
    scores(0, 0, sa_sc, ma_sc)

    def body(j, carry):
        b = 2 * j
        scores(b + 1, 1, sb_sc, mb_sc)
        finish(b, sa_sc, ma_sc)
        scores(b + 2, 1, sa_sc, ma_sc)
        finish(b + 1, sb_sc, mb_sc)
        return carry

    lax.fori_loop(0, NA_BLOCKS // 2 - 1, body, 0)
    scores(last, 2, sb_sc, mb_sc)
    finish(last - 1, sa_sc, ma_sc)
    finish(last, sb_sc, mb_sc)


def _na_attn(qk, vt, tables):
    s = qk.shape[0]
    return pl.pallas_call(
        _na_attn_kernel,
        grid=(NA_HEADS,),
        in_specs=[pl.BlockSpec((s, NA_HEAD_DIM), lambda h: (0, h)),
                  pl.BlockSpec((s, NA_HEAD_DIM), lambda h: (0, NA_HEADS + h)),
                  pl.BlockSpec((V_ROWS, s), lambda h: (h, 0)),
                  pl.BlockSpec((1, NA_PAIR_KINDS, NA_PAIR_ROWS, LANES), lambda h: (h, 0, 0, 0))],
        out_specs=pl.BlockSpec((s, NA_HEAD_DIM), lambda h: (0, h)),
        out_shape=jax.ShapeDtypeStruct((s, NA_HEADS * NA_HEAD_DIM), BF16),
        scratch_shapes=[pltpu.VMEM((3, NA_TK, NA_TQ), F32),
                        pltpu.VMEM((NA_TK, NA_TQ), F32), pltpu.VMEM((NA_TK, NA_TQ), F32),
                        pltpu.VMEM((1, NA_TQ), F32), pltpu.VMEM((1, NA_TQ), F32)],
        compiler_params=_params(1),
        name="na_attn",
    )(qk, qk, vt, tables)


def _merge_kernel(ya_ref, yb_ref, g_ref, x_ref, woa_ref, wob_ref, wo_ref, o_ref):
    a = jnp.dot(ya_ref[...], woa_ref[...], preferred_element_type=F32)
    b = jnp.dot(yb_ref[...], wob_ref[...], preferred_element_type=F32)
    ga = jax.nn.sigmoid(g_ref[:, :D_MODEL])
    gb = jax.nn.sigmoid(g_ref[:, D_MODEL:])
    merged = (ga * a + gb * b).astype(BF16)
    o_ref[...] = x_ref[...] + jnp.dot(merged, wo_ref[...], preferred_element_type=F32)


def _merge(ya, yb, gates, x, woa, wob, wo, tm=256):
    s = ya.shape[0]
    row = lambda w: pl.BlockSpec((tm, w), lambda i: (i, 0))
    full = lambda a: pl.BlockSpec(a.shape, lambda i: (0, 0))
    return pl.pallas_call(
        _merge_kernel,
        grid=(s // tm,),
        in_specs=[row(ya.shape[1]), row(yb.shape[1]), row(gates.shape[1]), row(D_MODEL),
                  full(woa), full(wob), full(wo)],
        out_specs=row(D_MODEL),
        out_shape=jax.ShapeDtypeStruct((s, D_MODEL), F32),
        compiler_params=_params(1),
        name="merge_out",
    )(ya, yb, gates, x, woa, wob, wo)


def _ffn_kernel(x_ref, g_ref, w1_ref, w2_ref, gf_ref, o_ref, u_sc, *, final_norm):
    f = pl.program_id(1)

    @pl.when(f == 0)
    def _init():
        x = x_ref[...]
        u_sc[...] = _rms(x, g_ref[...]).astype(BF16)
        o_ref[...] = x

    h = jnp.dot(u_sc[...], w1_ref[...], preferred_element_type=F32)
    a = jnp.square(jnp.maximum(h, 0.0)).astype(BF16)
    o_ref[...] += jnp.dot(a, w2_ref[...], preferred_element_type=F32)

    if final_norm:
        @pl.when(f == pl.num_programs(1) - 1)
        def _final():
            o_ref[...] = _rms(o_ref[...], gf_ref[...])


def _ffn(x, g, w1, w2, l, g_final, final_norm, tm=512, tf=1024):
    s, d = x.shape
    return pl.pallas_call(
        functools.partial(_ffn_kernel, final_norm=final_norm),
        grid=(s // tm, w1.shape[1] // tf),
        in_specs=[pl.BlockSpec((tm, d), lambda i, f: (i, 0)),
                  _layer_spec((1, d), l, lambda i, f: (0, 0)),
                  pl.BlockSpec((d, tf), lambda i, f: (0, f)),
                  pl.BlockSpec((tf, d), lambda i, f: (f, 0)),
                  pl.BlockSpec((1, d), lambda i, f: (0, 0))],
        out_specs=pl.BlockSpec((tm, d), lambda i, f: (i, 0)),
        out_shape=jax.ShapeDtypeStruct((s, d), F32),
        scratch_shapes=[pltpu.VMEM((tm, d), BF16)],
        compiler_params=_params(2),
        name="ffn",
    )(x, g, w1, w2, g_final)


def _rope_tables(s):
    pos = jnp.arange(s, dtype=F32)
    inv_freq = 1.0 / (ROPE_THETA ** (jnp.arange(0, QK_ROPE, 2, dtype=F32) / QK_ROPE))
    ang = pos[:, None] * inv_freq[None, :]
    cos, sin = jnp.cos(ang), jnp.sin(ang)
    half = QK_ROPE // 2
    z = lambda w: jnp.zeros((s, w), F32)
    cos_t = jnp.concatenate([cos, cos, z(LANES - QK_ROPE)], axis=1)
    sl_t = jnp.concatenate([-sin, z(LANES - half)], axis=1)
    sr_t = jnp.concatenate([z(half), sin, z(LANES - QK_ROPE)], axis=1)
    return cos_t, sl_t, sr_t


def kernel(x, norm_mix, w_in, norm_qa, w_uq, norm_kva, w_ukv, rpb, w_o_mla, w_o_na, w_out,
           norm_mlp, w_ff1, w_ff2, norm_final):
    b, s, d = x.shape
    assert (b, s, d) == (1, SEQ, D_MODEL)
    depth = w_in.shape[0]
    cos_t, sl_t, sr_t = _rope_tables(s)

    w_in_t = w_in.transpose(0, 2, 1)
    uq = w_uq.reshape(depth, Q_LORA, MLA_HEADS, QK_NOPE + QK_ROPE)
    w_qn = uq[..., :QK_NOPE].reshape(depth, Q_LORA, MLA_HEADS * QK_NOPE).astype(BF16)
    w_qp = jnp.pad(uq[..., QK_NOPE:], ((0, 0), (0, 0), (0, 0), (0, LANES - QK_ROPE)))
    w_qp = w_qp.reshape(depth, Q_LORA, MLA_HEADS * LANES).astype(BF16)
    ukv = w_ukv.reshape(depth, KV_LORA, MLA_HEADS, QK_NOPE + V_HEAD)
    w_kn = ukv[..., :QK_NOPE].reshape(depth, KV_LORA, MLA_HEADS * QK_NOPE).astype(BF16)
    w_vt = ukv[..., QK_NOPE:].reshape(depth, KV_LORA, MLA_HEADS * V_HEAD).transpose(0, 2, 1).astype(BF16)
    g_qa = norm_qa.reshape(depth, 1, Q_LORA)
    g_kva = norm_kva.reshape(depth, 1, KV_LORA)
    g_mlp = norm_mlp.reshape(depth, 1, d)
    g_mix = norm_mix.reshape(depth, 1, d)
    g_final = norm_final.reshape(1, d)

    xs = x.reshape(s, d)
    for l in range(depth):
        u, c = _norm_proj(xs, g_mix, w_in_t, l)
        qk_na = _mm_nt(u, w_in_t, l, ROW_QK, 2 * NA_W, BF16, PROJ_TM, PROJ_TN, name="proj_na_qk")
        vt_na = _proj_vt(u, w_in_t, l)
        gates = _mm_nt(u, w_in_t, l, ROW_G, 2 * D_MODEL, F32, PROJ_TM, PROJ_TN, name="proj_gate")
        q, k, vt = _mla_up(c, g_qa, g_kva, w_qn, w_qp, w_kn, w_vt, l, cos_t, sl_t, sr_t)
        y_a, w_oa, w_ob, w_o, w_1, w_2 = _mla_attn(q, k, vt, (w_o_mla, w_o_na, w_out, w_ff1, w_ff2), l)
        y_b = _na_attn(qk_na, vt_na, _na_pair_tables(rpb[l]))
        xs = _merge(y_a, y_b, gates, xs, w_oa, w_ob, w_o)
        xs = _ffn(xs, g_mlp, w_1, w_2, l, g_final, final_norm=(l == depth - 1))
    return xs.reshape(b, s, d)
```

```python
import functools

import jax
import jax.numpy as jnp
import numpy as np
from jax import lax
from jax.experimental import pallas as pl
from jax.experimental.pallas import tpu as pltpu

D_MODEL = 2048
SEQ = 8192
GRID_W = 64
ROWS = SEQ // GRID_W
MLA_HEADS = 8
Q_LORA = 512
KV_LORA = 512
QK_NOPE = 128
QK_ROPE = 64
V_HEAD = 128
ROPE_THETA = 10000.0
NA_HEADS = 8
NA_HEAD_DIM = 128
NA_KH = 8
NA_KW = 16
D_FF = 4 * D_MODEL
EPS = 1e-6
LOG2E = float(np.log2(np.e))

LANES = 128
MLA_QK_PAD = 2 * LANES
BF16_SUBLANES = 16
V_ROWS = V_HEAD + BF16_SUBLANES
MLA_REF_KEYS = 128
MLA_MAX_GAP = 60.0
NA_Q_ROWS = 8
NA_WIN_ROWS = 2 * NA_KH
NA_TQ = NA_Q_ROWS * GRID_W
NA_TK = NA_WIN_ROWS * GRID_W
NA_BLOCKS = SEQ // NA_TQ
NA_STRIP = NA_KH * GRID_W
NA_PAIR_ROWS = NA_STRIP + GRID_W
NA_PAIR_KINDS = 1 + NA_KH // 2
VMEM_LIMIT = 56 * 1024 * 1024

BF16 = jnp.bfloat16
F32 = jnp.float32
NT_DIMS = (((1,), (1,)), ((), ()))


def _params(n_axes):
    return pltpu.CompilerParams(
        dimension_semantics=("arbitrary",) * n_axes, vmem_limit_bytes=VMEM_LIMIT)


def _rms(x, g):
    return x * lax.rsqrt(jnp.mean(x * x, axis=-1, keepdims=True) + EPS) * g


def _layer_spec(shape, l, index_map):
    return pl.BlockSpec((None,) + tuple(shape), lambda *ids: (l,) + tuple(index_map(*ids)))


def _store_vt_with_ones(vt_ref, vt, heads):
    for h in range(heads):
        vt_ref[h * V_ROWS:h * V_ROWS + V_HEAD, :] = vt[h * V_HEAD:(h + 1) * V_HEAD, :]
        vt_ref[h * V_ROWS + V_HEAD:(h + 1) * V_ROWS, :] = jnp.ones((V_ROWS - V_HEAD, vt.shape[1]), BF16)


C_END = Q_LORA + KV_LORA + QK_ROPE
C_PAD = C_END + LANES - QK_ROPE
NA_W = NA_HEADS * NA_HEAD_DIM
ROW_QK = C_END
ROW_V = ROW_QK + 2 * NA_W
ROW_G = ROW_V + NA_W
PROJ_TM = 1024
PROJ_TN = 1024


def _wt_spec(rows, k, l, row_map):
    return pl.BlockSpec((pl.Element(1), pl.Element(rows), pl.Element(k)),
                        lambda *ids: (l, row_map(*ids), 0))


def _mm_nt_kernel(a_ref, wt_ref, o_ref, wb_sc):
    @pl.when(pl.program_id(1) == 0)
    def _cast():
        wb_sc[...] = wt_ref[0].astype(BF16)

    o_ref[...] = lax.dot_general(a_ref[...], wb_sc[...], NT_DIMS,
                                 preferred_element_type=F32).astype(o_ref.dtype)


def _mm_nt(a, wt, l, row0, n, out_dtype, tm, tn, name):
    m, k = a.shape
    return pl.pallas_call(
        _mm_nt_kernel,
        grid=(n // tn, m // tm),
        in_specs=[pl.BlockSpec((tm, k), lambda j, i: (i, 0)),
                  _wt_spec(tn, k, l, lambda j, i: pl.multiple_of(row0 + j * tn, QK_ROPE))],
        out_specs=pl.BlockSpec((tm, tn), lambda j, i: (i, j)),
        out_shape=jax.ShapeDtypeStruct((m, n), out_dtype),
        scratch_shapes=[pltpu.VMEM((tn, k), BF16)],
        compiler_params=_params(2),
        name=name,
    )(a, wt)


def _norm_proj_kernel(x_ref, g_ref, wt_ref, u_ref, c_ref, wb_sc):
    @pl.when(pl.program_id(0) == 0)
    def _cast():
        wb_sc[...] = wt_ref[0].astype(BF16)

    u = _rms(x_ref[...], g_ref[...]).astype(BF16)
    u_ref[...] = u
    c_ref[...] = lax.dot_general(u, wb_sc[...], NT_DIMS, preferred_element_type=F32)


def _norm_proj(x, g, wt, l, tm=512):
    s, d = x.shape
    return pl.pallas_call(
        _norm_proj_kernel,
        grid=(s // tm,),
        in_specs=[pl.BlockSpec((tm, d), lambda i: (i, 0)),
                  _layer_spec((1, d), l, lambda i: (0, 0)),
                  _wt_spec(C_PAD, d, l, lambda i: 0)],
        out_specs=[pl.BlockSpec((tm, d), lambda i: (i, 0)), pl.BlockSpec((tm, C_PAD), lambda i: (i, 0))],
        out_shape=[jax.ShapeDtypeStruct((s, d), BF16), jax.ShapeDtypeStruct((s, C_PAD), F32)],
        scratch_shapes=[pltpu.VMEM((C_PAD, d), BF16)],
        compiler_params=_params(1),
        name="norm_proj_c",
    )(x, g, wt)


def _proj_vt_kernel(u_ref, wt_ref, vt_ref, wb_sc):
    @pl.when(pl.program_id(0) == 0)
    def _cast():
        wb_sc[...] = wt_ref[0].astype(BF16)

    vt = lax.dot_general(wb_sc[...], u_ref[...], NT_DIMS, preferred_element_type=F32).astype(BF16)
    _store_vt_with_ones(vt_ref, vt, NA_HEADS)


def _proj_vt(u, wt, l, tm=1024):
    s, k = u.shape
    return pl.pallas_call(
        _proj_vt_kernel,
        grid=(s // tm,),
        in_specs=[pl.BlockSpec((tm, k), lambda i: (i, 0)),
                  _wt_spec(NA_W, k, l, lambda i: ROW_V)],
        out_specs=pl.BlockSpec((NA_HEADS * V_ROWS, tm), lambda i: (0, i)),
        out_shape=jax.ShapeDtypeStruct((NA_HEADS * V_ROWS, s), BF16),
        scratch_shapes=[pltpu.VMEM((NA_W, k), BF16)],
        compiler_params=_params(1),
        name="proj_na_vt",
    )(u, wt)


def _rope128(x, c, sl, sr):
    return x * c + pltpu.roll(x, LANES - QK_ROPE // 2, 1) * sl + pltpu.roll(x, QK_ROPE // 2, 1) * sr


def _mla_up_kernel(c_ref, gq_ref, gkv_ref, wqn_ref, wqp_ref, wkn_ref, wvt_ref,
                   cos_ref, sl_ref, sr_ref, q_ref, k_ref, vt_ref):
    scale = (QK_NOPE + QK_ROPE) ** -0.5 * LOG2E
    cq = _rms(c_ref[:, :Q_LORA], gq_ref[...]).astype(BF16)
    ckv = _rms(c_ref[:, Q_LORA:Q_LORA + KV_LORA], gkv_ref[...]).astype(BF16)
    kpe = c_ref[:, Q_LORA + KV_LORA:]
    kpe = jnp.where(lax.broadcasted_iota(jnp.int32, kpe.shape, 1) < QK_ROPE, kpe, 0.0)
    cos, sl, sr = cos_ref[...], sl_ref[...], sr_ref[...]
    kpe_rot = _rope128(kpe, cos, sl, sr).astype(BF16)
    qn = jnp.dot(cq, wqn_ref[...], preferred_element_type=F32) * scale
    qp = jnp.dot(cq, wqp_ref[...], preferred_element_type=F32)
    kn = jnp.dot(ckv, wkn_ref[...], preferred_element_type=F32)
    vt = lax.dot_general(wvt_ref[...], ckv, NT_DIMS, preferred_element_type=F32).astype(BF16)
    _store_vt_with_ones(vt_ref, vt, MLA_HEADS)
    for h in range(MLA_HEADS):
        lo = h * MLA_QK_PAD
        hs = slice(h * LANES, (h + 1) * LANES)
        q_ref[:, lo:lo + LANES] = qn[:, hs].astype(BF16)
        q_ref[:, lo + LANES:lo + 2 * LANES] = (_rope128(qp[:, hs], cos, sl, sr) * scale).astype(BF16)
        k_ref[:, lo:lo + LANES] = kn[:, hs].astype(BF16)
        k_ref[:, lo + LANES:lo + 2 * LANES] = kpe_rot


def _mla_up(c, gq, gkv, wqn, wqp, wkn, wvt, l, cos_t, sl_t, sr_t, tm=1024):
    s = c.shape[0]
    row = lambda w: pl.BlockSpec((tm, w), lambda i: (i, 0))
    full = lambda a: _layer_spec(a.shape[1:], l, lambda i: (0, 0))
    hw = MLA_HEADS * V_ROWS
    return pl.pallas_call(
        _mla_up_kernel,
        grid=(s // tm,),
        in_specs=[row(c.shape[1]), full(gq), full(gkv), full(wqn), full(wqp), full(wkn), full(wvt),
                  row(LANES), row(LANES), row(LANES)],
        out_specs=[row(MLA_HEADS * MLA_QK_PAD), row(MLA_HEADS * MLA_QK_PAD),
                   pl.BlockSpec((hw, tm), lambda i: (0, i))],
        out_shape=[jax.ShapeDtypeStruct((s, MLA_HEADS * MLA_QK_PAD), BF16),
                   jax.ShapeDtypeStruct((s, MLA_HEADS * MLA_QK_PAD), BF16),
                   jax.ShapeDtypeStruct((hw, s), BF16)],
        compiler_params=_params(1),
        name="mla_up",
    )(c, gq, gkv, wqn, wqp, wkn, wvt, cos_t, sl_t, sr_t)


def _mla_attn_kernel(*refs, tk, n_cast):
    q_ref, k_ref, vt_ref = refs[:3]
    cast_src = refs[3:3 + n_cast]
    o_ref = refs[3 + n_cast]
    cast_dst = refs[4 + n_cast:4 + 2 * n_cast]
    r_sc, mx_sc, m_sc, acc_sc, sa_sc, sb_sc, ma_sc, mb_sc = refs[4 + 2 * n_cast:]

    for src, dst in zip(cast_src, cast_dst):
        dst[...] = src[...].astype(BF16)

    q = q_ref[...]
    n_chunks = k_ref.shape[0] // tk
    assert n_chunks % 2 == 0

    def scores(c):
        k = k_ref[pl.ds(pl.multiple_of(c * tk, tk), tk), :]
        return lax.dot_general(k, q, NT_DIMS, preferred_element_type=F32)

    def values(c, pt):
        vt = vt_ref[:, pl.ds(pl.multiple_of(c * tk, tk), tk)]
        return jnp.dot(vt, pt.astype(BF16), preferred_element_type=F32)

    def write_out():
        o_ref[...] = (acc_sc[:V_HEAD, :] / acc_sc[V_HEAD:V_HEAD + 1, :]).T.astype(o_ref.dtype)

    s_ref = lax.dot_general(k_ref[:MLA_REF_KEYS, :], q, NT_DIMS, preferred_element_type=F32)
    r = jnp.max(s_ref, axis=0, keepdims=True)
    r_sc[...] = r
    mx_sc[...] = r
    acc_sc[...] = jnp.zeros(acc_sc.shape, F32)

    def step(c):
        st = scores(c)
        mx_sc[...] = jnp.maximum(mx_sc[...], jnp.max(st, axis=0, keepdims=True))
        acc_sc[...] += values(c, jnp.exp2(st - r_sc[...]))

    def single_pass_body(j, carry):
        step(2 * j)
        step(2 * j + 1)
        return carry

    lax.fori_loop(0, n_chunks // 2, single_pass_body, 0)
    in_range = jnp.max(mx_sc[...] - r_sc[...]) <= MLA_MAX_GAP

    @pl.when(in_range)
    def _accept():
        write_out()

    @pl.when(jnp.logical_not(in_range))
    def _online_softmax():
        m_sc[...] = jnp.full(m_sc.shape, -jnp.inf, F32)
        acc_sc[...] = jnp.zeros(acc_sc.shape, F32)

        def stage(c, s_sc, cm_sc):
            st = scores(c)
            s_sc[...] = st
            cm_sc[...] = jnp.max(st, axis=0, keepdims=True)

        def update(c, s_sc, cm_sc):
            m_prev = m_sc[...]
            m_new = jnp.maximum(m_prev, cm_sc[...])
            alpha = jnp.exp2(m_prev - m_new)
            acc_sc[...] = alpha * acc_sc[...] + values(c, jnp.exp2(s_sc[...] - m_new))
            m_sc[...] = m_new

        stage(0, sa_sc, ma_sc)

        def body(j, carry):
            c = 2 * j
            stage(c + 1, sb_sc, mb_sc)
            update(c, sa_sc, ma_sc)
            stage(c + 2, sa_sc, ma_sc)
            update(c + 1, sb_sc, mb_sc)
            return carry

        lax.fori_loop(0, n_chunks // 2 - 1, body, 0)
        stage(n_chunks - 1, sb_sc, mb_sc)
        update(n_chunks - 2, sa_sc, ma_sc)
        update(n_chunks - 1, sb_sc, mb_sc)
        write_out()


def _mla_attn(q, k, vt, casts=(), l=0, tq=2048, tk=1024):
    s = q.shape[0]
    nq = s // tq
    steps = MLA_HEADS * nq
    share = lambda w: (w.shape[1] // steps, w.shape[2])
    step_row = lambda h, i: (h * nq + i, 0)
    for w in casts:
        assert w.shape[1] % (steps * BF16_SUBLANES) == 0
    return pl.pallas_call(
        functools.partial(_mla_attn_kernel, tk=tk, n_cast=len(casts)),
        grid=(MLA_HEADS, nq),
        in_specs=[pl.BlockSpec((tq, MLA_QK_PAD), lambda h, i: (i, h)),
                  pl.BlockSpec((s, MLA_QK_PAD), lambda h, i: (0, h)),
                  pl.BlockSpec((V_ROWS, s), lambda h, i: (h, 0))]
                 + [_layer_spec(share(w), l, step_row) for w in casts],
        out_specs=[pl.BlockSpec((tq, V_HEAD), lambda h, i: (i, h))]
                  + [pl.BlockSpec(share(w), step_row) for w in casts],
        out_shape=[jax.ShapeDtypeStruct((s, MLA_HEADS * V_HEAD), BF16)]
                  + [jax.ShapeDtypeStruct(w.shape[1:], BF16) for w in casts],
        scratch_shapes=[pltpu.VMEM((1, tq), F32), pltpu.VMEM((1, tq), F32), pltpu.VMEM((1, tq), F32),
                        pltpu.VMEM((V_ROWS, tq), F32),
                        pltpu.VMEM((tk, tq), F32), pltpu.VMEM((tk, tq), F32),
                        pltpu.VMEM((1, tq), F32), pltpu.VMEM((1, tq), F32)],
        compiler_params=_params(2),
        name="mla_attn",
    )(q, k, vt, *casts)


def _na_row_start(r):
    return min(max(r - NA_KH // 2, 0), ROWS - NA_KH)


def _na_win_start(blk):
    return min(max(blk * NA_Q_ROWS - NA_KH // 2, 0), ROWS - NA_WIN_ROWS)


def _na_pair_placement(blk, pair):
    r_e = blk * NA_Q_ROWS + 2 * pair
    r_o = r_e + 1
    off_e = _na_row_start(r_e) - _na_win_start(blk)
    off_o = _na_row_start(r_o) - _na_win_start(blk)
    dy_e = _na_row_start(r_e) - r_e + (NA_KH - 1)
    dy_o = _na_row_start(r_o) - r_o + (NA_KH - 1)
    if off_o == off_e + 1:
        assert dy_e == dy_o == NA_KH // 2 - 1
        return 0, off_e * GRID_W, NA_PAIR_ROWS
    assert off_o == off_e and dy_e == dy_o + 1 and dy_e % 2 == 1
    return 1 + (NA_KH - 1 - dy_e) // 2, off_e * GRID_W, NA_STRIP


def _na_pair_tables(rpb_l):
    c = np.arange(GRID_W)
    col_start = np.clip(c - NA_KW // 2, 0, GRID_W - NA_KW)
    col_ok = (c[None, :] >= col_start[:, None]) & (c[None, :] < col_start[:, None] + NA_KW)
    dx = np.clip(c[None, :] - c[:, None], -(NA_KW - 1), NA_KW - 1) + (NA_KW - 1)
    rpb_l = rpb_l.astype(F32)
    toe = jnp.full((NA_HEADS, 2 * NA_KH - 1, GRID_W, GRID_W), -jnp.inf, F32)
    for d in range(2 * NA_KW - 1):
        toe = jnp.where(((dx == d) & col_ok).T, rpb_l[:, :, d][:, :, None, None], toe)

    def strip(dy0, lo, hi):
        t = toe[:, dy0:dy0 + NA_KH].reshape(NA_HEADS, NA_STRIP, GRID_W)
        return jnp.pad(t, ((0, 0), (lo, hi), (0, 0)), constant_values=-jnp.inf)

    mid = NA_KH // 2 - 1
    kinds = [jnp.concatenate([strip(mid, 0, GRID_W), strip(mid, GRID_W, 0)], axis=2)]
    for j in range(NA_KH // 2):
        dy_e = NA_KH - 1 - 2 * j
        kinds.append(jnp.concatenate([strip(dy_e, 0, GRID_W), strip(dy_e - 1, 0, GRID_W)], axis=2))
    return jnp.stack(kinds, axis=1)


def _na_attn_kernel(q_ref, k_ref, vt_ref, t_ref, o_ref, bias_sc, sa_sc, sb_sc, ma_sc, mb_sc):
    last = NA_BLOCKS - 1
    bias_sc[...] = jnp.full(bias_sc.shape, -jnp.inf, F32)
    for var, blk in enumerate((0, 1, last)):
        for pair in range(NA_Q_ROWS // 2):
            kind, top, rows = _na_pair_placement(blk, pair)
            bias_sc[var, top:top + rows, pair * LANES:(pair + 1) * LANES] = t_ref[0, kind, :rows, :] * LOG2E

    def win_start(blk):
        start = jnp.clip(blk * NA_TQ - (NA_KH // 2) * GRID_W, 0, SEQ - NA_TK)
        return pl.multiple_of(start, (NA_KH // 2) * GRID_W)

    def scores(blk, var, s_sc, cm_sc):
        q = q_ref[pl.ds(pl.multiple_of(blk * NA_TQ, NA_TQ), NA_TQ), :]
        k = k_ref[pl.ds(win_start(blk), NA_TK), :]
        st = lax.dot_general(k, q, NT_DIMS, preferred_element_type=F32)
        st = st * (NA_HEAD_DIM ** -0.5 * LOG2E) + bias_sc[var]
        s_sc[...] = st
        cm_sc[...] = jnp.max(st, axis=0, keepdims=True)

    def finish(blk, s_sc, cm_sc):
        vt = vt_ref[:, pl.ds(win_start(blk), NA_TK)]
        pt = jnp.exp2(s_sc[...] - cm_sc[...])
        acc = jnp.dot(vt, pt.astype(BF16), preferred_element_type=F32)
        o = (acc[:V_HEAD, :] / acc[V_HEAD:V_HEAD + 1, :]).T
        o_ref[pl.ds(pl.multiple_of(blk * NA_TQ, NA_TQ), NA_TQ), :] = o.astype(o_ref.dtype)

    assert NA_BLOCKS % 2 == 0
    scores(0, 0, sa_sc, ma_sc)

    def body(j, carry):
        b = 2 * j
        scores(b + 1, 1, sb_sc, mb_sc)
        finish(b, sa_sc, ma_sc)
        scores(b + 2, 1, sa_sc, ma_sc)
        finish(b + 1, sb_sc, mb_sc)
        return carry

    lax.fori_loop(0, NA_BLOCKS // 2 - 1, body, 0)
    scores(last, 2, sb_sc, mb_sc)
    finish(last - 1, sa_sc, ma_sc)
    finish(last, sb_sc, mb_sc)


def _na_attn(qk, vt, tables):
    s = qk.shape[0]
    return pl.pallas_call(
        _na_attn_kernel,
        grid=(NA_HEADS,),
        in_specs=[pl.BlockSpec((s, NA_HEAD_DIM), lambda h: (0, h)),
                  pl.BlockSpec((s, NA_HEAD_DIM), lambda h: (0, NA_HEADS + h)),
                  pl.BlockSpec((V_ROWS, s), lambda h: (h, 0)),
                  pl.BlockSpec((1, NA_PAIR_KINDS, NA_PAIR_ROWS, LANES), lambda h: (h, 0, 0, 0))],
        out_specs=pl.BlockSpec((s, NA_HEAD_DIM), lambda h: (0, h)),
        out_shape=jax.ShapeDtypeStruct((s, NA_HEADS * NA_HEAD_DIM), BF16),
        scratch_shapes=[pltpu.VMEM((3, NA_TK, NA_TQ), F32),
                        pltpu.VMEM((NA_TK, NA_TQ), F32), pltpu.VMEM((NA_TK, NA_TQ), F32),
                        pltpu.VMEM((1, NA_TQ), F32), pltpu.VMEM((1, NA_TQ), F32)],
        compiler_params=_params(1),
        name="na_attn",
    )(qk, qk, vt, tables)


def _merge_kernel(ya_ref, yb_ref, g_ref, x_ref, woa_ref, wob_ref, wo_ref, o_ref):
    a = jnp.dot(ya_ref[...], woa_ref[...], preferred_element_type=F32)
    b = jnp.dot(yb_ref[...], wob_ref[...], preferred_element_type=F32)
    ga = jax.nn.sigmoid(g_ref[:, :D_MODEL])
    gb = jax.nn.sigmoid(g_ref[:, D_MODEL:])
    merged = (ga * a + gb * b).astype(BF16)
    o_ref[...] = x_ref[...] + jnp.dot(merged, wo_ref[...], preferred_element_type=F32)


def _merge(ya, yb, gates, x, woa, wob, wo, tm=256):
    s = ya.shape[0]
    row = lambda w: pl.BlockSpec((tm, w), lambda i: (i, 0))
    full = lambda a: pl.BlockSpec(a.shape, lambda i: (0, 0))
    return pl.pallas_call(
        _merge_kernel,
        grid=(s // tm,),
        in_specs=[row(ya.shape[1]), row(yb.shape[1]), row(gates.shape[1]), row(D_MODEL),
                  full(woa), full(wob), full(wo)],
        out_specs=row(D_MODEL),
        out_shape=jax.ShapeDtypeStruct((s, D_MODEL), F32),
        compiler_params=_params(1),
        name="merge_out",
    )(ya, yb, gates, x, woa, wob, wo)


def _ffn_kernel(x_ref, g_ref, w1_ref, w2_ref, gf_ref, o_ref, u_sc, *, final_norm):
    f = pl.program_id(1)

    @pl.when(f == 0)
    def _init():
        x = x_ref[...]
        u_sc[...] = _rms(x, g_ref[...]).astype(BF16)
        o_ref[...] = x

    h = jnp.dot(u_sc[...], w1_ref[...], preferred_element_type=F32)
    a = jnp.square(jnp.maximum(h, 0.0)).astype(BF16)
    o_ref[...] += jnp.dot(a, w2_ref[...], preferred_element_type=F32)

    if final_norm:
        @pl.when(f == pl.num_programs(1) - 1)
        def _final():
            o_ref[...] = _rms(o_ref[...], gf_ref[...])


def _ffn(x, g, w1, w2, l, g_final, final_norm, tm=1024, tf=512):
    s, d = x.shape
    return pl.pallas_call(
        functools.partial(_ffn_kernel, final_norm=final_norm),
        grid=(s // tm, w1.shape[1] // tf),
        in_specs=[pl.BlockSpec((tm, d), lambda i, f: (i, 0)),
                  _layer_spec((1, d), l, lambda i, f: (0, 0)),
                  pl.BlockSpec((d, tf), lambda i, f: (0, f)),
                  pl.BlockSpec((tf, d), lambda i, f: (f, 0)),
                  pl.BlockSpec((1, d), lambda i, f: (0, 0))],
        out_specs=pl.BlockSpec((tm, d), lambda i, f: (i, 0)),
        out_shape=jax.ShapeDtypeStruct((s, d), F32),
        scratch_shapes=[pltpu.VMEM((tm, d), BF16)],
        compiler_params=_params(2),
        name="ffn",
    )(x, g, w1, w2, g_final)


def _rope_tables(s):
    pos = jnp.arange(s, dtype=F32)
    inv_freq = 1.0 / (ROPE_THETA ** (jnp.arange(0, QK_ROPE, 2, dtype=F32) / QK_ROPE))
    ang = pos[:, None] * inv_freq[None, :]
    cos, sin = jnp.cos(ang), jnp.sin(ang)
    half = QK_ROPE // 2
    z = lambda w: jnp.zeros((s, w), F32)
    cos_t = jnp.concatenate([cos, cos, z(LANES - QK_ROPE)], axis=1)
    sl_t = jnp.concatenate([-sin, z(LANES - half)], axis=1)
    sr_t = jnp.concatenate([z(half), sin, z(LANES - QK_ROPE)], axis=1)
    return cos_t, sl_t, sr_t


def kernel(x, norm_mix, w_in, norm_qa, w_uq, norm_kva, w_ukv, rpb, w_o_mla, w_o_na, w_out,
           norm_mlp, w_ff1, w_ff2, norm_final):
    b, s, d = x.shape
    assert (b, s, d) == (1, SEQ, D_MODEL)
    depth = w_in.shape[0]
    cos_t, sl_t, sr_t = _rope_tables(s)

    w_in_t = w_in.transpose(0, 2, 1)
    uq = w_uq.reshape(depth, Q_LORA, MLA_HEADS, QK_NOPE + QK_ROPE)
    w_qn = uq[..., :QK_NOPE].reshape(depth, Q_LORA, MLA_HEADS * QK_NOPE).astype(BF16)
    w_qp = jnp.pad(uq[..., QK_NOPE:], ((0, 0), (0, 0), (0, 0), (0, LANES - QK_ROPE)))
    w_qp = w_qp.reshape(depth, Q_LORA, MLA_HEADS * LANES).astype(BF16)
    ukv = w_ukv.reshape(depth, KV_LORA, MLA_HEADS, QK_NOPE + V_HEAD)
    w_kn = ukv[..., :QK_NOPE].reshape(depth, KV_LORA, MLA_HEADS * QK_NOPE).astype(BF16)
    w_vt = ukv[..., QK_NOPE:].reshape(depth, KV_LORA, MLA_HEADS * V_HEAD).transpose(0, 2, 1).astype(BF16)
    g_qa = norm_qa.reshape(depth, 1, Q_LORA)
    g_kva = norm_kva.reshape(depth, 1, KV_LORA)
    g_mlp = norm_mlp.reshape(depth, 1, d)
    g_mix = norm_mix.reshape(depth, 1, d)
    g_final = norm_final.reshape(1, d)

    xs = x.reshape(s, d)
    for l in range(depth):
        u, c = _norm_proj(xs, g_mix, w_in_t, l)
        qk_na = _mm_nt(u, w_in_t, l, ROW_QK, 2 * NA_W, BF16, PROJ_TM, PROJ_TN, name="proj_na_qk")
        vt_na = _proj_vt(u, w_in_t, l)
        gates = _mm_nt(u, w_in_t, l, ROW_G, 2 * D_MODEL, F32, PROJ_TM, PROJ_TN, name="proj_gate")
        q, k, vt = _mla_up(c, g_qa, g_kva, w_qn, w_qp, w_kn, w_vt, l, cos_t, sl_t, sr_t)
        y_a, w_oa, w_ob, w_o, w_1, w_2 = _mla_attn(q, k, vt, (w_o_mla, w_o_na, w_out, w_ff1, w_ff2), l)
        y_b = _na_attn(qk_na, vt_na, _na_pair_tables(rpb[l]))
        xs = _merge(y_a, y_b, gates, xs, w_oa, w_ob, w_o)
        xs = _ffn(xs, g_mlp, w_1, w_2, l, g_final, final_norm=(l == depth - 1))
    return xs.reshape(b, s, d)
```

```python
import functools

import jax
import jax.numpy as jnp
import numpy as np
from jax import lax
from jax.experimental import pallas as pl
from jax.experimental.pallas import tpu as pltpu

D_MODEL = 2048
SEQ = 8192
GRID_W = 64
ROWS = SEQ // GRID_W
MLA_HEADS = 8
Q_LORA = 512
KV_LORA = 512
QK_NOPE = 128
QK_ROPE = 64
V_HEAD = 128
ROPE_THETA = 10000.0
NA_HEADS = 8
NA_HEAD_DIM = 128
NA_KH = 8
NA_KW = 16
D_FF = 4 * D_MODEL
EPS = 1e-6
LOG2E = float(np.log2(np.e))

LANES = 128
MLA_QK_PAD = 2 * LANES
BF16_SUBLANES = 16
V_ROWS = V_HEAD + BF16_SUBLANES
MLA_REF_KEYS = 128
MLA_MAX_GAP = 60.0
NA_Q_ROWS = 8
NA_WIN_ROWS = 2 * NA_KH
NA_TQ = NA_Q_ROWS * GRID_W
NA_TK = NA_WIN_ROWS * GRID_W
NA_BLOCKS = SEQ // NA_TQ
NA_STRIP = NA_KH * GRID_W
NA_PAIR_ROWS = NA_STRIP + GRID_W
NA_PAIR_KINDS = 1 + NA_KH // 2
VMEM_LIMIT = 56 * 1024 * 1024

BF16 = jnp.bfloat16
F32 = jnp.float32
NT_DIMS = (((1,), (1,)), ((), ()))


def _params(n_axes):
    return pltpu.CompilerParams(
        dimension_semantics=("arbitrary",) * n_axes, vmem_limit_bytes=VMEM_LIMIT)


def _rms(x, g):
    return x * lax.rsqrt(jnp.mean(x * x, axis=-1, keepdims=True) + EPS) * g


def _layer_spec(shape, l, index_map):
    return pl.BlockSpec((None,) + tuple(shape), lambda *ids: (l,) + tuple(index_map(*ids)))


def _store_vt_with_ones(vt_ref, vt, heads):
    for h in range(heads):
        vt_ref[h * V_ROWS:h * V_ROWS + V_HEAD, :] = vt[h * V_HEAD:(h + 1) * V_HEAD, :]
        vt_ref[h * V_ROWS + V_HEAD:(h + 1) * V_ROWS, :] = jnp.ones((V_ROWS - V_HEAD, vt.shape[1]), BF16)


C_END = Q_LORA + KV_LORA + QK_ROPE
C_PAD = C_END + LANES - QK_ROPE
NA_W = NA_HEADS * NA_HEAD_DIM
ROW_QK = C_END
ROW_V = ROW_QK + 2 * NA_W
ROW_G = ROW_V + NA_W
PROJ_TM = 1024
PROJ_TN = 1024


def _wt_spec(rows, k, l, row_map):
    return pl.BlockSpec((pl.Element(1), pl.Element(rows), pl.Element(k)),
                        lambda *ids: (l, row_map(*ids), 0))


def _mm_nt_kernel(a_ref, wt_ref, o_ref, wb_sc):
    @pl.when(pl.program_id(1) == 0)
    def _cast():
        wb_sc[...] = wt_ref[0].astype(BF16)

    o_ref[...] = lax.dot_general(a_ref[...], wb_sc[...], NT_DIMS,
                                 preferred_element_type=F32).astype(o_ref.dtype)


def _mm_nt(a, wt, l, row0, n, out_dtype, tm, tn, name):
    m, k = a.shape
    return pl.pallas_call(
        _mm_nt_kernel,
        grid=(n // tn, m // tm),
        in_specs=[pl.BlockSpec((tm, k), lambda j, i: (i, 0)),
                  _wt_spec(tn, k, l, lambda j, i: pl.multiple_of(row0 + j * tn, QK_ROPE))],
        out_specs=pl.BlockSpec((tm, tn), lambda j, i: (i, j)),
        out_shape=jax.ShapeDtypeStruct((m, n), out_dtype),
        scratch_shapes=[pltpu.VMEM((tn, k), BF16)],
        compiler_params=_params(2),
        name=name,
    )(a, wt)


def _norm_proj_kernel(x_ref, g_ref, wt_ref, u_ref, c_ref, wb_sc):
    @pl.when(pl.program_id(0) == 0)
    def _cast():
        wb_sc[...] = wt_ref[0].astype(BF16)

    u = _rms(x_ref[...], g_ref[...]).astype(BF16)
    u_ref[...] = u
    c_ref[...] = lax.dot_general(u, wb_sc[...], NT_DIMS, preferred_element_type=F32)


def _norm_proj(x, g, wt, l, tm=512):
    s, d = x.shape
    return pl.pallas_call(
        _norm_proj_kernel,
        grid=(s // tm,),
        in_specs=[pl.BlockSpec((tm, d), lambda i: (i, 0)),
                  _layer_spec((1, d), l, lambda i: (0, 0)),
                  _wt_spec(C_PAD, d, l, lambda i: 0)],
        out_specs=[pl.BlockSpec((tm, d), lambda i: (i, 0)), pl.BlockSpec((tm, C_PAD), lambda i: (i, 0))],
        out_shape=[jax.ShapeDtypeStruct((s, d), BF16), jax.ShapeDtypeStruct((s, C_PAD), F32)],
        scratch_shapes=[pltpu.VMEM((C_PAD, d), BF16)],
        compiler_params=_params(1),
        name="norm_proj_c",
    )(x, g, wt)


def _proj_vt_kernel(u_ref, wt_ref, vt_ref, wb_sc):
    @pl.when(pl.program_id(0) == 0)
    def _cast():
        wb_sc[...] = wt_ref[0].astype(BF16)

    vt = lax.dot_general(wb_sc[...], u_ref[...], NT_DIMS, preferred_element_type=F32).astype(BF16)
    _store_vt_with_ones(vt_ref, vt, NA_HEADS)


def _proj_vt(u, wt, l, tm=1024):
    s, k = u.shape
    return pl.pallas_call(
        _proj_vt_kernel,
        grid=(s // tm,),
        in_specs=[pl.BlockSpec((tm, k), lambda i: (i, 0)),
                  _wt_spec(NA_W, k, l, lambda i: ROW_V)],
        out_specs=pl.BlockSpec((NA_HEADS * V_ROWS, tm), lambda i: (0, i)),
        out_shape=jax.ShapeDtypeStruct((NA_HEADS * V_ROWS, s), BF16),
        scratch_shapes=[pltpu.VMEM((NA_W, k), BF16)],
        compiler_params=_params(1),
        name="proj_na_vt",
    )(u, wt)


def _rope128(x, c, sl, sr):
    return x * c + pltpu.roll(x, LANES - QK_ROPE // 2, 1) * sl + pltpu.roll(x, QK_ROPE // 2, 1) * sr


def _mla_up_kernel(c_ref, gq_ref, gkv_ref, wqn_ref, wqp_ref, wkn_ref, wvt_ref,
                   cos_ref, sl_ref, sr_ref, q_ref, k_ref, vt_ref):
    scale = (QK_NOPE + QK_ROPE) ** -0.5 * LOG2E
    cq = _rms(c_ref[:, :Q_LORA], gq_ref[...]).astype(BF16)
    ckv = _rms(c_ref[:, Q_LORA:Q_LORA + KV_LORA], gkv_ref[...]).astype(BF16)
    kpe = c_ref[:, Q_LORA + KV_LORA:]
    kpe = jnp.where(lax.broadcasted_iota(jnp.int32, kpe.shape, 1) < QK_ROPE, kpe, 0.0)
    cos, sl, sr = cos_ref[...], sl_ref[...], sr_ref[...]
    kpe_rot = _rope128(kpe, cos, sl, sr).astype(BF16)
    qn = jnp.dot(cq, wqn_ref[...], preferred_element_type=F32) * scale
    qp = jnp.dot(cq, wqp_ref[...], preferred_element_type=F32)
    kn = jnp.dot(ckv, wkn_ref[...], preferred_element_type=F32)
    vt = lax.dot_general(wvt_ref[...], ckv, NT_DIMS, preferred_element_type=F32).astype(BF16)
    _store_vt_with_ones(vt_ref, vt, MLA_HEADS)
    for h in range(MLA_HEADS):
        lo = h * MLA_QK_PAD
        hs = slice(h * LANES, (h + 1) * LANES)
        q_ref[:, lo:lo + LANES] = qn[:, hs].astype(BF16)
        q_ref[:, lo + LANES:lo + 2 * LANES] = (_rope128(qp[:, hs], cos, sl, sr) * scale).astype(BF16)
        k_ref[:, lo:lo + LANES] = kn[:, hs].astype(BF16)
        k_ref[:, lo + LANES:lo + 2 * LANES] = kpe_rot


def _mla_up(c, gq, gkv, wqn, wqp, wkn, wvt, l, cos_t, sl_t, sr_t, tm=1024):
    s = c.shape[0]
    row = lambda w: pl.BlockSpec((tm, w), lambda i: (i, 0))
    full = lambda a: _layer_spec(a.shape[1:], l, lambda i: (0, 0))
    hw = MLA_HEADS * V_ROWS
    return pl.pallas_call(
        _mla_up_kernel,
        grid=(s // tm,),
        in_specs=[row(c.shape[1]), full(gq), full(gkv), full(wqn), full(wqp), full(wkn), full(wvt),
                  row(LANES), row(LANES), row(LANES)],
        out_specs=[row(MLA_HEADS * MLA_QK_PAD), row(MLA_HEADS * MLA_QK_PAD),
                   pl.BlockSpec((hw, tm), lambda i: (0, i))],
        out_shape=[jax.ShapeDtypeStruct((s, MLA_HEADS * MLA_QK_PAD), BF16),
                   jax.ShapeDtypeStruct((s, MLA_HEADS * MLA_QK_PAD), BF16),
                   jax.ShapeDtypeStruct((hw, s), BF16)],
        compiler_params=_params(1),
        name="mla_up",
    )(c, gq, gkv, wqn, wqp, wkn, wvt, cos_t, sl_t, sr_t)


def _mla_attn_kernel(*refs, tk, n_cast):
    q_ref, k_ref, vt_ref = refs[:3]
    cast_src = refs[3:3 + n_cast]
    o_ref = refs[3 + n_cast]
    cast_dst = refs[4 + n_cast:4 + 2 * n_cast]
    r_sc, mx_sc, m_sc, acc_sc, sa_sc, sb_sc, ma_sc, mb_sc = refs[4 + 2 * n_cast:]

    for src, dst in zip(cast_src, cast_dst):
        dst[...] = src[...].astype(BF16)

    q = q_ref[...]
    n_chunks = k_ref.shape[0] // tk
    assert n_chunks % 2 == 0

    def scores(c):
        k = k_ref[pl.ds(pl.multiple_of(c * tk, tk), tk), :]
        return lax.dot_general(k, q, NT_DIMS, preferred_element_type=F32)

    def values(c, pt):
        vt = vt_ref[:, pl.ds(pl.multiple_of(c * tk, tk), tk)]
        return jnp.dot(vt, pt.astype(BF16), preferred_element_type=F32)

    def write_out():
        o_ref[...] = (acc_sc[:V_HEAD, :] / acc_sc[V_HEAD:V_HEAD + 1, :]).T.astype(o_ref.dtype)

    s_ref = lax.dot_general(k_ref[:MLA_REF_KEYS, :], q, NT_DIMS, preferred_element_type=F32)
    r = jnp.max(s_ref, axis=0, keepdims=True)
    r_sc[...] = r
    mx_sc[...] = r
    acc_sc[...] = jnp.zeros(acc_sc.shape, F32)

    def step(c):
        st = scores(c)
        mx_sc[...] = jnp.maximum(mx_sc[...], jnp.max(st, axis=0, keepdims=True))
        acc_sc[...] += values(c, jnp.exp2(st - r_sc[...]))

    def single_pass_body(j, carry):
        step(2 * j)
        step(2 * j + 1)
        return carry

    lax.fori_loop(0, n_chunks // 2, single_pass_body, 0)
    in_range = jnp.max(mx_sc[...] - r_sc[...]) <= MLA_MAX_GAP

    @pl.when(in_range)
    def _accept():
        write_out()

    @pl.when(jnp.logical_not(in_range))
    def _online_softmax():
        m_sc[...] = jnp.full(m_sc.shape, -jnp.inf, F32)
        acc_sc[...] = jnp.zeros(acc_sc.shape, F32)

        def stage(c, s_sc, cm_sc):
            st = scores(c)
            s_sc[...] = st
            cm_sc[...] = jnp.max(st, axis=0, keepdims=True)

        def update(c, s_sc, cm_sc):
            m_prev = m_sc[...]
            m_new = jnp.maximum(m_prev, cm_sc[...])
            alpha = jnp.exp2(m_prev - m_new)
            acc_sc[...] = alpha * acc_sc[...] + values(c, jnp.exp2(s_sc[...] - m_new))
            m_sc[...] = m_new

        stage(0, sa_sc, ma_sc)

        def body(j, carry):
            c = 2 * j
            stage(c + 1, sb_sc, mb_sc)
            update(c, sa_sc, ma_sc)
            stage(c + 2, sa_sc, ma_sc)
            update(c + 1, sb_sc, mb_sc)
            return carry

        lax.fori_loop(0, n_chunks // 2 - 1, body, 0)
        stage(n_chunks - 1, sb_sc, mb_sc)
        update(n_chunks - 2, sa_sc, ma_sc)
        update(n_chunks - 1, sb_sc, mb_sc)
        write_out()


def _mla_attn(q, k, vt, casts=(), l=0, tq=2048, tk=1024):
    s = q.shape[0]
    nq = s // tq
    steps = MLA_HEADS * nq
    share = lambda w: (w.shape[1] // steps, w.shape[2])
    step_row = lambda h, i: (h * nq + i, 0)
    for w in casts:
        assert w.shape[1] % (steps * BF16_SUBLANES) == 0
    return pl.pallas_call(
        functools.partial(_mla_attn_kernel, tk=tk, n_cast=len(casts)),
        grid=(MLA_HEADS, nq),
        in_specs=[pl.BlockSpec((tq, MLA_QK_PAD), lambda h, i: (i, h)),
                  pl.BlockSpec((s, MLA_QK_PAD), lambda h, i: (0, h)),
                  pl.BlockSpec((V_ROWS, s), lambda h, i: (h, 0))]
                 + [_layer_spec(share(w), l, step_row) for w in casts],
        out_specs=[pl.BlockSpec((tq, V_HEAD), lambda h, i: (i, h))]
                  + [pl.BlockSpec(share(w), step_row) for w in casts],
        out_shape=[jax.ShapeDtypeStruct((s, MLA_HEADS * V_HEAD), BF16)]
                  + [jax.ShapeDtypeStruct(w.shape[1:], BF16) for w in casts],
        scratch_shapes=[pltpu.VMEM((1, tq), F32), pltpu.VMEM((1, tq), F32), pltpu.VMEM((1, tq), F32),
                        pltpu.VMEM((V_ROWS, tq), F32),
                        pltpu.VMEM((tk, tq), F32), pltpu.VMEM((tk, tq), F32),
                        pltpu.VMEM((1, tq), F32), pltpu.VMEM((1, tq), F32)],
        compiler_params=_params(2),
        name="mla_attn",
    )(q, k, vt, *casts)


def _na_row_start(r):
    return min(max(r - NA_KH // 2, 0), ROWS - NA_KH)


def _na_win_start(blk):
    return min(max(blk * NA_Q_ROWS - NA_KH // 2, 0), ROWS - NA_WIN_ROWS)


def _na_pair_placement(blk, pair):
    r_e = blk * NA_Q_ROWS + 2 * pair
    r_o = r_e + 1
    off_e = _na_row_start(r_e) - _na_win_start(blk)
    off_o = _na_row_start(r_o) - _na_win_start(blk)
    dy_e = _na_row_start(r_e) - r_e + (NA_KH - 1)
    dy_o = _na_row_start(r_o) - r_o + (NA_KH - 1)
    if off_o == off_e + 1:
        assert dy_e == dy_o == NA_KH // 2 - 1
        return 0, off_e * GRID_W, NA_PAIR_ROWS
    assert off_o == off_e and dy_e == dy_o + 1 and dy_e % 2 == 1
    return 1 + (NA_KH - 1 - dy_e) // 2, off_e * GRID_W, NA_STRIP


def _na_pair_tables(rpb):
    depth = rpb.shape[0]
    c = np.arange(GRID_W)
    col_start = np.clip(c - NA_KW // 2, 0, GRID_W - NA_KW)
    col_ok = (c[None, :] >= col_start[:, None]) & (c[None, :] < col_start[:, None] + NA_KW)
    dx = np.clip(c[None, :] - c[:, None], -(NA_KW - 1), NA_KW - 1) + (NA_KW - 1)
    pick = ((dx.T[None] == np.arange(2 * NA_KW - 1)[:, None, None]) & col_ok.T[None]).astype(np.float32)
    toe = jnp.einsum("lhyd,dkq->lhykq", rpb.astype(F32), pick, precision=lax.Precision.HIGHEST)
    toe = jnp.where(col_ok.T, toe, -jnp.inf)

    def strip(dy0, lo, hi):
        t = toe[:, :, dy0:dy0 + NA_KH].reshape(depth, NA_HEADS, NA_STRIP, GRID_W)
        return jnp.pad(t, ((0, 0), (0, 0), (lo, hi), (0, 0)), constant_values=-jnp.inf)

    mid = NA_KH // 2 - 1
    kinds = [jnp.concatenate([strip(mid, 0, GRID_W), strip(mid, GRID_W, 0)], axis=3)]
    for j in range(NA_KH // 2):
        dy_e = NA_KH - 1 - 2 * j
        kinds.append(jnp.concatenate([strip(dy_e, 0, GRID_W), strip(dy_e - 1, 0, GRID_W)], axis=3))
    return jnp.stack(kinds, axis=2)


def _na_attn_kernel(q_ref, k_ref, vt_ref, t_ref, o_ref, bias_sc, sa_sc, sb_sc, ma_sc, mb_sc):
    last = NA_BLOCKS - 1
    bias_sc[...] = jnp.full(bias_sc.shape, -jnp.inf, F32)
    for var, blk in enumerate((0, 1, last)):
        for pair in range(NA_Q_ROWS // 2):
            kind, top, rows = _na_pair_placement(blk, pair)
            bias_sc[var, top:top + rows, pair * LANES:(pair + 1) * LANES] = t_ref[0, kind, :rows, :] * LOG2E

    def win_start(blk):
        start = jnp.clip(blk * NA_TQ - (NA_KH // 2) * GRID_W, 0, SEQ - NA_TK)
        return pl.multiple_of(start, (NA_KH // 2) * GRID_W)

    def scores(blk, var, s_sc, cm_sc):
        q = q_ref[pl.ds(pl.multiple_of(blk * NA_TQ, NA_TQ), NA_TQ), :]
        k = k_ref[pl.ds(win_start(blk), NA_TK), :]
        st = lax.dot_general(k, q, NT_DIMS, preferred_element_type=F32)
        st = st * (NA_HEAD_DIM ** -0.5 * LOG2E) + bias_sc[var]
        s_sc[...] = st
        cm_sc[...] = jnp.max(st, axis=0, keepdims=True)

    def finish(blk, s_sc, cm_sc):
        vt = vt_ref[:, pl.ds(win_start(blk), NA_TK)]
        pt = jnp.exp2(s_sc[...] - cm_sc[...])
        acc = jnp.dot(vt, pt.astype(BF16), preferred_element_type=F32)
        o = (acc[:V_HEAD, :] / acc[V_HEAD:V_HEAD + 1, :]).T
        o_ref[pl.ds(pl.multiple_of(blk * NA_TQ, NA_TQ), NA_TQ), :] = o.astype(o_ref.dtype)

    assert NA_BLOCKS % 2 == 0
    scores(0, 0, sa_sc, ma_sc)

    def body(j, carry):
        b = 2 * j
        scores(b + 1, 1, sb_sc, mb_sc)
        finish(b, sa_sc, ma_sc)
        scores(b + 2, 1, sa_sc, ma_sc)
        finish(b + 1, sb_sc, mb_sc)
        return carry

    lax.fori_loop(0, NA_BLOCKS // 2 - 1, body, 0)
    scores(last, 2, sb_sc, mb_sc)
    finish(last - 1, sa_sc, ma_sc)
    finish(last, sb_sc, mb_sc)


def _na_attn(qk, vt, tables, l):
    s = qk.shape[0]
    return pl.pallas_call(
        _na_attn_kernel,
        grid=(NA_HEADS,),
        in_specs=[pl.BlockSpec((s, NA_HEAD_DIM), lambda h: (0, h)),
                  pl.BlockSpec((s, NA_HEAD_DIM), lambda h: (0, NA_HEADS + h)),
                  pl.BlockSpec((V_ROWS, s), lambda h: (h, 0)),
                  _layer_spec((1, NA_PAIR_KINDS, NA_PAIR_ROWS, LANES), l, lambda h: (h, 0, 0, 0))],
        out_specs=pl.BlockSpec((s, NA_HEAD_DIM), lambda h: (0, h)),
        out_shape=jax.ShapeDtypeStruct((s, NA_HEADS * NA_HEAD_DIM), BF16),
        scratch_shapes=[pltpu.VMEM((3, NA_TK, NA_TQ), F32),
                        pltpu.VMEM((NA_TK, NA_TQ), F32), pltpu.VMEM((NA_TK, NA_TQ), F32),
                        pltpu.VMEM((1, NA_TQ), F32), pltpu.VMEM((1, NA_TQ), F32)],
        compiler_params=_params(1),
        name="na_attn",
    )(qk, qk, vt, tables)


def _merge_kernel(ya_ref, yb_ref, g_ref, x_ref, woa_ref, wob_ref, wo_ref, o_ref):
    a = jnp.dot(ya_ref[...], woa_ref[...], preferred_element_type=F32)
    b = jnp.dot(yb_ref[...], wob_ref[...], preferred_element_type=F32)
    ga = jax.nn.sigmoid(g_ref[:, :D_MODEL])
    gb = jax.nn.sigmoid(g_ref[:, D_MODEL:])
    merged = (ga * a + gb * b).astype(BF16)
    o_ref[...] = x_ref[...] + jnp.dot(merged, wo_ref[...], preferred_element_type=F32)


def _merge(ya, yb, gates, x, woa, wob, wo, tm=256):
    s = ya.shape[0]
    row = lambda w: pl.BlockSpec((tm, w), lambda i: (i, 0))
    full = lambda a: pl.BlockSpec(a.shape, lambda i: (0, 0))
    return pl.pallas_call(
        _merge_kernel,
        grid=(s // tm,),
        in_specs=[row(ya.shape[1]), row(yb.shape[1]), row(gates.shape[1]), row(D_MODEL),
                  full(woa), full(wob), full(wo)],
        out_specs=row(D_MODEL),
        out_shape=jax.ShapeDtypeStruct((s, D_MODEL), F32),
        compiler_params=_params(1),
        name="merge_out",
    )(ya, yb, gates, x, woa, wob, wo)


def _ffn_kernel(x_ref, g_ref, w1_ref, w2_ref, gf_ref, o_ref, u_sc, *, final_norm):
    f = pl.program_id(1)

    @pl.when(f == 0)
    def _init():
        x = x_ref[...]
        u_sc[...] = _rms(x, g_ref[...]).astype(BF16)
        o_ref[...] = x

    h = jnp.dot(u_sc[...], w1_ref[...], preferred_element_type=F32)
    a = jnp.square(jnp.maximum(h, 0.0)).astype(BF16)
    o_ref[...] += jnp.dot(a, w2_ref[...], preferred_element_type=F32)

    if final_norm:
        @pl.when(f == pl.num_programs(1) - 1)
        def _final():
            o_ref[...] = _rms(o_ref[...], gf_ref[...])


def _ffn(x, g, w1, w2, l, g_final, final_norm, tm=512, tf=1024):
    s, d = x.shape
    return pl.pallas_call(
        functools.partial(_ffn_kernel, final_norm=final_norm),
        grid=(s // tm, w1.shape[1] // tf),
        in_specs=[pl.BlockSpec((tm, d), lambda i, f: (i, 0)),
                  _layer_spec((1, d), l, lambda i, f: (0, 0)),
                  pl.BlockSpec((d, tf), lambda i, f: (0, f)),
                  pl.BlockSpec((tf, d), lambda i, f: (f, 0)),
                  pl.BlockSpec((1, d), lambda i, f: (0, 0))],
        out_specs=pl.BlockSpec((tm, d), lambda i, f: (i, 0)),
        out_shape=jax.ShapeDtypeStruct((s, d), F32),
        scratch_shapes=[pltpu.VMEM((tm, d), BF16)],
        compiler_params=_params(2),
        name="ffn",
    )(x, g, w1, w2, g_final)


def _rope_tables(s):
    pos = jnp.arange(s, dtype=F32)
    inv_freq = 1.0 / (ROPE_THETA ** (jnp.arange(0, QK_ROPE, 2, dtype=F32) / QK_ROPE))
    ang = pos[:, None] * inv_freq[None, :]
    cos, sin = jnp.cos(ang), jnp.sin(ang)
    half = QK_ROPE // 2
    z = lambda w: jnp.zeros((s, w), F32)
    cos_t = jnp.concatenate([cos, cos, z(LANES - QK_ROPE)], axis=1)
    sl_t = jnp.concatenate([-sin, z(LANES - half)], axis=1)
    sr_t = jnp.concatenate([z(half), sin, z(LANES - QK_ROPE)], axis=1)
    return cos_t, sl_t, sr_t


def kernel(x, norm_mix, w_in, norm_qa, w_uq, norm_kva, w_ukv, rpb, w_o_mla, w_o_na, w_out,
           norm_mlp, w_ff1, w_ff2, norm_final):
    b, s, d = x.shape
    assert (b, s, d) == (1, SEQ, D_MODEL)
    depth = w_in.shape[0]
    cos_t, sl_t, sr_t = _rope_tables(s)

    w_in_t = w_in.transpose(0, 2, 1)
    uq = w_uq.reshape(depth, Q_LORA, MLA_HEADS, QK_NOPE + QK_ROPE)
    w_qn = uq[..., :QK_NOPE].reshape(depth, Q_LORA, MLA_HEADS * QK_NOPE).astype(BF16)
    w_qp = jnp.pad(uq[..., QK_NOPE:], ((0, 0), (0, 0), (0, 0), (0, LANES - QK_ROPE)))
    w_qp = w_qp.reshape(depth, Q_LORA, MLA_HEADS * LANES).astype(BF16)
    ukv = w_ukv.reshape(depth, KV_LORA, MLA_HEADS, QK_NOPE + V_HEAD)
    w_kn = ukv[..., :QK_NOPE].reshape(depth, KV_LORA, MLA_HEADS * QK_NOPE).astype(BF16)
    w_vt = ukv[..., QK_NOPE:].reshape(depth, KV_LORA, MLA_HEADS * V_HEAD).transpose(0, 2, 1).astype(BF16)
    g_qa = norm_qa.reshape(depth, 1, Q_LORA)
    g_kva = norm_kva.reshape(depth, 1, KV_LORA)
    g_mlp = norm_mlp.reshape(depth, 1, d)
    g_mix = norm_mix.reshape(depth, 1, d)
    g_final = norm_final.reshape(1, d)
    na_tables = _na_pair_tables(rpb)

    xs = x.reshape(s, d)
    for l in range(depth):
        u, c = _norm_proj(xs, g_mix, w_in_t, l)
        qk_na = _mm_nt(u, w_in_t, l, ROW_QK, 2 * NA_W, BF16, PROJ_TM, PROJ_TN, name="proj_na_qk")
        vt_na = _proj_vt(u, w_in_t, l)
        gates = _mm_nt(u, w_in_t, l, ROW_G, 2 * D_MODEL, F32, PROJ_TM, PROJ_TN, name="proj_gate")
        q, k, vt = _mla_up(c, g_qa, g_kva, w_qn, w_qp, w_kn, w_vt, l, cos_t, sl_t, sr_t)
        y_a, w_oa, w_ob, w_o, w_1, w_2 = _mla_attn(q, k, vt, (w_o_mla, w_o_na, w_out, w_ff1, w_ff2), l)
        y_b = _na_attn(qk_na, vt_na, na_tables, l)
        xs = _merge(y_a, y_b, gates, xs, w_oa, w_ob, w_o)
        xs = _ffn(xs, g_mlp, w_1, w_2, l, g_final, final_norm=(l == depth - 1))
    return xs.reshape(b, s, d)
```

```python
import functools

import jax
import jax.numpy as jnp
import numpy as np
from jax import lax
from jax.experimental import pallas as pl
from jax.experimental.pallas import tpu as pltpu

D_MODEL = 2048
SEQ = 8192
GRID_W = 64
ROWS = SEQ // GRID_W
MLA_HEADS = 8
Q_LORA = 512
KV_LORA = 512
QK_NOPE = 128
QK_ROPE = 64
V_HEAD = 128
ROPE_THETA = 10000.0
NA_HEADS = 8
NA_HEAD_DIM = 128
NA_KH = 8
NA_KW = 16
D_FF = 4 * D_MODEL
EPS = 1e-6
LOG2E = float(np.log2(np.e))

LANES = 128
MLA_QK_PAD = 2 * LANES
BF16_SUBLANES = 16
V_ROWS = V_HEAD + BF16_SUBLANES
MLA_REF_KEYS = 128
MLA_MAX_GAP = 60.0
NA_Q_ROWS = 8
NA_WIN_ROWS = 2 * NA_KH
NA_TQ = NA_Q_ROWS * GRID_W
NA_TK = NA_WIN_ROWS * GRID_W
NA_BLOCKS = SEQ // NA_TQ
NA_STRIP = NA_KH * GRID_W
VMEM_LIMIT = 56 * 1024 * 1024

BF16 = jnp.bfloat16
F32 = jnp.float32
NT_DIMS = (((1,), (1,)), ((), ()))


def _params(n_axes):
    return pltpu.CompilerParams(
        dimension_semantics=("arbitrary",) * n_axes, vmem_limit_bytes=VMEM_LIMIT)


def _rms(x, g):
    return x * lax.rsqrt(jnp.mean(x * x, axis=-1, keepdims=True) + EPS) * g


def _layer_spec(shape, l, index_map):
    return pl.BlockSpec((None,) + tuple(shape), lambda *ids: (l,) + tuple(index_map(*ids)))


def _store_vt_with_ones(vt_ref, vt, heads):
    for h in range(heads):
        vt_ref[h * V_ROWS:h * V_ROWS + V_HEAD, :] = vt[h * V_HEAD:(h + 1) * V_HEAD, :]
        vt_ref[h * V_ROWS + V_HEAD:(h + 1) * V_ROWS, :] = jnp.ones((V_ROWS - V_HEAD, vt.shape[1]), BF16)


C_END = Q_LORA + KV_LORA + QK_ROPE
C_PAD = C_END + LANES - QK_ROPE
NA_W = NA_HEADS * NA_HEAD_DIM
ROW_QK = C_END
ROW_V = ROW_QK + 2 * NA_W
ROW_G = ROW_V + NA_W
PROJ_TM = 1024
PROJ_TN = 1024


def _wt_spec(rows, k, l, row_map):
    return pl.BlockSpec((pl.Element(1), pl.Element(rows), pl.Element(k)),
                        lambda *ids: (l, row_map(*ids), 0))


def _mm_nt_kernel(a_ref, wt_ref, o_ref, wb_sc):
    @pl.when(pl.program_id(1) == 0)
    def _cast():
        wb_sc[...] = wt_ref[0].astype(BF16)

    o_ref[...] = lax.dot_general(a_ref[...], wb_sc[...], NT_DIMS,
                                 preferred_element_type=F32).astype(o_ref.dtype)


def _mm_nt(a, wt, l, row0, n, out_dtype, tm, tn, name):
    m, k = a.shape
    return pl.pallas_call(
        _mm_nt_kernel,
        grid=(n // tn, m // tm),
        in_specs=[pl.BlockSpec((tm, k), lambda j, i: (i, 0)),
                  _wt_spec(tn, k, l, lambda j, i: pl.multiple_of(row0 + j * tn, QK_ROPE))],
        out_specs=pl.BlockSpec((tm, tn), lambda j, i: (i, j)),
        out_shape=jax.ShapeDtypeStruct((m, n), out_dtype),
        scratch_shapes=[pltpu.VMEM((tn, k), BF16)],
        compiler_params=_params(2),
        name=name,
    )(a, wt)


def _norm_proj_kernel(x_ref, g_ref, wt_ref, u_ref, c_ref, wb_sc):
    @pl.when(pl.program_id(0) == 0)
    def _cast():
        wb_sc[...] = wt_ref[0].astype(BF16)

    u = _rms(x_ref[...], g_ref[...]).astype(BF16)
    u_ref[...] = u
    c_ref[...] = lax.dot_general(u, wb_sc[...], NT_DIMS, preferred_element_type=F32)


def _norm_proj(x, g, wt, l, tm=512):
    s, d = x.shape
    return pl.pallas_call(
        _norm_proj_kernel,
        grid=(s // tm,),
        in_specs=[pl.BlockSpec((tm, d), lambda i: (i, 0)),
                  _layer_spec((1, d), l, lambda i: (0, 0)),
                  _wt_spec(C_PAD, d, l, lambda i: 0)],
        out_specs=[pl.BlockSpec((tm, d), lambda i: (i, 0)), pl.BlockSpec((tm, C_PAD), lambda i: (i, 0))],
        out_shape=[jax.ShapeDtypeStruct((s, d), BF16), jax.ShapeDtypeStruct((s, C_PAD), F32)],
        scratch_shapes=[pltpu.VMEM((C_PAD, d), BF16)],
        compiler_params=_params(1),
        name="norm_proj_c",
    )(x, g, wt)


def _proj_vt_kernel(u_ref, wt_ref, vt_ref, wb_sc):
    @pl.when(pl.program_id(0) == 0)
    def _cast():
        wb_sc[...] = wt_ref[0].astype(BF16)

    vt = lax.dot_general(wb_sc[...], u_ref[...], NT_DIMS, preferred_element_type=F32).astype(BF16)
    _store_vt_with_ones(vt_ref, vt, NA_HEADS)


def _proj_vt(u, wt, l, tm=1024):
    s, k = u.shape
    return pl.pallas_call(
        _proj_vt_kernel,
        grid=(s // tm,),
        in_specs=[pl.BlockSpec((tm, k), lambda i: (i, 0)),
                  _wt_spec(NA_W, k, l, lambda i: ROW_V)],
        out_specs=pl.BlockSpec((NA_HEADS * V_ROWS, tm), lambda i: (0, i)),
        out_shape=jax.ShapeDtypeStruct((NA_HEADS * V_ROWS, s), BF16),
        scratch_shapes=[pltpu.VMEM((NA_W, k), BF16)],
        compiler_params=_params(1),
        name="proj_na_vt",
    )(u, wt)


def _rope128(x, c, sl, sr):
    return x * c + pltpu.roll(x, LANES - QK_ROPE // 2, 1) * sl + pltpu.roll(x, QK_ROPE // 2, 1) * sr


def _mla_up_kernel(c_ref, gq_ref, gkv_ref, wqn_ref, wqp_ref, wkn_ref, wvt_ref,
                   cos_ref, sl_ref, sr_ref, q_ref, k_ref, vt_ref):
    scale = (QK_NOPE + QK_ROPE) ** -0.5 * LOG2E
    cq = _rms(c_ref[:, :Q_LORA], gq_ref[...]).astype(BF16)
    ckv = _rms(c_ref[:, Q_LORA:Q_LORA + KV_LORA], gkv_ref[...]).astype(BF16)
    kpe = c_ref[:, Q_LORA + KV_LORA:]
    kpe = jnp.where(lax.broadcasted_iota(jnp.int32, kpe.shape, 1) < QK_ROPE, kpe, 0.0)
    cos, sl, sr = cos_ref[...], sl_ref[...], sr_ref[...]
    kpe_rot = _rope128(kpe, cos, sl, sr).astype(BF16)
    qn = jnp.dot(cq, wqn_ref[...], preferred_element_type=F32) * scale
    qp = jnp.dot(cq, wqp_ref[...], preferred_element_type=F32)
    kn = jnp.dot(ckv, wkn_ref[...], preferred_element_type=F32)
    vt = lax.dot_general(wvt_ref[...], ckv, NT_DIMS, preferred_element_type=F32).astype(BF16)
    _store_vt_with_ones(vt_ref, vt, MLA_HEADS)
    for h in range(MLA_HEADS):
        lo = h * MLA_QK_PAD
        hs = slice(h * LANES, (h + 1) * LANES)
        q_ref[:, lo:lo + LANES] = qn[:, hs].astype(BF16)
        q_ref[:, lo + LANES:lo + 2 * LANES] = (_rope128(qp[:, hs], cos, sl, sr) * scale).astype(BF16)
        k_ref[:, lo:lo + LANES] = kn[:, hs].astype(BF16)
        k_ref[:, lo + LANES:lo + 2 * LANES] = kpe_rot


def _mla_up(c, gq, gkv, wqn, wqp, wkn, wvt, l, cos_t, sl_t, sr_t, tm=1024):
    s = c.shape[0]
    row = lambda w: pl.BlockSpec((tm, w), lambda i: (i, 0))
    full = lambda a: _layer_spec(a.shape[1:], l, lambda i: (0, 0))
    hw = MLA_HEADS * V_ROWS
    return pl.pallas_call(
        _mla_up_kernel,
        grid=(s // tm,),
        in_specs=[row(c.shape[1]), full(gq), full(gkv), full(wqn), full(wqp), full(wkn), full(wvt),
                  row(LANES), row(LANES), row(LANES)],
        out_specs=[row(MLA_HEADS * MLA_QK_PAD), row(MLA_HEADS * MLA_QK_PAD),
                   pl.BlockSpec((hw, tm), lambda i: (0, i))],
        out_shape=[jax.ShapeDtypeStruct((s, MLA_HEADS * MLA_QK_PAD), BF16),
                   jax.ShapeDtypeStruct((s, MLA_HEADS * MLA_QK_PAD), BF16),
                   jax.ShapeDtypeStruct((hw, s), BF16)],
        compiler_params=_params(1),
        name="mla_up",
    )(c, gq, gkv, wqn, wqp, wkn, wvt, cos_t, sl_t, sr_t)


def _mla_attn_kernel(*refs, tk, n_cast):
    q_ref, k_ref, vt_ref = refs[:3]
    cast_src = refs[3:3 + n_cast]
    o_ref = refs[3 + n_cast]
    cast_dst = refs[4 + n_cast:4 + 2 * n_cast]
    r_sc, mx_sc, m_sc, acc_sc, sa_sc, sb_sc, ma_sc, mb_sc = refs[4 + 2 * n_cast:]

    for src, dst in zip(cast_src, cast_dst):
        dst[...] = src[...].astype(BF16)

    q = q_ref[...]
    n_chunks = k_ref.shape[0] // tk
    assert n_chunks % 2 == 0

    def scores(c):
        k = k_ref[pl.ds(pl.multiple_of(c * tk, tk), tk), :]
        return lax.dot_general(k, q, NT_DIMS, preferred_element_type=F32)

    def values(c, pt):
        vt = vt_ref[:, pl.ds(pl.multiple_of(c * tk, tk), tk)]
        return jnp.dot(vt, pt.astype(BF16), preferred_element_type=F32)

    def write_out():
        o_ref[...] = (acc_sc[:V_HEAD, :] / acc_sc[V_HEAD:V_HEAD + 1, :]).T.astype(o_ref.dtype)

    s_ref = lax.dot_general(k_ref[:MLA_REF_KEYS, :], q, NT_DIMS, preferred_element_type=F32)
    r = jnp.max(s_ref, axis=0, keepdims=True)
    r_sc[...] = r
    mx_sc[...] = r
    acc_sc[...] = jnp.zeros(acc_sc.shape, F32)

    def step(c):
        st = scores(c)
        mx_sc[...] = jnp.maximum(mx_sc[...], jnp.max(st, axis=0, keepdims=True))
        acc_sc[...] += values(c, jnp.exp2(st - r_sc[...]))

    def single_pass_body(j, carry):
        step(2 * j)
        step(2 * j + 1)
        return carry

    lax.fori_loop(0, n_chunks // 2, single_pass_body, 0)
    in_range = jnp.max(mx_sc[...] - r_sc[...]) <= MLA_MAX_GAP

    @pl.when(in_range)
    def _accept():
        write_out()

    @pl.when(jnp.logical_not(in_range))
    def _online_softmax():
        m_sc[...] = jnp.full(m_sc.shape, -jnp.inf, F32)
        acc_sc[...] = jnp.zeros(acc_sc.shape, F32)

        def stage(c, s_sc, cm_sc):
            st = scores(c)
            s_sc[...] = st
            cm_sc[...] = jnp.max(st, axis=0, keepdims=True)

        def update(c, s_sc, cm_sc):
            m_prev = m_sc[...]
            m_new = jnp.maximum(m_prev, cm_sc[...])
            alpha = jnp.exp2(m_prev - m_new)
            acc_sc[...] = alpha * acc_sc[...] + values(c, jnp.exp2(s_sc[...] - m_new))
            m_sc[...] = m_new

        stage(0, sa_sc, ma_sc)

        def body(j, carry):
            c = 2 * j
            stage(c + 1, sb_sc, mb_sc)
            update(c, sa_sc, ma_sc)
            stage(c + 2, sa_sc, ma_sc)
            update(c + 1, sb_sc, mb_sc)
            return carry

        lax.fori_loop(0, n_chunks // 2 - 1, body, 0)
        stage(n_chunks - 1, sb_sc, mb_sc)
        update(n_chunks - 2, sa_sc, ma_sc)
        update(n_chunks - 1, sb_sc, mb_sc)
        write_out()


def _mla_attn(q, k, vt, casts=(), l=0, tq=2048, tk=1024):
    s = q.shape[0]
    nq = s // tq
    steps = MLA_HEADS * nq
    share = lambda w: (w.shape[1] // steps, w.shape[2])
    step_row = lambda h, i: (h * nq + i, 0)
    for w in casts:
        assert w.shape[1] % (steps * BF16_SUBLANES) == 0
    return pl.pallas_call(
        functools.partial(_mla_attn_kernel, tk=tk, n_cast=len(casts)),
        grid=(MLA_HEADS, nq),
        in_specs=[pl.BlockSpec((tq, MLA_QK_PAD), lambda h, i: (i, h)),
                  pl.BlockSpec((s, MLA_QK_PAD), lambda h, i: (0, h)),
                  pl.BlockSpec((V_ROWS, s), lambda h, i: (h, 0))]
                 + [_layer_spec(share(w), l, step_row) for w in casts],
        out_specs=[pl.BlockSpec((tq, V_HEAD), lambda h, i: (i, h))]
                  + [pl.BlockSpec(share(w), step_row) for w in casts],
        out_shape=[jax.ShapeDtypeStruct((s, MLA_HEADS * V_HEAD), BF16)]
                  + [jax.ShapeDtypeStruct(w.shape[1:], BF16) for w in casts],
        scratch_shapes=[pltpu.VMEM((1, tq), F32), pltpu.VMEM((1, tq), F32), pltpu.VMEM((1, tq), F32),
                        pltpu.VMEM((V_ROWS, tq), F32),
                        pltpu.VMEM((tk, tq), F32), pltpu.VMEM((tk, tq), F32),
                        pltpu.VMEM((1, tq), F32), pltpu.VMEM((1, tq), F32)],
        compiler_params=_params(2),
        name="mla_attn",
    )(q, k, vt, *casts)


def _na_row_start(r):
    return min(max(r - NA_KH // 2, 0), ROWS - NA_KH)


def _na_win_start(blk):
    return min(max(blk * NA_Q_ROWS - NA_KH // 2, 0), ROWS - NA_WIN_ROWS)


def _na_row_window(blk, r_in_blk):
    r = blk * NA_Q_ROWS + r_in_blk
    return _na_row_start(r) - _na_win_start(blk), _na_row_start(r) - r + (NA_KH - 1)


def _na_rpb_rows(rpb):
    n_dy, n_dx = 2 * NA_KH - 1, 2 * NA_KW - 1
    return jnp.pad(rpb.astype(F32)[..., ::-1], ((0, 0), (0, 0), (0, 2 * NA_KH - n_dy), (0, LANES - n_dx)))


def _na_attn_kernel(q_ref, k_ref, vt_ref, r_ref, o_ref, toe_sc, bias_sc, sa_sc, sb_sc, ma_sc, mb_sc):
    last = NA_BLOCKS - 1

    kc = lax.broadcasted_iota(jnp.int32, (GRID_W, LANES), 0)
    lane = lax.broadcasted_iota(jnp.int32, (GRID_W, LANES), 1)
    for half in range(2):
        qc = lane - half * GRID_W
        col_start = jnp.clip(qc - NA_KW // 2, 0, GRID_W - NA_KW)
        ok = (qc >= 0) & (qc < GRID_W) & (kc >= col_start) & (kc < col_start + NA_KW)
        shift = (LANES - (NA_KW - 1) + half * GRID_W) % LANES
        for dy in range(2 * NA_KH - 1):
            row = jnp.broadcast_to(r_ref[0, dy:dy + 1, :], (GRID_W, LANES))
            toe = pltpu.roll(row, shift, 1, stride=1, stride_axis=0)
            toe_sc[half, dy] = jnp.where(ok, toe * LOG2E, -jnp.inf)

    bias_sc[...] = jnp.full(bias_sc.shape, -jnp.inf, F32)
    for var, blk in enumerate((0, 1, last)):
        for pair in range(NA_Q_ROWS // 2):
            (off_e, dy_e), (off_o, dy_o) = _na_row_window(blk, 2 * pair), _na_row_window(blk, 2 * pair + 1)
            for b in range(min(off_e, off_o), max(off_e, off_o) + NA_KH):
                parts = []
                if 0 <= b - off_e < NA_KH:
                    parts.append(toe_sc[0, dy_e + b - off_e])
                if 0 <= b - off_o < NA_KH:
                    parts.append(toe_sc[1, dy_o + b - off_o])
                blkv = parts[0] if len(parts) == 1 else jnp.maximum(parts[0], parts[1])
                bias_sc[var, b * GRID_W:(b + 1) * GRID_W, pair * LANES:(pair + 1) * LANES] = blkv

    def win_start(blk):
        start = jnp.clip(blk * NA_TQ - (NA_KH // 2) * GRID_W, 0, SEQ - NA_TK)
        return pl.multiple_of(start, (NA_KH // 2) * GRID_W)

    def scores(blk, var, s_sc, cm_sc):
        q = q_ref[pl.ds(pl.multiple_of(blk * NA_TQ, NA_TQ), NA_TQ), :]
        k = k_ref[pl.ds(win_start(blk), NA_TK), :]
        st = lax.dot_general(k, q, NT_DIMS, preferred_element_type=F32)
        st = st * (NA_HEAD_DIM ** -0.5 * LOG2E) + bias_sc[var]
        s_sc[...] = st
        cm_sc[...] = jnp.max(st, axis=0, keepdims=True)

    def finish(blk, s_sc, cm_sc):
        vt = vt_ref[:, pl.ds(win_start(blk), NA_TK)]
        pt = jnp.exp2(s_sc[...] - cm_sc[...])
        acc = jnp.dot(vt, pt.astype(BF16), preferred_element_type=F32)
        o = (acc[:V_HEAD, :] / acc[V_HEAD:V_HEAD + 1, :]).T
        o_ref[pl.ds(pl.multiple_of(blk * NA_TQ, NA_TQ), NA_TQ), :] = o.astype(o_ref.dtype)

    assert NA_BLOCKS % 2 == 0
    scores(0, 0, sa_sc, ma_sc)

    def body(j, carry):
        b = 2 * j
        scores(b + 1, 1, sb_sc, mb_sc)
        finish(b, sa_sc, ma_sc)
        scores(b + 2, 1, sa_sc, ma_sc)
        finish(b + 1, sb_sc, mb_sc)
        return carry

    lax.fori_loop(0, NA_BLOCKS // 2 - 1, body, 0)
    scores(last, 2, sb_sc, mb_sc)
    finish(last - 1, sa_sc, ma_sc)
    finish(last, sb_sc, mb_sc)


def _na_attn(qk, vt, rpb_rows, l):
    s = qk.shape[0]
    return pl.pallas_call(
        _na_attn_kernel,
        grid=(NA_HEADS,),
        in_specs=[pl.BlockSpec((s, NA_HEAD_DIM), lambda h: (0, h)),
                  pl.BlockSpec((s, NA_HEAD_DIM), lambda h: (0, NA_HEADS + h)),
                  pl.BlockSpec((V_ROWS, s), lambda h: (h, 0)),
                  _layer_spec((1, 2 * NA_KH, LANES), l, lambda h: (h, 0, 0))],
        out_specs=pl.BlockSpec((s, NA_HEAD_DIM), lambda h: (0, h)),
        out_shape=jax.ShapeDtypeStruct((s, NA_HEADS * NA_HEAD_DIM), BF16),
        scratch_shapes=[pltpu.VMEM((2, 2 * NA_KH - 1, GRID_W, LANES), F32), pltpu.VMEM((3, NA_TK, NA_TQ), F32),
                        pltpu.VMEM((NA_TK, NA_TQ), F32), pltpu.VMEM((NA_TK, NA_TQ), F32),
                        pltpu.VMEM((1, NA_TQ), F32), pltpu.VMEM((1, NA_TQ), F32)],
        compiler_params=_params(1),
        name="na_attn",
    )(qk, qk, vt, rpb_rows)


def _merge_kernel(ya_ref, yb_ref, g_ref, x_ref, woa_ref, wob_ref, wo_ref, o_ref):
    a = jnp.dot(ya_ref[...], woa_ref[...], preferred_element_type=F32)
    b = jnp.dot(yb_ref[...], wob_ref[...], preferred_element_type=F32)
    ga = jax.nn.sigmoid(g_ref[:, :D_MODEL])
    gb = jax.nn.sigmoid(g_ref[:, D_MODEL:])
    merged = (ga * a + gb * b).astype(BF16)
    o_ref[...] = x_ref[...] + jnp.dot(merged, wo_ref[...], preferred_element_type=F32)


def _merge(ya, yb, gates, x, woa, wob, wo, tm=256):
    s = ya.shape[0]
    row = lambda w: pl.BlockSpec((tm, w), lambda i: (i, 0))
    full = lambda a: pl.BlockSpec(a.shape, lambda i: (0, 0))
    return pl.pallas_call(
        _merge_kernel,
        grid=(s // tm,),
        in_specs=[row(ya.shape[1]), row(yb.shape[1]), row(gates.shape[1]), row(D_MODEL),
                  full(woa), full(wob), full(wo)],
        out_specs=row(D_MODEL),
        out_shape=jax.ShapeDtypeStruct((s, D_MODEL), F32),
        compiler_params=_params(1),
        name="merge_out",
    )(ya, yb, gates, x, woa, wob, wo)


def _ffn_kernel(x_ref, g_ref, w1_ref, w2_ref, gf_ref, o_ref, u_sc, *, final_norm):
    f = pl.program_id(1)

    @pl.when(f == 0)
    def _init():
        x = x_ref[...]
        u_sc[...] = _rms(x, g_ref[...]).astype(BF16)
        o_ref[...] = x

    h = jnp.dot(u_sc[...], w1_ref[...], preferred_element_type=F32)
    a = jnp.square(jnp.maximum(h, 0.0)).astype(BF16)
    o_ref[...] += jnp.dot(a, w2_ref[...], preferred_element_type=F32)

    if final_norm:
        @pl.when(f == pl.num_programs(1) - 1)
        def _final():
            o_ref[...] = _rms(o_ref[...], gf_ref[...])


def _ffn(x, g, w1, w2, l, g_final, final_norm, tm=512, tf=1024):
    s, d = x.shape
    return pl.pallas_call(
        functools.partial(_ffn_kernel, final_norm=final_norm),
        grid=(s // tm, w1.shape[1] // tf),
        in_specs=[pl.BlockSpec((tm, d), lambda i, f: (i, 0)),
                  _layer_spec((1, d), l, lambda i, f: (0, 0)),
                  pl.BlockSpec((d, tf), lambda i, f: (0, f)),
                  pl.BlockSpec((tf, d), lambda i, f: (f, 0)),
                  pl.BlockSpec((1, d), lambda i, f: (0, 0))],
        out_specs=pl.BlockSpec((tm, d), lambda i, f: (i, 0)),
        out_shape=jax.ShapeDtypeStruct((s, d), F32),
        scratch_shapes=[pltpu.VMEM((tm, d), BF16)],
        compiler_params=_params(2),
        name="ffn",
    )(x, g, w1, w2, g_final)


def _rope_tables(s):
    pos = jnp.arange(s, dtype=F32)
    inv_freq = 1.0 / (ROPE_THETA ** (jnp.arange(0, QK_ROPE, 2, dtype=F32) / QK_ROPE))
    ang = pos[:, None] * inv_freq[None, :]
    cos, sin = jnp.cos(ang), jnp.sin(ang)
    half = QK_ROPE // 2
    z = lambda w: jnp.zeros((s, w), F32)
    cos_t = jnp.concatenate([cos, cos, z(LANES - QK_ROPE)], axis=1)
    sl_t = jnp.concatenate([-sin, z(LANES - half)], axis=1)
    sr_t = jnp.concatenate([z(half), sin, z(LANES - QK_ROPE)], axis=1)
    return cos_t, sl_t, sr_t


def kernel(x, norm_mix, w_in, norm_qa, w_uq, norm_kva, w_ukv, rpb, w_o_mla, w_o_na, w_out,
           norm_mlp, w_ff1, w_ff2, norm_final):
    b, s, d = x.shape
    assert (b, s, d) == (1, SEQ, D_MODEL)
    depth = w_in.shape[0]
    cos_t, sl_t, sr_t = _rope_tables(s)

    w_in_t = w_in.transpose(0, 2, 1)
    uq = w_uq.reshape(depth, Q_LORA, MLA_HEADS, QK_NOPE + QK_ROPE)
    w_qn = uq[..., :QK_NOPE].reshape(depth, Q_LORA, MLA_HEADS * QK_NOPE).astype(BF16)
    w_qp = jnp.pad(uq[..., QK_NOPE:], ((0, 0), (0, 0), (0, 0), (0, LANES - QK_ROPE)))
    w_qp = w_qp.reshape(depth, Q_LORA, MLA_HEADS * LANES).astype(BF16)
    ukv = w_ukv.reshape(depth, KV_LORA, MLA_HEADS, QK_NOPE + V_HEAD)
    w_kn = ukv[..., :QK_NOPE].reshape(depth, KV_LORA, MLA_HEADS * QK_NOPE).astype(BF16)
    w_vt = ukv[..., QK_NOPE:].reshape(depth, KV_LORA, MLA_HEADS * V_HEAD).transpose(0, 2, 1).astype(BF16)
    g_qa = norm_qa.reshape(depth, 1, Q_LORA)
    g_kva = norm_kva.reshape(depth, 1, KV_LORA)
    g_mlp = norm_mlp.reshape(depth, 1, d)
    g_mix = norm_mix.reshape(depth, 1, d)
    g_final = norm_final.reshape(1, d)
    rpb_rows = _na_rpb_rows(rpb)

    xs = x.reshape(s, d)
    for l in range(depth):
        u, c = _norm_proj(xs, g_mix, w_in_t, l)
        qk_na = _mm_nt(u, w_in_t, l, ROW_QK, 2 * NA_W, BF16, PROJ_TM, PROJ_TN, name="proj_na_qk")
        vt_na = _proj_vt(u, w_in_t, l)
        gates = _mm_nt(u, w_in_t, l, ROW_G, 2 * D_MODEL, F32, PROJ_TM, PROJ_TN, name="proj_gate")
        q, k, vt = _mla_up(c, g_qa, g_kva, w_qn, w_qp, w_kn, w_vt, l, cos_t, sl_t, sr_t)
        y_a, w_oa, w_ob, w_o, w_1, w_2 = _mla_attn(q, k, vt, (w_o_mla, w_o_na, w_out, w_ff1, w_ff2), l)
        y_b = _na_attn(qk_na, vt_na, rpb_rows, l)
        xs = _merge(y_a, y_b, gates, xs, w_oa, w_ob, w_o)
        xs = _ffn(xs, g_mlp, w_1, w_2, l, g_final, final_norm=(l == depth - 1))
    return xs.reshape(b, s, d)
```

```python
import functools

import jax
import jax.numpy as jnp
import numpy as np
from jax import lax
from jax.experimental import pallas as pl
from jax.experimental.pallas import tpu as pltpu

D_MODEL = 2048
SEQ = 8192
GRID_W = 64
ROWS = SEQ // GRID_W
MLA_HEADS = 8
Q_LORA = 512
KV_LORA = 512
QK_NOPE = 128
QK_ROPE = 64
V_HEAD = 128
ROPE_THETA = 10000.0
NA_HEADS = 8
NA_HEAD_DIM = 128
NA_KH = 8
NA_KW = 16
D_FF = 4 * D_MODEL
EPS = 1e-6
LOG2E = float(np.log2(np.e))

LANES = 128
MLA_QK_PAD = 2 * LANES
BF16_SUBLANES = 16
V_ROWS = V_HEAD + BF16_SUBLANES
MLA_REF_KEYS = 128
MLA_MAX_GAP = 60.0
NA_Q_ROWS = 8
NA_WIN_ROWS = 2 * NA_KH
NA_TQ = NA_Q_ROWS * GRID_W
NA_TK = NA_WIN_ROWS * GRID_W
NA_BLOCKS = SEQ // NA_TQ
NA_STRIP = NA_KH * GRID_W
VMEM_LIMIT = 56 * 1024 * 1024

BF16 = jnp.bfloat16
F32 = jnp.float32
NT_DIMS = (((1,), (1,)), ((), ()))


def _params(n_axes):
    return pltpu.CompilerParams(
        dimension_semantics=("arbitrary",) * n_axes, vmem_limit_bytes=VMEM_LIMIT)


def _rms(x, g):
    return x * lax.rsqrt(jnp.mean(x * x, axis=-1, keepdims=True) + EPS) * g


def _layer_spec(shape, l, index_map):
    return pl.BlockSpec((None,) + tuple(shape), lambda *ids: (l,) + tuple(index_map(*ids)))


def _store_vt_with_ones(vt_ref, vt, heads):
    for h in range(heads):
        vt_ref[h * V_ROWS:h * V_ROWS + V_HEAD, :] = vt[h * V_HEAD:(h + 1) * V_HEAD, :]
        vt_ref[h * V_ROWS + V_HEAD:(h + 1) * V_ROWS, :] = jnp.ones((V_ROWS - V_HEAD, vt.shape[1]), BF16)


C_END = Q_LORA + KV_LORA + QK_ROPE
C_PAD = C_END + LANES - QK_ROPE
NA_W = NA_HEADS * NA_HEAD_DIM
ROW_QK = C_END
ROW_V = ROW_QK + 2 * NA_W
ROW_G = ROW_V + NA_W
PROJ_TM = 1024
PROJ_TN = 1024


def _wt_spec(rows, k, l, row_map):
    return pl.BlockSpec((pl.Element(1), pl.Element(rows), pl.Element(k)),
                        lambda *ids: (l, row_map(*ids), 0))


def _mm_nt_kernel(a_ref, wt_ref, o_ref, wb_sc):
    @pl.when(pl.program_id(1) == 0)
    def _cast():
        wb_sc[...] = wt_ref[0].astype(BF16)

    o_ref[...] = lax.dot_general(a_ref[...], wb_sc[...], NT_DIMS,
                                 preferred_element_type=F32).astype(o_ref.dtype)


def _mm_nt(a, wt, l, row0, n, out_dtype, tm, tn, name):
    m, k = a.shape
    return pl.pallas_call(
        _mm_nt_kernel,
        grid=(n // tn, m // tm),
        in_specs=[pl.BlockSpec((tm, k), lambda j, i: (i, 0)),
                  _wt_spec(tn, k, l, lambda j, i: pl.multiple_of(row0 + j * tn, QK_ROPE))],
        out_specs=pl.BlockSpec((tm, tn), lambda j, i: (i, j)),
        out_shape=jax.ShapeDtypeStruct((m, n), out_dtype),
        scratch_shapes=[pltpu.VMEM((tn, k), BF16)],
        compiler_params=_params(2),
        name=name,
    )(a, wt)


def _norm_proj_kernel(x_ref, g_ref, wt_ref, u_ref, c_ref, wb_sc):
    @pl.when(pl.program_id(0) == 0)
    def _cast():
        wb_sc[...] = wt_ref[0].astype(BF16)

    u = _rms(x_ref[...], g_ref[...]).astype(BF16)
    u_ref[...] = u
    c_ref[...] = lax.dot_general(u, wb_sc[...], NT_DIMS, preferred_element_type=F32)


def _norm_proj(x, g, wt, l, tm=512):
    s, d = x.shape
    return pl.pallas_call(
        _norm_proj_kernel,
        grid=(s // tm,),
        in_specs=[pl.BlockSpec((tm, d), lambda i: (i, 0)),
                  _layer_spec((1, d), l, lambda i: (0, 0)),
                  _wt_spec(C_PAD, d, l, lambda i: 0)],
        out_specs=[pl.BlockSpec((tm, d), lambda i: (i, 0)), pl.BlockSpec((tm, C_PAD), lambda i: (i, 0))],
        out_shape=[jax.ShapeDtypeStruct((s, d), BF16), jax.ShapeDtypeStruct((s, C_PAD), F32)],
        scratch_shapes=[pltpu.VMEM((C_PAD, d), BF16)],
        compiler_params=_params(1),
        name="norm_proj_c",
    )(x, g, wt)


def _proj_vt_kernel(u_ref, wt_ref, vt_ref, wb_sc):
    @pl.when(pl.program_id(0) == 0)
    def _cast():
        wb_sc[...] = wt_ref[0].astype(BF16)

    vt = lax.dot_general(wb_sc[...], u_ref[...], NT_DIMS, preferred_element_type=F32).astype(BF16)
    _store_vt_with_ones(vt_ref, vt, NA_HEADS)


def _proj_vt(u, wt, l, tm=1024):
    s, k = u.shape
    return pl.pallas_call(
        _proj_vt_kernel,
        grid=(s // tm,),
        in_specs=[pl.BlockSpec((tm, k), lambda i: (i, 0)),
                  _wt_spec(NA_W, k, l, lambda i: ROW_V)],
        out_specs=pl.BlockSpec((NA_HEADS * V_ROWS, tm), lambda i: (0, i)),
        out_shape=jax.ShapeDtypeStruct((NA_HEADS * V_ROWS, s), BF16),
        scratch_shapes=[pltpu.VMEM((NA_W, k), BF16)],
        compiler_params=_params(1),
        name="proj_na_vt",
    )(u, wt)


def _rope128(x, c, sl, sr):
    return x * c + pltpu.roll(x, LANES - QK_ROPE // 2, 1) * sl + pltpu.roll(x, QK_ROPE // 2, 1) * sr


def _mla_up_kernel(c_ref, gq_ref, gkv_ref, wqn_ref, wqp_ref, wkn_ref, wvt_ref,
                   cos_ref, sl_ref, sr_ref, q_ref, k_ref, vt_ref):
    scale = (QK_NOPE + QK_ROPE) ** -0.5 * LOG2E
    cq = _rms(c_ref[:, :Q_LORA], gq_ref[...]).astype(BF16)
    ckv = _rms(c_ref[:, Q_LORA:Q_LORA + KV_LORA], gkv_ref[...]).astype(BF16)
    kpe = c_ref[:, Q_LORA + KV_LORA:]
    kpe = jnp.where(lax.broadcasted_iota(jnp.int32, kpe.shape, 1) < QK_ROPE, kpe, 0.0)
    cos, sl, sr = cos_ref[...], sl_ref[...], sr_ref[...]
    kpe_rot = _rope128(kpe, cos, sl, sr).astype(BF16)
    qn = jnp.dot(cq, wqn_ref[...], preferred_element_type=F32) * scale
    qp = jnp.dot(cq, wqp_ref[...], preferred_element_type=F32)
    kn = jnp.dot(ckv, wkn_ref[...], preferred_element_type=F32)
    vt = lax.dot_general(wvt_ref[...], ckv, NT_DIMS, preferred_element_type=F32).astype(BF16)
    _store_vt_with_ones(vt_ref, vt, MLA_HEADS)
    for h in range(MLA_HEADS):
        lo = h * MLA_QK_PAD
        hs = slice(h * LANES, (h + 1) * LANES)
        q_ref[:, lo:lo + LANES] = qn[:, hs].astype(BF16)
        q_ref[:, lo + LANES:lo + 2 * LANES] = (_rope128(qp[:, hs], cos, sl, sr) * scale).astype(BF16)
        k_ref[:, lo:lo + LANES] = kn[:, hs].astype(BF16)
        k_ref[:, lo + LANES:lo + 2 * LANES] = kpe_rot


def _mla_up(c, gq, gkv, wqn, wqp, wkn, wvt, l, cos_t, sl_t, sr_t, tm=1024):
    s = c.shape[0]
    row = lambda w: pl.BlockSpec((tm, w), lambda i: (i, 0))
    full = lambda a: _layer_spec(a.shape[1:], l, lambda i: (0, 0))
    hw = MLA_HEADS * V_ROWS
    return pl.pallas_call(
        _mla_up_kernel,
        grid=(s // tm,),
        in_specs=[row(c.shape[1]), full(gq), full(gkv), full(wqn), full(wqp), full(wkn), full(wvt),
                  row(LANES), row(LANES), row(LANES)],
        out_specs=[row(MLA_HEADS * MLA_QK_PAD), row(MLA_HEADS * MLA_QK_PAD),
                   pl.BlockSpec((hw, tm), lambda i: (0, i))],
        out_shape=[jax.ShapeDtypeStruct((s, MLA_HEADS * MLA_QK_PAD), BF16),
                   jax.ShapeDtypeStruct((s, MLA_HEADS * MLA_QK_PAD), BF16),
                   jax.ShapeDtypeStruct((hw, s), BF16)],
        compiler_params=_params(1),
        name="mla_up",
    )(c, gq, gkv, wqn, wqp, wkn, wvt, cos_t, sl_t, sr_t)


def _mla_attn_kernel(*refs, tk, n_cast):
    q_ref, k_ref, vt_ref = refs[:3]
    cast_src = refs[3:3 + n_cast]
    o_ref = refs[3 + n_cast]
    cast_dst = refs[4 + n_cast:4 + 2 * n_cast]
    r_sc, mx_sc, m_sc, acc_sc = refs[4 + 2 * n_cast:]

    q = q_ref[...]
    n_chunks = k_ref.shape[0] // tk
    assert n_chunks % 2 == 0

    def scores(c):
        k = k_ref[pl.ds(pl.multiple_of(c * tk, tk), tk), :]
        return lax.dot_general(k, q, NT_DIMS, preferred_element_type=F32)

    def values(c, pt):
        vt = vt_ref[:, pl.ds(pl.multiple_of(c * tk, tk), tk)]
        return jnp.dot(vt, pt.astype(BF16), preferred_element_type=F32)

    def write_out():
        o_ref[...] = (acc_sc[:V_HEAD, :] / acc_sc[V_HEAD:V_HEAD + 1, :]).T.astype(o_ref.dtype)

    s_ref = lax.dot_general(k_ref[:MLA_REF_KEYS, :], q, NT_DIMS, preferred_element_type=F32)
    r = jnp.max(s_ref, axis=0, keepdims=True)
    r_sc[...] = r
    mx_sc[...] = r
    acc_sc[...] = jnp.zeros(acc_sc.shape, F32)

    def step(c):
        st = scores(c)
        mx_sc[...] = jnp.maximum(mx_sc[...], jnp.max(st, axis=0, keepdims=True))
        acc_sc[...] += values(c, jnp.exp2(st - r_sc[...]))

    n_trips = n_chunks // 2

    def convert(j):
        for src, dst in zip(cast_src, cast_dst):
            rows = src.shape[0] // n_trips
            if rows % BF16_SUBLANES == 0:
                sl = pl.ds(pl.multiple_of(j * rows, rows), rows)
                dst[sl, :] = src[sl, :].astype(BF16)

    for src, dst in zip(cast_src, cast_dst):
        if (src.shape[0] // n_trips) % BF16_SUBLANES != 0:
            dst[...] = src[...].astype(BF16)

    def single_pass_body(j, carry):
        convert(j)
        step(2 * j)
        step(2 * j + 1)
        return carry

    lax.fori_loop(0, n_chunks // 2, single_pass_body, 0)
    in_range = jnp.max(mx_sc[...] - r_sc[...]) <= MLA_MAX_GAP

    @pl.when(in_range)
    def _accept():
        write_out()

    @pl.when(jnp.logical_not(in_range))
    def _online_softmax():
        m_sc[...] = jnp.full(m_sc.shape, -jnp.inf, F32)
        acc_sc[...] = jnp.zeros(acc_sc.shape, F32)

        def body(c, carry):
            st = scores(c)
            m_prev = m_sc[...]
            m_new = jnp.maximum(m_prev, jnp.max(st, axis=0, keepdims=True))
            acc_sc[...] = jnp.exp2(m_prev - m_new) * acc_sc[...] + values(c, jnp.exp2(st - m_new))
            m_sc[...] = m_new
            return carry

        lax.fori_loop(0, n_chunks, body, 0)
        write_out()


def _mla_attn(q, k, vt, casts=(), l=0, tq=2048, tk=1024):
    s = q.shape[0]
    nq = s // tq
    steps = MLA_HEADS * nq
    share = lambda w: (w.shape[1] // steps, w.shape[2])
    step_row = lambda h, i: (h * nq + i, 0)
    for w in casts:
        assert w.shape[1] % (steps * BF16_SUBLANES) == 0
    return pl.pallas_call(
        functools.partial(_mla_attn_kernel, tk=tk, n_cast=len(casts)),
        grid=(MLA_HEADS, nq),
        in_specs=[pl.BlockSpec((tq, MLA_QK_PAD), lambda h, i: (i, h)),
                  pl.BlockSpec((s, MLA_QK_PAD), lambda h, i: (0, h)),
                  pl.BlockSpec((V_ROWS, s), lambda h, i: (h, 0))]
                 + [_layer_spec(share(w), l, step_row) for w in casts],
        out_specs=[pl.BlockSpec((tq, V_HEAD), lambda h, i: (i, h))]
                  + [pl.BlockSpec(share(w), step_row) for w in casts],
        out_shape=[jax.ShapeDtypeStruct((s, MLA_HEADS * V_HEAD), BF16)]
                  + [jax.ShapeDtypeStruct(w.shape[1:], BF16) for w in casts],
        scratch_shapes=[pltpu.VMEM((1, tq), F32), pltpu.VMEM((1, tq), F32), pltpu.VMEM((1, tq), F32),
                        pltpu.VMEM((V_ROWS, tq), F32)],
        compiler_params=_params(2),
        name="mla_attn",
    )(q, k, vt, *casts)


def _na_row_start(r):
    return min(max(r - NA_KH // 2, 0), ROWS - NA_KH)


def _na_win_start(blk):
    return min(max(blk * NA_Q_ROWS - NA_KH // 2, 0), ROWS - NA_WIN_ROWS)


def _na_row_window(blk, r_in_blk):
    r = blk * NA_Q_ROWS + r_in_blk
    return _na_row_start(r) - _na_win_start(blk), _na_row_start(r) - r + (NA_KH - 1)


def _na_rpb_rows(rpb):
    n_dy, n_dx = 2 * NA_KH - 1, 2 * NA_KW - 1
    return jnp.pad(rpb.astype(F32)[..., ::-1], ((0, 0), (0, 0), (0, 2 * NA_KH - n_dy), (0, LANES - n_dx)))


def _na_attn_kernel(q_ref, k_ref, vt_ref, r_ref, o_ref, toe_sc, bias_sc, sa_sc, sb_sc, ma_sc, mb_sc):
    last = NA_BLOCKS - 1

    kc = lax.broadcasted_iota(jnp.int32, (GRID_W, LANES), 0)
    lane = lax.broadcasted_iota(jnp.int32, (GRID_W, LANES), 1)
    for half in range(2):
        qc = lane - half * GRID_W
        col_start = jnp.clip(qc - NA_KW // 2, 0, GRID_W - NA_KW)
        ok = (qc >= 0) & (qc < GRID_W) & (kc >= col_start) & (kc < col_start + NA_KW)
        shift = (LANES - (NA_KW - 1) + half * GRID_W) % LANES
        for dy in range(2 * NA_KH - 1):
            row = jnp.broadcast_to(r_ref[0, dy:dy + 1, :], (GRID_W, LANES))
            toe = pltpu.roll(row, shift, 1, stride=1, stride_axis=0)
            toe_sc[half, dy] = jnp.where(ok, toe * LOG2E, -jnp.inf)

    bias_sc[...] = jnp.full(bias_sc.shape, -jnp.inf, F32)
    for var, blk in enumerate((0, 1, last)):
        for pair in range(NA_Q_ROWS // 2):
            (off_e, dy_e), (off_o, dy_o) = _na_row_window(blk, 2 * pair), _na_row_window(blk, 2 * pair + 1)
            for b in range(min(off_e, off_o), max(off_e, off_o) + NA_KH):
                parts = []
                if 0 <= b - off_e < NA_KH:
                    parts.append(toe_sc[0, dy_e + b - off_e])
                if 0 <= b - off_o < NA_KH:
                    parts.append(toe_sc[1, dy_o + b - off_o])
                blkv = parts[0] if len(parts) == 1 else jnp.maximum(parts[0], parts[1])
                bias_sc[var, b * GRID_W:(b + 1) * GRID_W, pair * LANES:(pair + 1) * LANES] = blkv

    def win_start(blk):
        start = jnp.clip(blk * NA_TQ - (NA_KH // 2) * GRID_W, 0, SEQ - NA_TK)
        return pl.multiple_of(start, (NA_KH // 2) * GRID_W)

    def scores(blk, var, s_sc, cm_sc):
        q = q_ref[pl.ds(pl.multiple_of(blk * NA_TQ, NA_TQ), NA_TQ), :]
        k = k_ref[pl.ds(win_start(blk), NA_TK), :]
        st = lax.dot_general(k, q, NT_DIMS, preferred_element_type=F32)
        st = st * (NA_HEAD_DIM ** -0.5 * LOG2E) + bias_sc[var]
        s_sc[...] = st
        cm_sc[...] = jnp.max(st, axis=0, keepdims=True)

    def finish(blk, s_sc, cm_sc):
        vt = vt_ref[:, pl.ds(win_start(blk), NA_TK)]
        pt = jnp.exp2(s_sc[...] - cm_sc[...])
        acc = jnp.dot(vt, pt.astype(BF16), preferred_element_type=F32)
        o = (acc[:V_HEAD, :] / acc[V_HEAD:V_HEAD + 1, :]).T
        o_ref[pl.ds(pl.multiple_of(blk * NA_TQ, NA_TQ), NA_TQ), :] = o.astype(o_ref.dtype)

    assert NA_BLOCKS % 2 == 0
    scores(0, 0, sa_sc, ma_sc)

    def body(j, carry):
        b = 2 * j
        scores(b + 1, 1, sb_sc, mb_sc)
        finish(b, sa_sc, ma_sc)
        scores(b + 2, 1, sa_sc, ma_sc)
        finish(b + 1, sb_sc, mb_sc)
        return carry

    lax.fori_loop(0, NA_BLOCKS // 2 - 1, body, 0)
    scores(last, 2, sb_sc, mb_sc)
    finish(last - 1, sa_sc, ma_sc)
    finish(last, sb_sc, mb_sc)


def _na_attn(qk, vt, rpb_rows, l):
    s = qk.shape[0]
    return pl.pallas_call(
        _na_attn_kernel,
        grid=(NA_HEADS,),
        in_specs=[pl.BlockSpec((s, NA_HEAD_DIM), lambda h: (0, h)),
                  pl.BlockSpec((s, NA_HEAD_DIM), lambda h: (0, NA_HEADS + h)),
                  pl.BlockSpec((V_ROWS, s), lambda h: (h, 0)),
                  _layer_spec((1, 2 * NA_KH, LANES), l, lambda h: (h, 0, 0))],
        out_specs=pl.BlockSpec((s, NA_HEAD_DIM), lambda h: (0, h)),
        out_shape=jax.ShapeDtypeStruct((s, NA_HEADS * NA_HEAD_DIM), BF16),
        scratch_shapes=[pltpu.VMEM((2, 2 * NA_KH - 1, GRID_W, LANES), F32), pltpu.VMEM((3, NA_TK, NA_TQ), F32),
                        pltpu.VMEM((NA_TK, NA_TQ), F32), pltpu.VMEM((NA_TK, NA_TQ), F32),
                        pltpu.VMEM((1, NA_TQ), F32), pltpu.VMEM((1, NA_TQ), F32)],
        compiler_params=_params(1),
        name="na_attn",
    )(qk, qk, vt, rpb_rows)


def _merge_kernel(ya_ref, yb_ref, g_ref, x_ref, woa_ref, wob_ref, wo_ref, o_ref):
    a = jnp.dot(ya_ref[...], woa_ref[...], preferred_element_type=F32)
    b = jnp.dot(yb_ref[...], wob_ref[...], preferred_element_type=F32)
    ga = jax.nn.sigmoid(g_ref[:, :D_MODEL])
    gb = jax.nn.sigmoid(g_ref[:, D_MODEL:])
    merged = (ga * a + gb * b).astype(BF16)
    o_ref[...] = x_ref[...] + jnp.dot(merged, wo_ref[...], preferred_element_type=F32)


def _merge(ya, yb, gates, x, woa, wob, wo, tm=256):
    s = ya.shape[0]
    row = lambda w: pl.BlockSpec((tm, w), lambda i: (i, 0))
    full = lambda a: pl.BlockSpec(a.shape, lambda i: (0, 0))
    return pl.pallas_call(
        _merge_kernel,
        grid=(s // tm,),
        in_specs=[row(ya.shape[1]), row(yb.shape[1]), row(gates.shape[1]), row(D_MODEL),
                  full(woa), full(wob), full(wo)],
        out_specs=row(D_MODEL),
        out_shape=jax.ShapeDtypeStruct((s, D_MODEL), F32),
        compiler_params=_params(1),
        name="merge_out",
    )(ya, yb, gates, x, woa, wob, wo)


def _ffn_kernel(x_ref, g_ref, w1_ref, w2_ref, gf_ref, o_ref, u_sc, *, final_norm):
    f = pl.program_id(1)

    @pl.when(f == 0)
    def _init():
        x = x_ref[...]
        u_sc[...] = _rms(x, g_ref[...]).astype(BF16)
        o_ref[...] = x

    h = jnp.dot(u_sc[...], w1_ref[...], preferred_element_type=F32)
    a = jnp.square(jnp.maximum(h, 0.0)).astype(BF16)
    o_ref[...] += jnp.dot(a, w2_ref[...], preferred_element_type=F32)

    if final_norm:
        @pl.when(f == pl.num_programs(1) - 1)
        def _final():
            o_ref[...] = _rms(o_ref[...], gf_ref[...])


def _ffn(x, g, w1, w2, l, g_final, final_norm, tm=512, tf=1024):
    s, d = x.shape
    return pl.pallas_call(
        functools.partial(_ffn_kernel, final_norm=final_norm),
        grid=(s // tm, w1.shape[1] // tf),
        in_specs=[pl.BlockSpec((tm, d), lambda i, f: (i, 0)),
                  _layer_spec((1, d), l, lambda i, f: (0, 0)),
                  pl.BlockSpec((d, tf), lambda i, f: (0, f)),
                  pl.BlockSpec((tf, d), lambda i, f: (f, 0)),
                  pl.BlockSpec((1, d), lambda i, f: (0, 0))],
        out_specs=pl.BlockSpec((tm, d), lambda i, f: (i, 0)),
        out_shape=jax.ShapeDtypeStruct((s, d), F32),
        scratch_shapes=[pltpu.VMEM((tm, d), BF16)],
        compiler_params=_params(2),
        name="ffn",
    )(x, g, w1, w2, g_final)


def _rope_tables(s):
    pos = jnp.arange(s, dtype=F32)
    inv_freq = 1.0 / (ROPE_THETA ** (jnp.arange(0, QK_ROPE, 2, dtype=F32) / QK_ROPE))
    ang = pos[:, None] * inv_freq[None, :]
    cos, sin = jnp.cos(ang), jnp.sin(ang)
    half = QK_ROPE // 2
    z = lambda w: jnp.zeros((s, w), F32)
    cos_t = jnp.concatenate([cos, cos, z(LANES - QK_ROPE)], axis=1)
    sl_t = jnp.concatenate([-sin, z(LANES - half)], axis=1)
    sr_t = jnp.concatenate([z(half), sin, z(LANES - QK_ROPE)], axis=1)
    return cos_t, sl_t, sr_t


def kernel(x, norm_mix, w_in, norm_qa, w_uq, norm_kva, w_ukv, rpb, w_o_mla, w_o_na, w_out,
           norm_mlp, w_ff1, w_ff2, norm_final):
    b, s, d = x.shape
    assert (b, s, d) == (1, SEQ, D_MODEL)
    depth = w_in.shape[0]
    cos_t, sl_t, sr_t = _rope_tables(s)

    w_in_t = w_in.transpose(0, 2, 1)
    uq = w_uq.reshape(depth, Q_LORA, MLA_HEADS, QK_NOPE + QK_ROPE)
    w_qn = uq[..., :QK_NOPE].reshape(depth, Q_LORA, MLA_HEADS * QK_NOPE).astype(BF16)
    w_qp = jnp.pad(uq[..., QK_NOPE:], ((0, 0), (0, 0), (0, 0), (0, LANES - QK_ROPE)))
    w_qp = w_qp.reshape(depth, Q_LORA, MLA_HEADS * LANES).astype(BF16)
    ukv = w_ukv.reshape(depth, KV_LORA, MLA_HEADS, QK_NOPE + V_HEAD)
    w_kn = ukv[..., :QK_NOPE].reshape(depth, KV_LORA, MLA_HEADS * QK_NOPE).astype(BF16)
    w_vt = ukv[..., QK_NOPE:].reshape(depth, KV_LORA, MLA_HEADS * V_HEAD).transpose(0, 2, 1).astype(BF16)
    g_qa = norm_qa.reshape(depth, 1, Q_LORA)
    g_kva = norm_kva.reshape(depth, 1, KV_LORA)
    g_mlp = norm_mlp.reshape(depth, 1, d)
    g_mix = norm_mix.reshape(depth, 1, d)
    g_final = norm_final.reshape(1, d)
    rpb_rows = _na_rpb_rows(rpb)

    xs = x.reshape(s, d)
    for l in range(depth):
        u, c = _norm_proj(xs, g_mix, w_in_t, l)
        qk_na = _mm_nt(u, w_in_t, l, ROW_QK, 2 * NA_W, BF16, PROJ_TM, PROJ_TN, name="proj_na_qk")
        vt_na = _proj_vt(u, w_in_t, l)
        gates = _mm_nt(u, w_in_t, l, ROW_G, 2 * D_MODEL, F32, PROJ_TM, PROJ_TN, name="proj_gate")
        q, k, vt = _mla_up(c, g_qa, g_kva, w_qn, w_qp, w_kn, w_vt, l, cos_t, sl_t, sr_t)
        y_a, w_oa, w_ob, w_o, w_1, w_2 = _mla_attn(q, k, vt, (w_o_mla, w_o_na, w_out, w_ff1, w_ff2), l)
        y_b = _na_attn(qk_na, vt_na, rpb_rows, l)
        xs = _merge(y_a, y_b, gates, xs, w_oa, w_ob, w_o)
        xs = _ffn(xs, g_mlp, w_1, w_2, l, g_final, final_norm=(l == depth - 1))
    return xs.reshape(b, s, d)
```

```python
import functools

import jax
import jax.numpy as jnp
import numpy as np
from jax import lax
from jax.experimental import pallas as pl
from jax.experimental.pallas import tpu as pltpu

D_MODEL = 2048
SEQ = 8192
GRID_W = 64
ROWS = SEQ // GRID_W
MLA_HEADS = 8
Q_LORA = 512
KV_LORA = 512
QK_NOPE = 128
QK_ROPE = 64
V_HEAD = 128
ROPE_THETA = 10000.0
NA_HEADS = 8
NA_HEAD_DIM = 128
NA_KH = 8
NA_KW = 16
D_FF = 4 * D_MODEL
EPS = 1e-6
LOG2E = float(np.log2(np.e))

LANES = 128
MLA_QK_PAD = 2 * LANES
BF16_SUBLANES = 16
V_ROWS = V_HEAD + BF16_SUBLANES
MLA_REF_KEYS = 128
MLA_MAX_GAP = 60.0
NA_Q_ROWS = 8
NA_WIN_ROWS = 2 * NA_KH
NA_TQ = NA_Q_ROWS * GRID_W
NA_TK = NA_WIN_ROWS * GRID_W
NA_BLOCKS = SEQ // NA_TQ
NA_STRIP = NA_KH * GRID_W
VMEM_LIMIT = 56 * 1024 * 1024

BF16 = jnp.bfloat16
F32 = jnp.float32
NT_DIMS = (((1,), (1,)), ((), ()))


def _params(n_axes):
    return pltpu.CompilerParams(
        dimension_semantics=("arbitrary",) * n_axes, vmem_limit_bytes=VMEM_LIMIT)


def _rms(x, g):
    return x * lax.rsqrt(jnp.mean(x * x, axis=-1, keepdims=True) + EPS) * g


def _layer_spec(shape, l, index_map):
    return pl.BlockSpec((None,) + tuple(shape), lambda *ids: (l,) + tuple(index_map(*ids)))


def _store_vt_with_ones(vt_ref, vt, heads):
    for h in range(heads):
        vt_ref[h * V_ROWS:h * V_ROWS + V_HEAD, :] = vt[h * V_HEAD:(h + 1) * V_HEAD, :]
        vt_ref[h * V_ROWS + V_HEAD:(h + 1) * V_ROWS, :] = jnp.ones((V_ROWS - V_HEAD, vt.shape[1]), BF16)


C_END = Q_LORA + KV_LORA + QK_ROPE
C_PAD = C_END + LANES - QK_ROPE
NA_W = NA_HEADS * NA_HEAD_DIM
ROW_QK = C_END
ROW_V = ROW_QK + 2 * NA_W
ROW_G = ROW_V + NA_W
PROJ_TM = 1024
PROJ_TN = 1024


def _wt_spec(rows, k, l, row_map):
    return pl.BlockSpec((pl.Element(1), pl.Element(rows), pl.Element(k)),
                        lambda *ids: (l, row_map(*ids), 0))


def _mm_nt_kernel(a_ref, wt_ref, o_ref, wb_sc):
    @pl.when(pl.program_id(1) == 0)
    def _cast():
        wb_sc[...] = wt_ref[0].astype(BF16)

    o_ref[...] = lax.dot_general(a_ref[...], wb_sc[...], NT_DIMS,
                                 preferred_element_type=F32).astype(o_ref.dtype)


def _mm_nt(a, wt, l, row0, n, out_dtype, tm, tn, name):
    m, k = a.shape
    return pl.pallas_call(
        _mm_nt_kernel,
        grid=(n // tn, m // tm),
        in_specs=[pl.BlockSpec((tm, k), lambda j, i: (i, 0)),
                  _wt_spec(tn, k, l, lambda j, i: pl.multiple_of(row0 + j * tn, QK_ROPE))],
        out_specs=pl.BlockSpec((tm, tn), lambda j, i: (i, j)),
        out_shape=jax.ShapeDtypeStruct((m, n), out_dtype),
        scratch_shapes=[pltpu.VMEM((tn, k), BF16)],
        compiler_params=_params(2),
        name=name,
    )(a, wt)


def _norm_proj_kernel(x_ref, g_ref, wt_ref, u_ref, c_ref, wb_sc):
    @pl.when(pl.program_id(0) == 0)
    def _cast():
        wb_sc[...] = wt_ref[0].astype(BF16)

    u = _rms(x_ref[...], g_ref[...]).astype(BF16)
    u_ref[...] = u
    c_ref[...] = lax.dot_general(u, wb_sc[...], NT_DIMS, preferred_element_type=F32)


def _norm_proj(x, g, wt, l, tm=512):
    s, d = x.shape
    return pl.pallas_call(
        _norm_proj_kernel,
        grid=(s // tm,),
        in_specs=[pl.BlockSpec((tm, d), lambda i: (i, 0)),
                  _layer_spec((1, d), l, lambda i: (0, 0)),
                  _wt_spec(C_PAD, d, l, lambda i: 0)],
        out_specs=[pl.BlockSpec((tm, d), lambda i: (i, 0)), pl.BlockSpec((tm, C_PAD), lambda i: (i, 0))],
        out_shape=[jax.ShapeDtypeStruct((s, d), BF16), jax.ShapeDtypeStruct((s, C_PAD), F32)],
        scratch_shapes=[pltpu.VMEM((C_PAD, d), BF16)],
        compiler_params=_params(1),
        name="norm_proj_c",
    )(x, g, wt)


def _proj_vt_kernel(u_ref, wt_ref, vt_ref, wb_sc):
    @pl.when(pl.program_id(0) == 0)
    def _cast():
        wb_sc[...] = wt_ref[0].astype(BF16)

    vt = lax.dot_general(wb_sc[...], u_ref[...], NT_DIMS, preferred_element_type=F32).astype(BF16)
    _store_vt_with_ones(vt_ref, vt, NA_HEADS)


def _proj_vt(u, wt, l, tm=1024):
    s, k = u.shape
    return pl.pallas_call(
        _proj_vt_kernel,
        grid=(s // tm,),
        in_specs=[pl.BlockSpec((tm, k), lambda i: (i, 0)),
                  _wt_spec(NA_W, k, l, lambda i: ROW_V)],
        out_specs=pl.BlockSpec((NA_HEADS * V_ROWS, tm), lambda i: (0, i)),
        out_shape=jax.ShapeDtypeStruct((NA_HEADS * V_ROWS, s), BF16),
        scratch_shapes=[pltpu.VMEM((NA_W, k), BF16)],
        compiler_params=_params(1),
        name="proj_na_vt",
    )(u, wt)


def _rope128(x, c, sl, sr):
    return x * c + pltpu.roll(x, LANES - QK_ROPE // 2, 1) * sl + pltpu.roll(x, QK_ROPE // 2, 1) * sr


def _mla_up_kernel(c_ref, gq_ref, gkv_ref, wqn_ref, wqp_ref, wkn_ref, wvt_ref,
                   cos_ref, sl_ref, sr_ref, q_ref, k_ref, vt_ref):
    scale = (QK_NOPE + QK_ROPE) ** -0.5 * LOG2E
    cq = _rms(c_ref[:, :Q_LORA], gq_ref[...]).astype(BF16)
    ckv = _rms(c_ref[:, Q_LORA:Q_LORA + KV_LORA], gkv_ref[...]).astype(BF16)
    kpe = c_ref[:, Q_LORA + KV_LORA:]
    kpe = jnp.where(lax.broadcasted_iota(jnp.int32, kpe.shape, 1) < QK_ROPE, kpe, 0.0)
    cos, sl, sr = cos_ref[...], sl_ref[...], sr_ref[...]
    kpe_rot = _rope128(kpe, cos, sl, sr).astype(BF16)
    qn = jnp.dot(cq, wqn_ref[...], preferred_element_type=F32) * scale
    qp = jnp.dot(cq, wqp_ref[...], preferred_element_type=F32)
    kn = jnp.dot(ckv, wkn_ref[...], preferred_element_type=F32)
    vt = lax.dot_general(wvt_ref[...], ckv, NT_DIMS, preferred_element_type=F32).astype(BF16)
    _store_vt_with_ones(vt_ref, vt, MLA_HEADS)
    for h in range(MLA_HEADS):
        lo = h * MLA_QK_PAD
        hs = slice(h * LANES, (h + 1) * LANES)
        q_ref[:, lo:lo + LANES] = qn[:, hs].astype(BF16)
        q_ref[:, lo + LANES:lo + 2 * LANES] = (_rope128(qp[:, hs], cos, sl, sr) * scale).astype(BF16)
        k_ref[:, lo:lo + LANES] = kn[:, hs].astype(BF16)
        k_ref[:, lo + LANES:lo + 2 * LANES] = kpe_rot


def _mla_up(c, gq, gkv, wqn, wqp, wkn, wvt, l, cos_t, sl_t, sr_t, tm=1024):
    s = c.shape[0]
    row = lambda w: pl.BlockSpec((tm, w), lambda i: (i, 0))
    full = lambda a: _layer_spec(a.shape[1:], l, lambda i: (0, 0))
    hw = MLA_HEADS * V_ROWS
    return pl.pallas_call(
        _mla_up_kernel,
        grid=(s // tm,),
        in_specs=[row(c.shape[1]), full(gq), full(gkv), full(wqn), full(wqp), full(wkn), full(wvt),
                  row(LANES), row(LANES), row(LANES)],
        out_specs=[row(MLA_HEADS * MLA_QK_PAD), row(MLA_HEADS * MLA_QK_PAD),
                   pl.BlockSpec((hw, tm), lambda i: (0, i))],
        out_shape=[jax.ShapeDtypeStruct((s, MLA_HEADS * MLA_QK_PAD), BF16),
                   jax.ShapeDtypeStruct((s, MLA_HEADS * MLA_QK_PAD), BF16),
                   jax.ShapeDtypeStruct((hw, s), BF16)],
        compiler_params=_params(1),
        name="mla_up",
    )(c, gq, gkv, wqn, wqp, wkn, wvt, cos_t, sl_t, sr_t)


def _mla_attn_kernel(*refs, tk, n_cast):
    q_ref, k_ref, vt_ref = refs[:3]
    cast_src = refs[3:3 + n_cast]
    o_ref = refs[3 + n_cast]
    cast_dst = refs[4 + n_cast:4 + 2 * n_cast]
    r_sc, mx_sc, m_sc, acc_sc = refs[4 + 2 * n_cast:]

    for src, dst in zip(cast_src, cast_dst):
        dst[...] = src[...].astype(BF16)

    q = q_ref[...]
    n_chunks = k_ref.shape[0] // tk
    assert n_chunks % 2 == 0

    def scores(c):
        k = k_ref[pl.ds(pl.multiple_of(c * tk, tk), tk), :]
        return lax.dot_general(k, q, NT_DIMS, preferred_element_type=F32)

    def values(c, pt):
        vt = vt_ref[:, pl.ds(pl.multiple_of(c * tk, tk), tk)]
        return jnp.dot(vt, pt.astype(BF16), preferred_element_type=F32)

    def write_out():
        o_ref[...] = (acc_sc[:V_HEAD, :] / acc_sc[V_HEAD:V_HEAD + 1, :]).T.astype(o_ref.dtype)

    s_ref = lax.dot_general(k_ref[:MLA_REF_KEYS, :], q, NT_DIMS, preferred_element_type=F32)
    r = jnp.max(s_ref, axis=0, keepdims=True)
    r_sc[...] = r
    mx_sc[...] = r
    acc_sc[...] = jnp.zeros(acc_sc.shape, F32)

    def step(c):
        st = scores(c)
        mx_sc[...] = jnp.maximum(mx_sc[...], jnp.max(st, axis=0, keepdims=True))
        acc_sc[...] += values(c, jnp.exp2(st - r_sc[...]))

    def single_pass_body(j, carry):
        step(2 * j)
        step(2 * j + 1)
        return carry

    lax.fori_loop(0, n_chunks // 2, single_pass_body, 0)
    in_range = jnp.max(mx_sc[...] - r_sc[...]) <= MLA_MAX_GAP

    @pl.when(in_range)
    def _accept():
        write_out()

    @pl.when(jnp.logical_not(in_range))
    def _online_softmax():
        m_sc[...] = jnp.full(m_sc.shape, -jnp.inf, F32)
        acc_sc[...] = jnp.zeros(acc_sc.shape, F32)

        def body(c, carry):
            st = scores(c)
            m_prev = m_sc[...]
            m_new = jnp.maximum(m_prev, jnp.max(st, axis=0, keepdims=True))
            acc_sc[...] = jnp.exp2(m_prev - m_new) * acc_sc[...] + values(c, jnp.exp2(st - m_new))
            m_sc[...] = m_new
            return carry

        lax.fori_loop(0, n_chunks, body, 0)
        write_out()


def _mla_attn(q, k, vt, casts=(), l=0, tq=2048, tk=1024):
    s = q.shape[0]
    nq = s // tq
    steps = MLA_HEADS * nq
    share = lambda w: (w.shape[1] // steps, w.shape[2])
    step_row = lambda h, i: (h * nq + i, 0)
    for w in casts:
        assert w.shape[1] % (steps * BF16_SUBLANES) == 0
    return pl.pallas_call(
        functools.partial(_mla_attn_kernel, tk=tk, n_cast=len(casts)),
        grid=(MLA_HEADS, nq),
        in_specs=[pl.BlockSpec((tq, MLA_QK_PAD), lambda h, i: (i, h)),
                  pl.BlockSpec((s, MLA_QK_PAD), lambda h, i: (0, h)),
                  pl.BlockSpec((V_ROWS, s), lambda h, i: (h, 0))]
                 + [_layer_spec(share(w), l, step_row) for w in casts],
        out_specs=[pl.BlockSpec((tq, V_HEAD), lambda h, i: (i, h))]
                  + [pl.BlockSpec(share(w), step_row) for w in casts],
        out_shape=[jax.ShapeDtypeStruct((s, MLA_HEADS * V_HEAD), BF16)]
                  + [jax.ShapeDtypeStruct(w.shape[1:], BF16) for w in casts],
        scratch_shapes=[pltpu.VMEM((1, tq), F32), pltpu.VMEM((1, tq), F32), pltpu.VMEM((1, tq), F32),
                        pltpu.VMEM((V_ROWS, tq), F32)],
        compiler_params=_params(2),
        name="mla_attn",
    )(q, k, vt, *casts)


def _na_row_start(r):
    return min(max(r - NA_KH // 2, 0), ROWS - NA_KH)


def _na_win_start(blk):
    return min(max(blk * NA_Q_ROWS - NA_KH // 2, 0), ROWS - NA_WIN_ROWS)


def _na_row_window(blk, r_in_blk):
    r = blk * NA_Q_ROWS + r_in_blk
    return _na_row_start(r) - _na_win_start(blk), _na_row_start(r) - r + (NA_KH - 1)


def _na_rpb_rows(rpb):
    n_dy, n_dx = 2 * NA_KH - 1, 2 * NA_KW - 1
    return jnp.pad(rpb.astype(F32)[..., ::-1], ((0, 0), (0, 0), (0, 2 * NA_KH - n_dy), (0, LANES - n_dx)))


def _na_attn_kernel(q_ref, k_ref, vt_ref, r_ref, o_ref, toe_sc, bias_sc, sa_sc, sb_sc, ma_sc, mb_sc):
    last = NA_BLOCKS - 1

    kc = lax.broadcasted_iota(jnp.int32, (GRID_W, LANES), 0)
    lane = lax.broadcasted_iota(jnp.int32, (GRID_W, LANES), 1)
    for half in range(2):
        qc = lane - half * GRID_W
        col_start = jnp.clip(qc - NA_KW // 2, 0, GRID_W - NA_KW)
        ok = (qc >= 0) & (qc < GRID_W) & (kc >= col_start) & (kc < col_start + NA_KW)
        shift = (LANES - (NA_KW - 1) + half * GRID_W) % LANES
        for dy in range(2 * NA_KH - 1):
            row = jnp.broadcast_to(r_ref[0, dy:dy + 1, :], (GRID_W, LANES))
            toe = pltpu.roll(row, shift, 1, stride=1, stride_axis=0)
            toe_sc[half, dy] = jnp.where(ok, toe * LOG2E, -jnp.inf)

    bias_sc[...] = jnp.full(bias_sc.shape, -jnp.inf, F32)
    for var, blk in enumerate((0, 1, last)):
        for pair in range(NA_Q_ROWS // 2):
            (off_e, dy_e), (off_o, dy_o) = _na_row_window(blk, 2 * pair), _na_row_window(blk, 2 * pair + 1)
            for b in range(min(off_e, off_o), max(off_e, off_o) + NA_KH):
                parts = []
                if 0 <= b - off_e < NA_KH:
                    parts.append(toe_sc[0, dy_e + b - off_e])
                if 0 <= b - off_o < NA_KH:
                    parts.append(toe_sc[1, dy_o + b - off_o])
                blkv = parts[0] if len(parts) == 1 else jnp.maximum(parts[0], parts[1])
                bias_sc[var, b * GRID_W:(b + 1) * GRID_W, pair * LANES:(pair + 1) * LANES] = blkv

    def win_start(blk):
        start = jnp.clip(blk * NA_TQ - (NA_KH // 2) * GRID_W, 0, SEQ - NA_TK)
        return pl.multiple_of(start, (NA_KH // 2) * GRID_W)

    def scores(blk, var, s_sc, cm_sc):
        q = q_ref[pl.ds(pl.multiple_of(blk * NA_TQ, NA_TQ), NA_TQ), :]
        k = k_ref[pl.ds(win_start(blk), NA_TK), :]
        st = lax.dot_general(k, q, NT_DIMS, preferred_element_type=F32)
        st = st * (NA_HEAD_DIM ** -0.5 * LOG2E) + bias_sc[var]
        s_sc[...] = st
        cm_sc[...] = jnp.max(st, axis=0, keepdims=True)

    def finish(blk, s_sc, cm_sc):
        vt = vt_ref[:, pl.ds(win_start(blk), NA_TK)]
        pt = jnp.exp2(s_sc[...] - cm_sc[...])
        acc = jnp.dot(vt, pt.astype(BF16), preferred_element_type=F32)
        o = (acc[:V_HEAD, :] / acc[V_HEAD:V_HEAD + 1, :]).T
        o_ref[pl.ds(pl.multiple_of(blk * NA_TQ, NA_TQ), NA_TQ), :] = o.astype(o_ref.dtype)

    assert NA_BLOCKS % 2 == 0
    scores(0, 0, sa_sc, ma_sc)

    def body(j, carry):
        b = 2 * j
        scores(b + 1, 1, sb_sc, mb_sc)
        finish(b, sa_sc, ma_sc)
        scores(b + 2, 1, sa_sc, ma_sc)
        finish(b + 1, sb_sc, mb_sc)
        return carry

    lax.fori_loop(0, NA_BLOCKS // 2 - 1, body, 0)
    scores(last, 2, sb_sc, mb_sc)
    finish(last - 1, sa_sc, ma_sc)
    finish(last, sb_sc, mb_sc)


def _na_attn(qk, vt, rpb_rows, l):
    s = qk.shape[0]
    return pl.pallas_call(
        _na_attn_kernel,
        grid=(NA_HEADS,),
        in_specs=[pl.BlockSpec((s, NA_HEAD_DIM), lambda h: (0, h)),
                  pl.BlockSpec((s, NA_HEAD_DIM), lambda h: (0, NA_HEADS + h)),
                  pl.BlockSpec((V_ROWS, s), lambda h: (h, 0)),
                  _layer_spec((1, 2 * NA_KH, LANES), l, lambda h: (h, 0, 0))],
        out_specs=pl.BlockSpec((s, NA_HEAD_DIM), lambda h: (0, h)),
        out_shape=jax.ShapeDtypeStruct((s, NA_HEADS * NA_HEAD_DIM), BF16),
        scratch_shapes=[pltpu.VMEM((2, 2 * NA_KH - 1, GRID_W, LANES), F32), pltpu.VMEM((3, NA_TK, NA_TQ), F32),
                        pltpu.VMEM((NA_TK, NA_TQ), F32), pltpu.VMEM((NA_TK, NA_TQ), F32),
                        pltpu.VMEM((1, NA_TQ), F32), pltpu.VMEM((1, NA_TQ), F32)],
        compiler_params=_params(1),
        name="na_attn",
    )(qk, qk, vt, rpb_rows)


def _merge_kernel(ya_ref, yb_ref, g_ref, x_ref, woa_ref, wob_ref, wo_ref, o_ref):
    a = jnp.dot(ya_ref[...], woa_ref[...], preferred_element_type=F32)
    b = jnp.dot(yb_ref[...], wob_ref[...], preferred_element_type=F32)
    ga = jax.nn.sigmoid(g_ref[:, :D_MODEL])
    gb = jax.nn.sigmoid(g_ref[:, D_MODEL:])
    merged = (ga * a + gb * b).astype(BF16)
    o_ref[...] = x_ref[...] + jnp.dot(merged, wo_ref[...], preferred_element_type=F32)


def _merge(ya, yb, gates, x, woa, wob, wo, tm=256):
    s = ya.shape[0]
    row = lambda w: pl.BlockSpec((tm, w), lambda i: (i, 0))
    full = lambda a: pl.BlockSpec(a.shape, lambda i: (0, 0))
    return pl.pallas_call(
        _merge_kernel,
        grid=(s // tm,),
        in_specs=[row(ya.shape[1]), row(yb.shape[1]), row(gates.shape[1]), row(D_MODEL),
                  full(woa), full(wob), full(wo)],
        out_specs=row(D_MODEL),
        out_shape=jax.ShapeDtypeStruct((s, D_MODEL), F32),
        compiler_params=_params(1),
        name="merge_out",
    )(ya, yb, gates, x, woa, wob, wo)


def _ffn_kernel(x_ref, g_ref, w1_ref, w2_ref, gf_ref, o_ref, u_sc, *, final_norm):
    f = pl.program_id(1)

    @pl.when(f == 0)
    def _init():
        x = x_ref[...]
        u_sc[...] = _rms(x, g_ref[...]).astype(BF16)
        o_ref[...] = x

    h = jnp.dot(u_sc[...], w1_ref[...], preferred_element_type=F32)
    a = jnp.square(jnp.maximum(h, 0.0)).astype(BF16)
    o_ref[...] += jnp.dot(a, w2_ref[...], preferred_element_type=F32)

    if final_norm:
        @pl.when(f == pl.num_programs(1) - 1)
        def _final():
            o_ref[...] = _rms(o_ref[...], gf_ref[...])


def _ffn(x, g, w1, w2, l, g_final, final_norm, tm=512, tf=1024):
    s, d = x.shape
    return pl.pallas_call(
        functools.partial(_ffn_kernel, final_norm=final_norm),
        grid=(s // tm, w1.shape[1] // tf),
        in_specs=[pl.BlockSpec((tm, d), lambda i, f: (i, 0)),
                  _layer_spec((1, d), l, lambda i, f: (0, 0)),
                  pl.BlockSpec((d, tf), lambda i, f: (0, f)),
                  pl.BlockSpec((tf, d), lambda i, f: (f, 0)),
                  pl.BlockSpec((1, d), lambda i, f: (0, 0))],
        out_specs=pl.BlockSpec((tm, d), lambda i, f: (i, 0)),
        out_shape=jax.ShapeDtypeStruct((s, d), F32),
        scratch_shapes=[pltpu.VMEM((tm, d), BF16)],
        compiler_params=_params(2),
        name="ffn",
    )(x, g, w1, w2, g_final)


def _rope_tables(s):
    pos = jnp.arange(s, dtype=F32)
    inv_freq = 1.0 / (ROPE_THETA ** (jnp.arange(0, QK_ROPE, 2, dtype=F32) / QK_ROPE))
    ang = pos[:, None] * inv_freq[None, :]
    cos, sin = jnp.cos(ang), jnp.sin(ang)
    half = QK_ROPE // 2
    z = lambda w: jnp.zeros((s, w), F32)
    cos_t = jnp.concatenate([cos, cos, z(LANES - QK_ROPE)], axis=1)
    sl_t = jnp.concatenate([-sin, z(LANES - half)], axis=1)
    sr_t = jnp.concatenate([z(half), sin, z(LANES - QK_ROPE)], axis=1)
    return cos_t, sl_t, sr_t


def kernel(x, norm_mix, w_in, norm_qa, w_uq, norm_kva, w_ukv, rpb, w_o_mla, w_o_na, w_out,
           norm_mlp, w_ff1, w_ff2, norm_final):
    b, s, d = x.shape
    assert (b, s, d) == (1, SEQ, D_MODEL)
    depth = w_in.shape[0]
    cos_t, sl_t, sr_t = _rope_tables(s)

    w_in_t = w_in.transpose(0, 2, 1)
    uq = w_uq.reshape(depth, Q_LORA, MLA_HEADS, QK_NOPE + QK_ROPE)
    w_qn = uq[..., :QK_NOPE].reshape(depth, Q_LORA, MLA_HEADS * QK_NOPE).astype(BF16)
    w_qp = jnp.pad(uq[..., QK_NOPE:], ((0, 0), (0, 0), (0, 0), (0, LANES - QK_ROPE)))
    w_qp = w_qp.reshape(depth, Q_LORA, MLA_HEADS * LANES).astype(BF16)
    ukv = w_ukv.reshape(depth, KV_LORA, MLA_HEADS, QK_NOPE + V_HEAD)
    w_kn = ukv[..., :QK_NOPE].reshape(depth, KV_LORA, MLA_HEADS * QK_NOPE).astype(BF16)
    w_vt = ukv[..., QK_NOPE:].reshape(depth, KV_LORA, MLA_HEADS * V_HEAD).transpose(0, 2, 1).astype(BF16)
    g_qa = norm_qa.reshape(depth, 1, Q_LORA)
    g_kva = norm_kva.reshape(depth, 1, KV_LORA)
    g_mlp = norm_mlp.reshape(depth, 1, d)
    g_mix = norm_mix.reshape(depth, 1, d)
    g_final = norm_final.reshape(1, d)
    rpb_rows = _na_rpb_rows(rpb)

    xs = x.reshape(s, d)
    for l in range(depth):
        u, c = _norm_proj(xs, g_mix, w_in_t, l)
        qk_na = _mm_nt(u, w_in_t, l, ROW_QK, 2 * NA_W, BF16, PROJ_TM, PROJ_TN, name="proj_na_qk")
        vt_na = _proj_vt(u, w_in_t, l)
        gates = _mm_nt(u, w_in_t, l, ROW_G, 2 * D_MODEL, F32, PROJ_TM, PROJ_TN, name="proj_gate")
        q, k, vt = _mla_up(c, g_qa, g_kva, w_qn, w_qp, w_kn, w_vt, l, cos_t, sl_t, sr_t)
        y_a, w_oa, w_ob, w_o, w_1, w_2 = _mla_attn(q, k, vt, (w_o_mla, w_o_na, w_out, w_ff1, w_ff2), l)
        y_b = _na_attn(qk_na, vt_na, rpb_rows, l)
        xs = _merge(y_a, y_b, gates, xs, w_oa, w_ob, w_o)
        xs = _ffn(xs, g_mlp, w_1, w_2, l, g_final, final_norm=(l == depth - 1))
    return xs.reshape(b, s, d)
```

```python
import functools

import jax
import jax.numpy as jnp
import numpy as np
from jax import lax
from jax.experimental import pallas as pl
from jax.experimental.pallas import tpu as pltpu

D_MODEL = 2048
SEQ = 8192
GRID_W = 64
ROWS = SEQ // GRID_W
MLA_HEADS = 8
Q_LORA = 512
KV_LORA = 512
QK_NOPE = 128
QK_ROPE = 64
V_HEAD = 128
ROPE_THETA = 10000.0
NA_HEADS = 8
NA_HEAD_DIM = 128
NA_KH = 8
NA_KW = 16
D_FF = 4 * D_MODEL
EPS = 1e-6
LOG2E = float(np.log2(np.e))

LANES = 128
MLA_QK_PAD = 2 * LANES
F32_SUBLANES = 8
BF16_SUBLANES = 16
V_ROWS = V_HEAD + BF16_SUBLANES
MLA_REF_KEYS = 128
MLA_MAX_GAP = 60.0
NA_Q_ROWS = 8
NA_WIN_ROWS = 2 * NA_KH
NA_TQ = NA_Q_ROWS * GRID_W
NA_TK = NA_WIN_ROWS * GRID_W
NA_BLOCKS = SEQ // NA_TQ
NA_STRIP = NA_KH * GRID_W
VMEM_LIMIT = 56 * 1024 * 1024

BF16 = jnp.bfloat16
F32 = jnp.float32
NT_DIMS = (((1,), (1,)), ((), ()))


def _params(n_axes):
    return pltpu.CompilerParams(
        dimension_semantics=("arbitrary",) * n_axes, vmem_limit_bytes=VMEM_LIMIT)


def _rms(x, g):
    return x * lax.rsqrt(jnp.mean(x * x, axis=-1, keepdims=True) + EPS) * g


def _layer_spec(shape, l, index_map):
    return pl.BlockSpec((None,) + tuple(shape), lambda *ids: (l,) + tuple(index_map(*ids)))


def _store_vt_with_ones(vt_ref, vt, heads):
    for h in range(heads):
        vt_ref[h * V_ROWS:h * V_ROWS + V_HEAD, :] = vt[h * V_HEAD:(h + 1) * V_HEAD, :]
        vt_ref[h * V_ROWS + V_HEAD:(h + 1) * V_ROWS, :] = jnp.ones((V_ROWS - V_HEAD, vt.shape[1]), BF16)


C_END = Q_LORA + KV_LORA + QK_ROPE
C_PAD = C_END + LANES - QK_ROPE
NA_W = NA_HEADS * NA_HEAD_DIM
ROW_QK = C_END
ROW_V = ROW_QK + 2 * NA_W
ROW_G = ROW_V + NA_W
PROJ_TM = 1024
PROJ_TN = 1024


def _wt_spec(rows, k, l, row_map):
    return pl.BlockSpec((pl.Element(1), pl.Element(rows), pl.Element(k)),
                        lambda *ids: (l, row_map(*ids), 0))


def _mm_nt_kernel(a_ref, wt_ref, o_ref, wb_sc):
    @pl.when(pl.program_id(1) == 0)
    def _cast():
        wb_sc[...] = wt_ref[0].astype(BF16)

    o_ref[...] = lax.dot_general(a_ref[...], wb_sc[...], NT_DIMS,
                                 preferred_element_type=F32).astype(o_ref.dtype)


def _mm_nt(a, wt, l, row0, n, out_dtype, tm, tn, name):
    m, k = a.shape
    return pl.pallas_call(
        _mm_nt_kernel,
        grid=(n // tn, m // tm),
        in_specs=[pl.BlockSpec((tm, k), lambda j, i: (i, 0)),
                  _wt_spec(tn, k, l, lambda j, i: pl.multiple_of(row0 + j * tn, QK_ROPE))],
        out_specs=pl.BlockSpec((tm, tn), lambda j, i: (i, j)),
        out_shape=jax.ShapeDtypeStruct((m, n), out_dtype),
        scratch_shapes=[pltpu.VMEM((tn, k), BF16)],
        compiler_params=_params(2),
        name=name,
    )(a, wt)


def _norm_proj_kernel(x_ref, g_ref, wt_ref, u_ref, c_ref, wb_sc):
    @pl.when(pl.program_id(0) == 0)
    def _cast():
        wb_sc[...] = wt_ref[0].astype(BF16)

    u = _rms(x_ref[...], g_ref[...]).astype(BF16)
    u_ref[...] = u
    c_ref[...] = lax.dot_general(u, wb_sc[...], NT_DIMS, preferred_element_type=F32)


def _norm_proj(x, g, wt, l, tm=512):
    s, d = x.shape
    return pl.pallas_call(
        _norm_proj_kernel,
        grid=(s // tm,),
        in_specs=[pl.BlockSpec((tm, d), lambda i: (i, 0)),
                  _layer_spec((1, d), l, lambda i: (0, 0)),
                  _wt_spec(C_PAD, d, l, lambda i: 0)],
        out_specs=[pl.BlockSpec((tm, d), lambda i: (i, 0)), pl.BlockSpec((tm, C_PAD), lambda i: (i, 0))],
        out_shape=[jax.ShapeDtypeStruct((s, d), BF16), jax.ShapeDtypeStruct((s, C_PAD), F32)],
        scratch_shapes=[pltpu.VMEM((C_PAD, d), BF16)],
        compiler_params=_params(1),
        name="norm_proj_c",
    )(x, g, wt)


def _proj_vt_kernel(u_ref, wt_ref, vt_ref, wb_sc):
    @pl.when(pl.program_id(0) == 0)
    def _cast():
        wb_sc[...] = wt_ref[0].astype(BF16)

    vt = lax.dot_general(wb_sc[...], u_ref[...], NT_DIMS, preferred_element_type=F32).astype(BF16)
    _store_vt_with_ones(vt_ref, vt, NA_HEADS)


def _proj_vt(u, wt, l, tm=1024):
    s, k = u.shape
    return pl.pallas_call(
        _proj_vt_kernel,
        grid=(s // tm,),
        in_specs=[pl.BlockSpec((tm, k), lambda i: (i, 0)),
                  _wt_spec(NA_W, k, l, lambda i: ROW_V)],
        out_specs=pl.BlockSpec((NA_HEADS * V_ROWS, tm), lambda i: (0, i)),
        out_shape=jax.ShapeDtypeStruct((NA_HEADS * V_ROWS, s), BF16),
        scratch_shapes=[pltpu.VMEM((NA_W, k), BF16)],
        compiler_params=_params(1),
        name="proj_na_vt",
    )(u, wt)


def _rope128(x, c, sl, sr):
    return x * c + pltpu.roll(x, LANES - QK_ROPE // 2, 1) * sl + pltpu.roll(x, QK_ROPE // 2, 1) * sr


def _mla_up_kernel(c_ref, gq_ref, gkv_ref, wqn_ref, wqp_ref, wkn_ref, wvt_ref,
                   cos_ref, sl_ref, sr_ref, q_ref, k_ref, vt_ref):
    scale = (QK_NOPE + QK_ROPE) ** -0.5 * LOG2E
    cq = _rms(c_ref[:, :Q_LORA], gq_ref[...]).astype(BF16)
    ckv = _rms(c_ref[:, Q_LORA:Q_LORA + KV_LORA], gkv_ref[...]).astype(BF16)
    kpe = c_ref[:, Q_LORA + KV_LORA:]
    kpe = jnp.where(lax.broadcasted_iota(jnp.int32, kpe.shape, 1) < QK_ROPE, kpe, 0.0)
    cos, sl, sr = cos_ref[...], sl_ref[...], sr_ref[...]
    kpe_rot = _rope128(kpe, cos, sl, sr).astype(BF16)
    qn = jnp.dot(cq, wqn_ref[...], preferred_element_type=F32) * scale
    qp = jnp.dot(cq, wqp_ref[...], preferred_element_type=F32)
    kn = jnp.dot(ckv, wkn_ref[...], preferred_element_type=F32)
    vt = lax.dot_general(wvt_ref[...], ckv, NT_DIMS, preferred_element_type=F32).astype(BF16)
    _store_vt_with_ones(vt_ref, vt, MLA_HEADS)
    for h in range(MLA_HEADS):
        lo = h * MLA_QK_PAD
        hs = slice(h * LANES, (h + 1) * LANES)
        q_ref[:, lo:lo + LANES] = qn[:, hs].astype(BF16)
        q_ref[:, lo + LANES:lo + 2 * LANES] = (_rope128(qp[:, hs], cos, sl, sr) * scale).astype(BF16)
        k_ref[:, lo:lo + LANES] = kn[:, hs].astype(BF16)
        k_ref[:, lo + LANES:lo + 2 * LANES] = kpe_rot


def _mla_up(c, gq, gkv, wqn, wqp, wkn, wvt, l, cos_t, sl_t, sr_t, tm=1024):
    s = c.shape[0]
    row = lambda w: pl.BlockSpec((tm, w), lambda i: (i, 0))
    full = lambda a: _layer_spec(a.shape[1:], l, lambda i: (0, 0))
    hw = MLA_HEADS * V_ROWS
    return pl.pallas_call(
        _mla_up_kernel,
        grid=(s // tm,),
        in_specs=[row(c.shape[1]), full(gq), full(gkv), full(wqn), full(wqp), full(wkn), full(wvt),
                  row(LANES), row(LANES), row(LANES)],
        out_specs=[row(MLA_HEADS * MLA_QK_PAD), row(MLA_HEADS * MLA_QK_PAD),
                   pl.BlockSpec((hw, tm), lambda i: (0, i))],
        out_shape=[jax.ShapeDtypeStruct((s, MLA_HEADS * MLA_QK_PAD), BF16),
                   jax.ShapeDtypeStruct((s, MLA_HEADS * MLA_QK_PAD), BF16),
                   jax.ShapeDtypeStruct((hw, s), BF16)],
        compiler_params=_params(1),
        name="mla_up",
    )(c, gq, gkv, wqn, wqp, wkn, wvt, cos_t, sl_t, sr_t)


def _mla_attn_kernel(*refs, tk, n_cast):
    q_ref, k_ref, vt_ref = refs[:3]
    cast_src = refs[3:3 + n_cast]
    o_ref = refs[3 + n_cast]
    cast_dst = refs[4 + n_cast:4 + 2 * n_cast]
    stat_sc, acc_sc = refs[4 + 2 * n_cast:]
    r_sc, mx_sc, m_sc = (stat_sc.at[pl.ds(j * F32_SUBLANES, 1), :] for j in range(3))

    for src, dst in zip(cast_src, cast_dst):
        dst[...] = src[...].astype(BF16)

    q = q_ref[...]
    n_chunks = k_ref.shape[0] // tk
    assert n_chunks % 2 == 0

    def scores(c):
        k = k_ref[pl.ds(pl.multiple_of(c * tk, tk), tk), :]
        return lax.dot_general(k, q, NT_DIMS, preferred_element_type=F32)

    def values(c, pt):
        vt = vt_ref[:, pl.ds(pl.multiple_of(c * tk, tk), tk)]
        return jnp.dot(vt, pt.astype(BF16), preferred_element_type=F32)

    def write_out():
        o_ref[...] = (acc_sc[:V_HEAD, :] / acc_sc[V_HEAD:V_HEAD + 1, :]).T.astype(o_ref.dtype)

    s_ref = lax.dot_general(k_ref[:MLA_REF_KEYS, :], q, NT_DIMS, preferred_element_type=F32)
    r = jnp.max(s_ref, axis=0, keepdims=True)
    r_sc[...] = r
    mx_sc[...] = r
    acc_sc[...] = jnp.zeros(acc_sc.shape, F32)

    def step(c):
        st = scores(c)
        mx_sc[...] = jnp.maximum(mx_sc[...], jnp.max(st, axis=0, keepdims=True))
        acc_sc[...] += values(c, jnp.exp2(st - r_sc[...]))

    def single_pass_body(j, carry):
        step(2 * j)
        step(2 * j + 1)
        return carry

    lax.fori_loop(0, n_chunks // 2, single_pass_body, 0)
    in_range = jnp.max(mx_sc[...] - r_sc[...]) <= MLA_MAX_GAP

    @pl.when(in_range)
    def _accept():
        write_out()

    @pl.when(jnp.logical_not(in_range))
    def _online_softmax():
        m_sc[...] = jnp.full(m_sc.shape, -jnp.inf, F32)
        acc_sc[...] = jnp.zeros(acc_sc.shape, F32)

        def body(c, carry):
            st = scores(c)
            m_prev = m_sc[...]
            m_new = jnp.maximum(m_prev, jnp.max(st, axis=0, keepdims=True))
            acc_sc[...] = jnp.exp2(m_prev - m_new) * acc_sc[...] + values(c, jnp.exp2(st - m_new))
            m_sc[...] = m_new
            return carry

        lax.fori_loop(0, n_chunks, body, 0)
        write_out()


def _mla_attn(q, k, vt, casts=(), l=0, tq=2048, tk=1024):
    s = q.shape[0]
    nq = s // tq
    steps = MLA_HEADS * nq
    share = lambda w: (w.shape[1] // steps, w.shape[2])
    step_row = lambda h, i: (h * nq + i, 0)
    for w in casts:
        assert w.shape[1] % (steps * BF16_SUBLANES) == 0
    return pl.pallas_call(
        functools.partial(_mla_attn_kernel, tk=tk, n_cast=len(casts)),
        grid=(MLA_HEADS, nq),
        in_specs=[pl.BlockSpec((tq, MLA_QK_PAD), lambda h, i: (i, h)),
                  pl.BlockSpec((s, MLA_QK_PAD), lambda h, i: (0, h)),
                  pl.BlockSpec((V_ROWS, s), lambda h, i: (h, 0))]
                 + [_layer_spec(share(w), l, step_row) for w in casts],
        out_specs=[pl.BlockSpec((tq, V_HEAD), lambda h, i: (i, h))]
                  + [pl.BlockSpec(share(w), step_row) for w in casts],
        out_shape=[jax.ShapeDtypeStruct((s, MLA_HEADS * V_HEAD), BF16)]
                  + [jax.ShapeDtypeStruct(w.shape[1:], BF16) for w in casts],
        scratch_shapes=[pltpu.VMEM((3 * F32_SUBLANES, tq), F32), pltpu.VMEM((V_ROWS, tq), F32)],
        compiler_params=_params(2),
        name="mla_attn",
    )(q, k, vt, *casts)


def _na_row_start(r):
    return min(max(r - NA_KH // 2, 0), ROWS - NA_KH)


def _na_win_start(blk):
    return min(max(blk * NA_Q_ROWS - NA_KH // 2, 0), ROWS - NA_WIN_ROWS)


def _na_row_window(blk, r_in_blk):
    r = blk * NA_Q_ROWS + r_in_blk
    return _na_row_start(r) - _na_win_start(blk), _na_row_start(r) - r + (NA_KH - 1)


def _na_rpb_rows(rpb):
    n_dy, n_dx = 2 * NA_KH - 1, 2 * NA_KW - 1
    return jnp.pad(rpb.astype(F32)[..., ::-1], ((0, 0), (0, 0), (0, 2 * NA_KH - n_dy), (0, LANES - n_dx)))


def _na_attn_kernel(q_ref, k_ref, vt_ref, r_ref, o_ref, toe_sc, bias_sc, sa_sc, sb_sc, cm_sc):
    last = NA_BLOCKS - 1
    ma_sc, mb_sc = (cm_sc.at[pl.ds(j * F32_SUBLANES, 1), :] for j in range(2))

    kc = lax.broadcasted_iota(jnp.int32, (GRID_W, LANES), 0)
    lane = lax.broadcasted_iota(jnp.int32, (GRID_W, LANES), 1)
    for half in range(2):
        qc = lane - half * GRID_W
        col_start = jnp.clip(qc - NA_KW // 2, 0, GRID_W - NA_KW)
        ok = (qc >= 0) & (qc < GRID_W) & (kc >= col_start) & (kc < col_start + NA_KW)
        shift = (LANES - (NA_KW - 1) + half * GRID_W) % LANES
        for dy in range(2 * NA_KH - 1):
            row = jnp.broadcast_to(r_ref[0, dy:dy + 1, :], (GRID_W, LANES))
            toe = pltpu.roll(row, shift, 1, stride=1, stride_axis=0)
            toe_sc[half, dy] = jnp.where(ok, toe * LOG2E, -jnp.inf)

    bias_sc[...] = jnp.full(bias_sc.shape, -jnp.inf, F32)
    for var, blk in enumerate((0, 1, last)):
        for pair in range(NA_Q_ROWS // 2):
            (off_e, dy_e), (off_o, dy_o) = _na_row_window(blk, 2 * pair), _na_row_window(blk, 2 * pair + 1)
            for b in range(min(off_e, off_o), max(off_e, off_o) + NA_KH):
                parts = []
                if 0 <= b - off_e < NA_KH:
                    parts.append(toe_sc[0, dy_e + b - off_e])
                if 0 <= b - off_o < NA_KH:
                    parts.append(toe_sc[1, dy_o + b - off_o])
                blkv = parts[0] if len(parts) == 1 else jnp.maximum(parts[0], parts[1])
                bias_sc[var, b * GRID_W:(b + 1) * GRID_W, pair * LANES:(pair + 1) * LANES] = blkv

    def win_start(blk):
        start = jnp.clip(blk * NA_TQ - (NA_KH // 2) * GRID_W, 0, SEQ - NA_TK)
        return pl.multiple_of(start, (NA_KH // 2) * GRID_W)

    def scores(blk, var, s_sc, cm_sc):
        q = q_ref[pl.ds(pl.multiple_of(blk * NA_TQ, NA_TQ), NA_TQ), :]
        k = k_ref[pl.ds(win_start(blk), NA_TK), :]
        st = lax.dot_general(k, q, NT_DIMS, preferred_element_type=F32)
        st = st * (NA_HEAD_DIM ** -0.5 * LOG2E) + bias_sc[var]
        s_sc[...] = st
        cm_sc[...] = jnp.max(st, axis=0, keepdims=True)

    def finish(blk, s_sc, cm_sc):
        vt = vt_ref[:, pl.ds(win_start(blk), NA_TK)]
        pt = jnp.exp2(s_sc[...] - cm_sc[...])
        acc = jnp.dot(vt, pt.astype(BF16), preferred_element_type=F32)
        o = (acc[:V_HEAD, :] / acc[V_HEAD:V_HEAD + 1, :]).T
        o_ref[pl.ds(pl.multiple_of(blk * NA_TQ, NA_TQ), NA_TQ), :] = o.astype(o_ref.dtype)

    assert NA_BLOCKS % 2 == 0
    scores(0, 0, sa_sc, ma_sc)

    def body(j, carry):
        b = 2 * j
        scores(b + 1, 1, sb_sc, mb_sc)
        finish(b, sa_sc, ma_sc)
        scores(b + 2, 1, sa_sc, ma_sc)
        finish(b + 1, sb_sc, mb_sc)
        return carry

    lax.fori_loop(0, NA_BLOCKS // 2 - 1, body, 0)
    scores(last, 2, sb_sc, mb_sc)
    finish(last - 1, sa_sc, ma_sc)
    finish(last, sb_sc, mb_sc)


def _na_attn(qk, vt, rpb_rows, l):
    s = qk.shape[0]
    return pl.pallas_call(
        _na_attn_kernel,
        grid=(NA_HEADS,),
        in_specs=[pl.BlockSpec((s, NA_HEAD_DIM), lambda h: (0, h)),
                  pl.BlockSpec((s, NA_HEAD_DIM), lambda h: (0, NA_HEADS + h)),
                  pl.BlockSpec((V_ROWS, s), lambda h: (h, 0)),
                  _layer_spec((1, 2 * NA_KH, LANES), l, lambda h: (h, 0, 0))],
        out_specs=pl.BlockSpec((s, NA_HEAD_DIM), lambda h: (0, h)),
        out_shape=jax.ShapeDtypeStruct((s, NA_HEADS * NA_HEAD_DIM), BF16),
        scratch_shapes=[pltpu.VMEM((2, 2 * NA_KH - 1, GRID_W, LANES), F32), pltpu.VMEM((3, NA_TK, NA_TQ), F32),
                        pltpu.VMEM((NA_TK, NA_TQ), F32), pltpu.VMEM((NA_TK, NA_TQ), F32),
                        pltpu.VMEM((2 * F32_SUBLANES, NA_TQ), F32)],
        compiler_params=_params(1),
        name="na_attn",
    )(qk, qk, vt, rpb_rows)


def _merge_kernel(ya_ref, yb_ref, g_ref, x_ref, woa_ref, wob_ref, wo_ref, o_ref):
    a = jnp.dot(ya_ref[...], woa_ref[...], preferred_element_type=F32)
    b = jnp.dot(yb_ref[...], wob_ref[...], preferred_element_type=F32)
    ga = jax.nn.sigmoid(g_ref[:, :D_MODEL])
    gb = jax.nn.sigmoid(g_ref[:, D_MODEL:])
    merged = (ga * a + gb * b).astype(BF16)
    o_ref[...] = x_ref[...] + jnp.dot(merged, wo_ref[...], preferred_element_type=F32)


def _merge(ya, yb, gates, x, woa, wob, wo, tm=256):
    s = ya.shape[0]
    row = lambda w: pl.BlockSpec((tm, w), lambda i: (i, 0))
    full = lambda a: pl.BlockSpec(a.shape, lambda i: (0, 0))
    return pl.pallas_call(
        _merge_kernel,
        grid=(s // tm,),
        in_specs=[row(ya.shape[1]), row(yb.shape[1]), row(gates.shape[1]), row(D_MODEL),
                  full(woa), full(wob), full(wo)],
        out_specs=row(D_MODEL),
        out_shape=jax.ShapeDtypeStruct((s, D_MODEL), F32),
        compiler_params=_params(1),
        name="merge_out",
    )(ya, yb, gates, x, woa, wob, wo)


def _ffn_kernel(x_ref, g_ref, w1_ref, w2_ref, gf_ref, o_ref, u_sc, *, final_norm):
    f = pl.program_id(1)

    @pl.when(f == 0)
    def _init():
        x = x_ref[...]
        u_sc[...] = _rms(x, g_ref[...]).astype(BF16)
        o_ref[...] = x

    h = jnp.dot(u_sc[...], w1_ref[...], preferred_element_type=F32)
    a = jnp.square(jnp.maximum(h, 0.0)).astype(BF16)
    o_ref[...] += jnp.dot(a, w2_ref[...], preferred_element_type=F32)

    if final_norm:
        @pl.when(f == pl.num_programs(1) - 1)
        def _final():
            o_ref[...] = _rms(o_ref[...], gf_ref[...])


def _ffn(x, g, w1, w2, l, g_final, final_norm, tm=512, tf=1024):
    s, d = x.shape
    return pl.pallas_call(
        functools.partial(_ffn_kernel, final_norm=final_norm),
        grid=(s // tm, w1.shape[1] // tf),
        in_specs=[pl.BlockSpec((tm, d), lambda i, f: (i, 0)),
                  _layer_spec((1, d), l, lambda i, f: (0, 0)),
                  pl.BlockSpec((d, tf), lambda i, f: (0, f)),
                  pl.BlockSpec((tf, d), lambda i, f: (f, 0)),
                  pl.BlockSpec((1, d), lambda i, f: (0, 0))],
        out_specs=pl.BlockSpec((tm, d), lambda i, f: (i, 0)),
        out_shape=jax.ShapeDtypeStruct((s, d), F32),
        scratch_shapes=[pltpu.VMEM((tm, d), BF16)],
        compiler_params=_params(2),
        name="ffn",
    )(x, g, w1, w2, g_final)


def _rope_tables(s):
    pos = jnp.arange(s, dtype=F32)
    inv_freq = 1.0 / (ROPE_THETA ** (jnp.arange(0, QK_ROPE, 2, dtype=F32) / QK_ROPE))
    ang = pos[:, None] * inv_freq[None, :]
    cos, sin = jnp.cos(ang), jnp.sin(ang)
    half = QK_ROPE // 2
    z = lambda w: jnp.zeros((s, w), F32)
    cos_t = jnp.concatenate([cos, cos, z(LANES - QK_ROPE)], axis=1)
    sl_t = jnp.concatenate([-sin, z(LANES - half)], axis=1)
    sr_t = jnp.concatenate([z(half), sin, z(LANES - QK_ROPE)], axis=1)
    return cos_t, sl_t, sr_t


def kernel(x, norm_mix, w_in, norm_qa, w_uq, norm_kva, w_ukv, rpb, w_o_mla, w_o_na, w_out,
           norm_mlp, w_ff1, w_ff2, norm_final):
    b, s, d = x.shape
    assert (b, s, d) == (1, SEQ, D_MODEL)
    depth = w_in.shape[0]
    cos_t, sl_t, sr_t = _rope_tables(s)

    w_in_t = w_in.transpose(0, 2, 1)
    uq = w_uq.reshape(depth, Q_LORA, MLA_HEADS, QK_NOPE + QK_ROPE)
    w_qn = uq[..., :QK_NOPE].reshape(depth, Q_LORA, MLA_HEADS * QK_NOPE).astype(BF16)
    w_qp = jnp.pad(uq[..., QK_NOPE:], ((0, 0), (0, 0), (0, 0), (0, LANES - QK_ROPE)))
    w_qp = w_qp.reshape(depth, Q_LORA, MLA_HEADS * LANES).astype(BF16)
    ukv = w_ukv.reshape(depth, KV_LORA, MLA_HEADS, QK_NOPE + V_HEAD)
    w_kn = ukv[..., :QK_NOPE].reshape(depth, KV_LORA, MLA_HEADS * QK_NOPE).astype(BF16)
    w_vt = ukv[..., QK_NOPE:].reshape(depth, KV_LORA, MLA_HEADS * V_HEAD).transpose(0, 2, 1).astype(BF16)
    g_qa = norm_qa.reshape(depth, 1, Q_LORA)
    g_kva = norm_kva.reshape(depth, 1, KV_LORA)
    g_mlp = norm_mlp.reshape(depth, 1, d)
    g_mix = norm_mix.reshape(depth, 1, d)
    g_final = norm_final.reshape(1, d)
    rpb_rows = _na_rpb_rows(rpb)

    xs = x.reshape(s, d)
    for l in range(depth):
        u, c = _norm_proj(xs, g_mix, w_in_t, l)
        qk_na = _mm_nt(u, w_in_t, l, ROW_QK, 2 * NA_W, BF16, PROJ_TM, PROJ_TN, name="proj_na_qk")
        vt_na = _proj_vt(u, w_in_t, l)
        gates = _mm_nt(u, w_in_t, l, ROW_G, 2 * D_MODEL, F32, PROJ_TM, PROJ_TN, name="proj_gate")
        q, k, vt = _mla_up(c, g_qa, g_kva, w_qn, w_qp, w_kn, w_vt, l, cos_t, sl_t, sr_t)
        y_a, w_oa, w_ob, w_o, w_1, w_2 = _mla_attn(q, k, vt, (w_o_mla, w_o_na, w_out, w_ff1, w_ff2), l)
        y_b = _na_attn(qk_na, vt_na, rpb_rows, l)
        xs = _merge(y_a, y_b, gates, xs, w_oa, w_ob, w_o)
        xs = _ffn(xs, g_mlp, w_1, w_2, l, g_final, final_norm=(l == depth - 1))
    return xs.reshape(b, s, d)
```

```python
import functools

import jax
import jax.numpy as jnp
import numpy as np
from jax import lax
from jax.experimental import pallas as pl
from jax.experimental.pallas import tpu as pltpu

D_MODEL = 2048
SEQ = 8192
GRID_W = 64
ROWS = SEQ // GRID_W
MLA_HEADS = 8
Q_LORA = 512
KV_LORA = 512
QK_NOPE = 128
QK_ROPE = 64
V_HEAD = 128
ROPE_THETA = 10000.0
NA_HEADS = 8
NA_HEAD_DIM = 128
NA_KH = 8
NA_KW = 16
D_FF = 4 * D_MODEL
EPS = 1e-6
LOG2E = float(np.log2(np.e))

LANES = 128
MLA_QK_PAD = 2 * LANES
F32_SUBLANES = 8
BF16_SUBLANES = 16
V_ROWS = V_HEAD + BF16_SUBLANES
MLA_REF_KEYS = 128
MLA_MAX_GAP = 60.0
NA_Q_ROWS = 8
NA_WIN_ROWS = 2 * NA_KH
NA_TQ = NA_Q_ROWS * GRID_W
NA_TK = NA_WIN_ROWS * GRID_W
NA_BLOCKS = SEQ // NA_TQ
NA_STRIP = NA_KH * GRID_W
VMEM_LIMIT = 56 * 1024 * 1024

BF16 = jnp.bfloat16
F32 = jnp.float32
NT_DIMS = (((1,), (1,)), ((), ()))


def _params(n_axes):
    return pltpu.CompilerParams(
        dimension_semantics=("arbitrary",) * n_axes, vmem_limit_bytes=VMEM_LIMIT)


def _rms(x, g):
    return x * lax.rsqrt(jnp.mean(x * x, axis=-1, keepdims=True) + EPS) * g


def _layer_spec(shape, l, index_map):
    return pl.BlockSpec((None,) + tuple(shape), lambda *ids: (l,) + tuple(index_map(*ids)))


def _store_vt_with_ones(vt_ref, vt, heads):
    for h in range(heads):
        vt_ref[h * V_ROWS:h * V_ROWS + V_HEAD, :] = vt[h * V_HEAD:(h + 1) * V_HEAD, :]
        vt_ref[h * V_ROWS + V_HEAD:(h + 1) * V_ROWS, :] = jnp.ones((V_ROWS - V_HEAD, vt.shape[1]), BF16)


C_END = Q_LORA + KV_LORA + QK_ROPE
C_PAD = C_END + LANES - QK_ROPE
NA_W = NA_HEADS * NA_HEAD_DIM
ROW_QK = C_END
ROW_V = ROW_QK + 2 * NA_W
ROW_G = ROW_V + NA_W
PROJ_TM = 1024
PROJ_TN = 1024


def _wt_spec(rows, k, l, row_map):
    return pl.BlockSpec((pl.Element(1), pl.Element(rows), pl.Element(k)),
                        lambda *ids: (l, row_map(*ids), 0))


def _mm_nt_kernel(a_ref, wt_ref, o_ref, wb_sc):
    @pl.when(pl.program_id(1) == 0)
    def _cast():
        wb_sc[...] = wt_ref[0].astype(BF16)

    o_ref[...] = lax.dot_general(a_ref[...], wb_sc[...], NT_DIMS,
                                 preferred_element_type=F32).astype(o_ref.dtype)


def _mm_nt(a, wt, l, row0, n, out_dtype, tm, tn, name):
    m, k = a.shape
    return pl.pallas_call(
        _mm_nt_kernel,
        grid=(n // tn, m // tm),
        in_specs=[pl.BlockSpec((tm, k), lambda j, i: (i, 0)),
                  _wt_spec(tn, k, l, lambda j, i: pl.multiple_of(row0 + j * tn, QK_ROPE))],
        out_specs=pl.BlockSpec((tm, tn), lambda j, i: (i, j)),
        out_shape=jax.ShapeDtypeStruct((m, n), out_dtype),
        scratch_shapes=[pltpu.VMEM((tn, k), BF16)],
        compiler_params=_params(2),
        name=name,
    )(a, wt)


def _norm_proj_kernel(x_ref, g_ref, wt_ref, u_ref, c_ref, wb_sc):
    @pl.when(pl.program_id(0) == 0)
    def _cast():
        wb_sc[...] = wt_ref[0].astype(BF16)

    u = _rms(x_ref[...], g_ref[...]).astype(BF16)
    u_ref[...] = u
    c_ref[...] = lax.dot_general(u, wb_sc[...], NT_DIMS, preferred_element_type=F32)


def _norm_proj(x, g, wt, l, tm=512):
    s, d = x.shape
    return pl.pallas_call(
        _norm_proj_kernel,
        grid=(s // tm,),
        in_specs=[pl.BlockSpec((tm, d), lambda i: (i, 0)),
                  _layer_spec((1, d), l, lambda i: (0, 0)),
                  _wt_spec(C_PAD, d, l, lambda i: 0)],
        out_specs=[pl.BlockSpec((tm, d), lambda i: (i, 0)), pl.BlockSpec((tm, C_PAD), lambda i: (i, 0))],
        out_shape=[jax.ShapeDtypeStruct((s, d), BF16), jax.ShapeDtypeStruct((s, C_PAD), F32)],
        scratch_shapes=[pltpu.VMEM((C_PAD, d), BF16)],
        compiler_params=_params(1),
        name="norm_proj_c",
    )(x, g, wt)


def _proj_vt_kernel(u_ref, wt_ref, vt_ref, wb_sc):
    @pl.when(pl.program_id(0) == 0)
    def _cast():
        wb_sc[...] = wt_ref[0].astype(BF16)

    vt = lax.dot_general(wb_sc[...], u_ref[...], NT_DIMS, preferred_element_type=F32).astype(BF16)
    _store_vt_with_ones(vt_ref, vt, NA_HEADS)


def _proj_vt(u, wt, l, tm=1024):
    s, k = u.shape
    return pl.pallas_call(
        _proj_vt_kernel,
        grid=(s // tm,),
        in_specs=[pl.BlockSpec((tm, k), lambda i: (i, 0)),
                  _wt_spec(NA_W, k, l, lambda i: ROW_V)],
        out_specs=pl.BlockSpec((NA_HEADS * V_ROWS, tm), lambda i: (0, i)),
        out_shape=jax.ShapeDtypeStruct((NA_HEADS * V_ROWS, s), BF16),
        scratch_shapes=[pltpu.VMEM((NA_W, k), BF16)],
        compiler_params=_params(1),
        name="proj_na_vt",
    )(u, wt)


def _rope128(x, c, sl, sr):
    return x * c + pltpu.roll(x, LANES - QK_ROPE // 2, 1) * sl + pltpu.roll(x, QK_ROPE // 2, 1) * sr


def _mla_up_kernel(c_ref, gq_ref, gkv_ref, wqn_ref, wqp_ref, wkn_ref, wvt_ref,
                   cos_ref, sl_ref, sr_ref, q_ref, k_ref, vt_ref):
    scale = (QK_NOPE + QK_ROPE) ** -0.5 * LOG2E
    cq = _rms(c_ref[:, :Q_LORA], gq_ref[...]).astype(BF16)
    ckv = _rms(c_ref[:, Q_LORA:Q_LORA + KV_LORA], gkv_ref[...]).astype(BF16)
    kpe = c_ref[:, Q_LORA + KV_LORA:]
    kpe = jnp.where(lax.broadcasted_iota(jnp.int32, kpe.shape, 1) < QK_ROPE, kpe, 0.0)
    cos, sl, sr = cos_ref[...], sl_ref[...], sr_ref[...]
    kpe_rot = _rope128(kpe, cos, sl, sr).astype(BF16)
    qn = jnp.dot(cq, wqn_ref[...], preferred_element_type=F32) * scale
    qp = jnp.dot(cq, wqp_ref[...], preferred_element_type=F32)
    kn = jnp.dot(ckv, wkn_ref[...], preferred_element_type=F32)
    vt = lax.dot_general(wvt_ref[...], ckv, NT_DIMS, preferred_element_type=F32).astype(BF16)
    _store_vt_with_ones(vt_ref, vt, MLA_HEADS)
    for h in range(MLA_HEADS):
        lo = h * MLA_QK_PAD
        hs = slice(h * LANES, (h + 1) * LANES)
        q_ref[:, lo:lo + LANES] = qn[:, hs].astype(BF16)
        q_ref[:, lo + LANES:lo + 2 * LANES] = (_rope128(qp[:, hs], cos, sl, sr) * scale).astype(BF16)
        k_ref[:, lo:lo + LANES] = kn[:, hs].astype(BF16)
        k_ref[:, lo + LANES:lo + 2 * LANES] = kpe_rot


def _mla_up(c, gq, gkv, wqn, wqp, wkn, wvt, l, cos_t, sl_t, sr_t, tm=1024):
    s = c.shape[0]
    row = lambda w: pl.BlockSpec((tm, w), lambda i: (i, 0))
    full = lambda a: _layer_spec(a.shape[1:], l, lambda i: (0, 0))
    hw = MLA_HEADS * V_ROWS
    return pl.pallas_call(
        _mla_up_kernel,
        grid=(s // tm,),
        in_specs=[row(c.shape[1]), full(gq), full(gkv), full(wqn), full(wqp), full(wkn), full(wvt),
                  row(LANES), row(LANES), row(LANES)],
        out_specs=[row(MLA_HEADS * MLA_QK_PAD), row(MLA_HEADS * MLA_QK_PAD),
                   pl.BlockSpec((hw, tm), lambda i: (0, i))],
        out_shape=[jax.ShapeDtypeStruct((s, MLA_HEADS * MLA_QK_PAD), BF16),
                   jax.ShapeDtypeStruct((s, MLA_HEADS * MLA_QK_PAD), BF16),
                   jax.ShapeDtypeStruct((hw, s), BF16)],
        compiler_params=_params(1),
        name="mla_up",
    )(c, gq, gkv, wqn, wqp, wkn, wvt, cos_t, sl_t, sr_t)


def _mla_attn_kernel(*refs, tk, n_cast):
    q_ref, k_ref, vt_ref = refs[:3]
    cast_src = refs[3:3 + n_cast]
    o_ref = refs[3 + n_cast]
    cast_dst = refs[4 + n_cast:4 + 2 * n_cast]
    stat_sc, acc_sc = refs[4 + 2 * n_cast:]
    r_sc, mx_sc, m_sc = (stat_sc.at[pl.ds(j * F32_SUBLANES, 1), :] for j in range(3))

    for src, dst in zip(cast_src, cast_dst):
        dst[...] = src[...].astype(BF16)

    q = q_ref[...]
    n_chunks = k_ref.shape[0] // tk
    assert n_chunks % 2 == 0

    def scores(c):
        k = k_ref[pl.ds(pl.multiple_of(c * tk, tk), tk), :]
        return lax.dot_general(k, q, NT_DIMS, preferred_element_type=F32)

    def values(c, pt):
        vt = vt_ref[:, pl.ds(pl.multiple_of(c * tk, tk), tk)]
        return jnp.dot(vt, pt.astype(BF16), preferred_element_type=F32)

    def write_out():
        o_ref[...] = (acc_sc[:V_HEAD, :] / acc_sc[V_HEAD:V_HEAD + 1, :]).T.astype(o_ref.dtype)

    s_ref = lax.dot_general(k_ref[:MLA_REF_KEYS, :], q, NT_DIMS, preferred_element_type=F32)
    r = jnp.max(s_ref, axis=0, keepdims=True)
    r_sc[...] = r
    mx_sc[...] = r
    acc_sc[...] = jnp.zeros(acc_sc.shape, F32)

    def step(c):
        st = scores(c)
        mx_sc[...] = jnp.maximum(mx_sc[...], jnp.max(st, axis=0, keepdims=True))
        acc_sc[...] += values(c, jnp.exp2(st - r_sc[...]))

    def single_pass_body(j, carry):
        step(2 * j)
        step(2 * j + 1)
        return carry

    lax.fori_loop(0, n_chunks // 2, single_pass_body, 0)
    in_range = jnp.max(mx_sc[...] - r_sc[...]) <= MLA_MAX_GAP

    @pl.when(in_range)
    def _accept():
        write_out()

    @pl.when(jnp.logical_not(in_range))
    def _online_softmax():
        m_sc[...] = jnp.full(m_sc.shape, -jnp.inf, F32)
        acc_sc[...] = jnp.zeros(acc_sc.shape, F32)

        def body(c, carry):
            st = scores(c)
            m_prev = m_sc[...]
            m_new = jnp.maximum(m_prev, jnp.max(st, axis=0, keepdims=True))
            acc_sc[...] = jnp.exp2(m_prev - m_new) * acc_sc[...] + values(c, jnp.exp2(st - m_new))
            m_sc[...] = m_new
            return carry

        lax.fori_loop(0, n_chunks, body, 0)
        write_out()


def _mla_attn(q, k, vt, casts=(), l=0, tq=2048, tk=2048):
    s = q.shape[0]
    nq = s // tq
    steps = MLA_HEADS * nq
    share = lambda w: (w.shape[1] // steps, w.shape[2])
    step_row = lambda h, i: (h * nq + i, 0)
    for w in casts:
        assert w.shape[1] % (steps * BF16_SUBLANES) == 0
    return pl.pallas_call(
        functools.partial(_mla_attn_kernel, tk=tk, n_cast=len(casts)),
        grid=(MLA_HEADS, nq),
        in_specs=[pl.BlockSpec((tq, MLA_QK_PAD), lambda h, i: (i, h)),
                  pl.BlockSpec((s, MLA_QK_PAD), lambda h, i: (0, h)),
                  pl.BlockSpec((V_ROWS, s), lambda h, i: (h, 0))]
                 + [_layer_spec(share(w), l, step_row) for w in casts],
        out_specs=[pl.BlockSpec((tq, V_HEAD), lambda h, i: (i, h))]
                  + [pl.BlockSpec(share(w), step_row) for w in casts],
        out_shape=[jax.ShapeDtypeStruct((s, MLA_HEADS * V_HEAD), BF16)]
                  + [jax.ShapeDtypeStruct(w.shape[1:], BF16) for w in casts],
        scratch_shapes=[pltpu.VMEM((3 * F32_SUBLANES, tq), F32), pltpu.VMEM((V_ROWS, tq), F32)],
        compiler_params=_params(2),
        name="mla_attn",
    )(q, k, vt, *casts)


def _na_row_start(r):
    return min(max(r - NA_KH // 2, 0), ROWS - NA_KH)


def _na_win_start(blk):
    return min(max(blk * NA_Q_ROWS - NA_KH // 2, 0), ROWS - NA_WIN_ROWS)


def _na_row_window(blk, r_in_blk):
    r = blk * NA_Q_ROWS + r_in_blk
    return _na_row_start(r) - _na_win_start(blk), _na_row_start(r) - r + (NA_KH - 1)


def _na_rpb_rows(rpb):
    n_dy, n_dx = 2 * NA_KH - 1, 2 * NA_KW - 1
    return jnp.pad(rpb.astype(F32)[..., ::-1], ((0, 0), (0, 0), (0, 2 * NA_KH - n_dy), (0, LANES - n_dx)))


def _na_attn_kernel(q_ref, k_ref, vt_ref, r_ref, o_ref, toe_sc, bias_sc, sa_sc, sb_sc, cm_sc):
    last = NA_BLOCKS - 1
    ma_sc, mb_sc = (cm_sc.at[pl.ds(j * F32_SUBLANES, 1), :] for j in range(2))

    kc = lax.broadcasted_iota(jnp.int32, (GRID_W, LANES), 0)
    lane = lax.broadcasted_iota(jnp.int32, (GRID_W, LANES), 1)
    for half in range(2):
        qc = lane - half * GRID_W
        col_start = jnp.clip(qc - NA_KW // 2, 0, GRID_W - NA_KW)
        ok = (qc >= 0) & (qc < GRID_W) & (kc >= col_start) & (kc < col_start + NA_KW)
        shift = (LANES - (NA_KW - 1) + half * GRID_W) % LANES
        for dy in range(2 * NA_KH - 1):
            row = jnp.broadcast_to(r_ref[0, dy:dy + 1, :], (GRID_W, LANES))
            toe = pltpu.roll(row, shift, 1, stride=1, stride_axis=0)
            toe_sc[half, dy] = jnp.where(ok, toe * LOG2E, -jnp.inf)

    bias_sc[...] = jnp.full(bias_sc.shape, -jnp.inf, F32)
    for var, blk in enumerate((0, 1, last)):
        for pair in range(NA_Q_ROWS // 2):
            (off_e, dy_e), (off_o, dy_o) = _na_row_window(blk, 2 * pair), _na_row_window(blk, 2 * pair + 1)
            for b in range(min(off_e, off_o), max(off_e, off_o) + NA_KH):
                parts = []
                if 0 <= b - off_e < NA_KH:
                    parts.append(toe_sc[0, dy_e + b - off_e])
                if 0 <= b - off_o < NA_KH:
                    parts.append(toe_sc[1, dy_o + b - off_o])
                blkv = parts[0] if len(parts) == 1 else jnp.maximum(parts[0], parts[1])
                bias_sc[var, b * GRID_W:(b + 1) * GRID_W, pair * LANES:(pair + 1) * LANES] = blkv

    def win_start(blk):
        start = jnp.clip(blk * NA_TQ - (NA_KH // 2) * GRID_W, 0, SEQ - NA_TK)
        return pl.multiple_of(start, (NA_KH // 2) * GRID_W)

    def scores(blk, var, s_sc, cm_sc):
        q = q_ref[pl.ds(pl.multiple_of(blk * NA_TQ, NA_TQ), NA_TQ), :]
        k = k_ref[pl.ds(win_start(blk), NA_TK), :]
        st = lax.dot_general(k, q, NT_DIMS, preferred_element_type=F32)
        st = st * (NA_HEAD_DIM ** -0.5 * LOG2E) + bias_sc[var]
        s_sc[...] = st
        cm_sc[...] = jnp.max(st, axis=0, keepdims=True)

    def finish(blk, s_sc, cm_sc):
        vt = vt_ref[:, pl.ds(win_start(blk), NA_TK)]
        pt = jnp.exp2(s_sc[...] - cm_sc[...])
        acc = jnp.dot(vt, pt.astype(BF16), preferred_element_type=F32)
        o = (acc[:V_HEAD, :] / acc[V_HEAD:V_HEAD + 1, :]).T
        o_ref[pl.ds(pl.multiple_of(blk * NA_TQ, NA_TQ), NA_TQ), :] = o.astype(o_ref.dtype)

    assert NA_BLOCKS % 2 == 0
    scores(0, 0, sa_sc, ma_sc)

    def body(j, carry):
        b = 2 * j
        scores(b + 1, 1, sb_sc, mb_sc)
        finish(b, sa_sc, ma_sc)
        scores(b + 2, 1, sa_sc, ma_sc)
        finish(b + 1, sb_sc, mb_sc)
        return carry

    lax.fori_loop(0, NA_BLOCKS // 2 - 1, body, 0)
    scores(last, 2, sb_sc, mb_sc)
    finish(last - 1, sa_sc, ma_sc)
    finish(last, sb_sc, mb_sc)


def _na_attn(qk, vt, rpb_rows, l):
    s = qk.shape[0]
    return pl.pallas_call(
        _na_attn_kernel,
        grid=(NA_HEADS,),
        in_specs=[pl.BlockSpec((s, NA_HEAD_DIM), lambda h: (0, h)),
                  pl.BlockSpec((s, NA_HEAD_DIM), lambda h: (0, NA_HEADS + h)),
                  pl.BlockSpec((V_ROWS, s), lambda h: (h, 0)),
                  _layer_spec((1, 2 * NA_KH, LANES), l, lambda h: (h, 0, 0))],
        out_specs=pl.BlockSpec((s, NA_HEAD_DIM), lambda h: (0, h)),
        out_shape=jax.ShapeDtypeStruct((s, NA_HEADS * NA_HEAD_DIM), BF16),
        scratch_shapes=[pltpu.VMEM((2, 2 * NA_KH - 1, GRID_W, LANES), F32), pltpu.VMEM((3, NA_TK, NA_TQ), F32),
                        pltpu.VMEM((NA_TK, NA_TQ), F32), pltpu.VMEM((NA_TK, NA_TQ), F32),
                        pltpu.VMEM((2 * F32_SUBLANES, NA_TQ), F32)],
        compiler_params=_params(1),
        name="na_attn",
    )(qk, qk, vt, rpb_rows)


def _merge_kernel(ya_ref, yb_ref, g_ref, x_ref, woa_ref, wob_ref, wo_ref, o_ref):
    a = jnp.dot(ya_ref[...], woa_ref[...], preferred_element_type=F32)
    b = jnp.dot(yb_ref[...], wob_ref[...], preferred_element_type=F32)
    ga = jax.nn.sigmoid(g_ref[:, :D_MODEL])
    gb = jax.nn.sigmoid(g_ref[:, D_MODEL:])
    merged = (ga * a + gb * b).astype(BF16)
    o_ref[...] = x_ref[...] + jnp.dot(merged, wo_ref[...], preferred_element_type=F32)


def _merge(ya, yb, gates, x, woa, wob, wo, tm=256):
    s = ya.shape[0]
    row = lambda w: pl.BlockSpec((tm, w), lambda i: (i, 0))
    full = lambda a: pl.BlockSpec(a.shape, lambda i: (0, 0))
    return pl.pallas_call(
        _merge_kernel,
        grid=(s // tm,),
        in_specs=[row(ya.shape[1]), row(yb.shape[1]), row(gates.shape[1]), row(D_MODEL),
                  full(woa), full(wob), full(wo)],
        out_specs=row(D_MODEL),
        out_shape=jax.ShapeDtypeStruct((s, D_MODEL), F32),
        compiler_params=_params(1),
        name="merge_out",
    )(ya, yb, gates, x, woa, wob, wo)


def _ffn_kernel(x_ref, g_ref, w1_ref, w2_ref, gf_ref, o_ref, u_sc, *, final_norm):
    f = pl.program_id(1)

    @pl.when(f == 0)
    def _init():
        x = x_ref[...]
        u_sc[...] = _rms(x, g_ref[...]).astype(BF16)
        o_ref[...] = x

    h = jnp.dot(u_sc[...], w1_ref[...], preferred_element_type=F32)
    a = jnp.square(jnp.maximum(h, 0.0)).astype(BF16)
    o_ref[...] += jnp.dot(a, w2_ref[...], preferred_element_type=F32)

    if final_norm:
        @pl.when(f == pl.num_programs(1) - 1)
        def _final():
            o_ref[...] = _rms(o_ref[...], gf_ref[...])


def _ffn(x, g, w1, w2, l, g_final, final_norm, tm=512, tf=1024):
    s, d = x.shape
    return pl.pallas_call(
        functools.partial(_ffn_kernel, final_norm=final_norm),
        grid=(s // tm, w1.shape[1] // tf),
        in_specs=[pl.BlockSpec((tm, d), lambda i, f: (i, 0)),
                  _layer_spec((1, d), l, lambda i, f: (0, 0)),
                  pl.BlockSpec((d, tf), lambda i, f: (0, f)),
                  pl.BlockSpec((tf, d), lambda i, f: (f, 0)),
                  pl.BlockSpec((1, d), lambda i, f: (0, 0))],
        out_specs=pl.BlockSpec((tm, d), lambda i, f: (i, 0)),
        out_shape=jax.ShapeDtypeStruct((s, d), F32),
        scratch_shapes=[pltpu.VMEM((tm, d), BF16)],
        compiler_params=_params(2),
        name="ffn",
    )(x, g, w1, w2, g_final)


def _rope_tables(s):
    pos = jnp.arange(s, dtype=F32)
    inv_freq = 1.0 / (ROPE_THETA ** (jnp.arange(0, QK_ROPE, 2, dtype=F32) / QK_ROPE))
    ang = pos[:, None] * inv_freq[None, :]
    cos, sin = jnp.cos(ang), jnp.sin(ang)
    half = QK_ROPE // 2
    z = lambda w: jnp.zeros((s, w), F32)
    cos_t = jnp.concatenate([cos, cos, z(LANES - QK_ROPE)], axis=1)
    sl_t = jnp.concatenate([-sin, z(LANES - half)], axis=1)
    sr_t = jnp.concatenate([z(half), sin, z(LANES - QK_ROPE)], axis=1)
    return cos_t, sl_t, sr_t


def kernel(x, norm_mix, w_in, norm_qa, w_uq, norm_kva, w_ukv, rpb, w_o_mla, w_o_na, w_out,
           norm_mlp, w_ff1, w_ff2, norm_final):
    b, s, d = x.shape
    assert (b, s, d) == (1, SEQ, D_MODEL)
    depth = w_in.shape[0]
    cos_t, sl_t, sr_t = _rope_tables(s)

    w_in_t = w_in.transpose(0, 2, 1)
    uq = w_uq.reshape(depth, Q_LORA, MLA_HEADS, QK_NOPE + QK_ROPE)
    w_qn = uq[..., :QK_NOPE].reshape(depth, Q_LORA, MLA_HEADS * QK_NOPE).astype(BF16)
    w_qp = jnp.pad(uq[..., QK_NOPE:], ((0, 0), (0, 0), (0, 0), (0, LANES - QK_ROPE)))
    w_qp = w_qp.reshape(depth, Q_LORA, MLA_HEADS * LANES).astype(BF16)
    ukv = w_ukv.reshape(depth, KV_LORA, MLA_HEADS, QK_NOPE + V_HEAD)
    w_kn = ukv[..., :QK_NOPE].reshape(depth, KV_LORA, MLA_HEADS * QK_NOPE).astype(BF16)
    w_vt = ukv[..., QK_NOPE:].reshape(depth, KV_LORA, MLA_HEADS * V_HEAD).transpose(0, 2, 1).astype(BF16)
    g_qa = norm_qa.reshape(depth, 1, Q_LORA)
    g_kva = norm_kva.reshape(depth, 1, KV_LORA)
    g_mlp = norm_mlp.reshape(depth, 1, d)
    g_mix = norm_mix.reshape(depth, 1, d)
    g_final = norm_final.reshape(1, d)
    rpb_rows = _na_rpb_rows(rpb)

    xs = x.reshape(s, d)
    for l in range(depth):
        u, c = _norm_proj(xs, g_mix, w_in_t, l)
        qk_na = _mm_nt(u, w_in_t, l, ROW_QK, 2 * NA_W, BF16, PROJ_TM, PROJ_TN, name="proj_na_qk")
        vt_na = _proj_vt(u, w_in_t, l)
        gates = _mm_nt(u, w_in_t, l, ROW_G, 2 * D_MODEL, F32, PROJ_TM, PROJ_TN, name="proj_gate")
        q, k, vt = _mla_up(c, g_qa, g_kva, w_qn, w_qp, w_kn, w_vt, l, cos_t, sl_t, sr_t)
        y_a, w_oa, w_ob, w_o, w_1, w_2 = _mla_attn(q, k, vt, (w_o_mla, w_o_na, w_out, w_ff1, w_ff2), l)
        y_b = _na_attn(qk_na, vt_na, rpb_rows, l)
        xs = _merge(y_a, y_b, gates, xs, w_oa, w_ob, w_o)
        xs = _ffn(xs, g_mlp, w_1, w_2, l, g_final, final_norm=(l == depth - 1))
    return xs.reshape(b, s, d)
```

```python
import functools

import jax
import jax.numpy as jnp
import numpy as np
from jax import lax
from jax.experimental import pallas as pl
from jax.experimental.pallas import tpu as pltpu

D_MODEL = 2048
SEQ = 8192
GRID_W = 64
ROWS = SEQ // GRID_W
MLA_HEADS = 8
Q_LORA = 512
KV_LORA = 512
QK_NOPE = 128
QK_ROPE = 64
V_HEAD = 128
ROPE_THETA = 10000.0
NA_HEADS = 8
NA_HEAD_DIM = 128
NA_KH = 8
NA_KW = 16
D_FF = 4 * D_MODEL
EPS = 1e-6
LOG2E = float(np.log2(np.e))

LANES = 128
MLA_QK_PAD = 2 * LANES
F32_SUBLANES = 8
BF16_SUBLANES = 16
V_ROWS = V_HEAD + BF16_SUBLANES
MLA_REF_KEYS = 128
MLA_MAX_GAP = 60.0
NA_Q_ROWS = 8
NA_WIN_ROWS = 2 * NA_KH
NA_TQ = NA_Q_ROWS * GRID_W
NA_TK = NA_WIN_ROWS * GRID_W
NA_BLOCKS = SEQ // NA_TQ
NA_STRIP = NA_KH * GRID_W
VMEM_LIMIT = 56 * 1024 * 1024

BF16 = jnp.bfloat16
F32 = jnp.float32
NT_DIMS = (((1,), (1,)), ((), ()))


def _params(n_axes):
    return pltpu.CompilerParams(
        dimension_semantics=("arbitrary",) * n_axes, vmem_limit_bytes=VMEM_LIMIT)


def _rms(x, g):
    return x * lax.rsqrt(jnp.mean(x * x, axis=-1, keepdims=True) + EPS) * g


def _layer_spec(shape, l, index_map):
    return pl.BlockSpec((None,) + tuple(shape), lambda *ids: (l,) + tuple(index_map(*ids)))


def _store_vt_with_ones(vt_ref, vt, heads):
    for h in range(heads):
        vt_ref[h * V_ROWS:h * V_ROWS + V_HEAD, :] = vt[h * V_HEAD:(h + 1) * V_HEAD, :]
        vt_ref[h * V_ROWS + V_HEAD:(h + 1) * V_ROWS, :] = jnp.ones((V_ROWS - V_HEAD, vt.shape[1]), BF16)


C_END = Q_LORA + KV_LORA + QK_ROPE
C_PAD = C_END + LANES - QK_ROPE
NA_W = NA_HEADS * NA_HEAD_DIM
ROW_QK = C_END
ROW_V = ROW_QK + 2 * NA_W
ROW_G = ROW_V + NA_W
PROJ_TM = 1024
PROJ_TN = 1024


def _wt_spec(rows, k, l, row_map):
    return pl.BlockSpec((pl.Element(1), pl.Element(rows), pl.Element(k)),
                        lambda *ids: (l, row_map(*ids), 0))


def _mm_nt_kernel(a_ref, wt_ref, o_ref, wb_sc):
    @pl.when(pl.program_id(1) == 0)
    def _cast():
        wb_sc[...] = wt_ref[0].astype(BF16)

    o_ref[...] = lax.dot_general(a_ref[...], wb_sc[...], NT_DIMS,
                                 preferred_element_type=F32).astype(o_ref.dtype)


def _mm_nt(a, wt, l, row0, n, out_dtype, tm, tn, name):
    m, k = a.shape
    return pl.pallas_call(
        _mm_nt_kernel,
        grid=(n // tn, m // tm),
        in_specs=[pl.BlockSpec((tm, k), lambda j, i: (i, 0)),
                  _wt_spec(tn, k, l, lambda j, i: pl.multiple_of(row0 + j * tn, QK_ROPE))],
        out_specs=pl.BlockSpec((tm, tn), lambda j, i: (i, j)),
        out_shape=jax.ShapeDtypeStruct((m, n), out_dtype),
        scratch_shapes=[pltpu.VMEM((tn, k), BF16)],
        compiler_params=_params(2),
        name=name,
    )(a, wt)


def _norm_proj_kernel(x_ref, g_ref, wt_ref, u_ref, c_ref, wb_sc):
    @pl.when(pl.program_id(0) == 0)
    def _cast():
        wb_sc[...] = wt_ref[0].astype(BF16)

    u = _rms(x_ref[...], g_ref[...]).astype(BF16)
    u_ref[...] = u
    c_ref[...] = lax.dot_general(u, wb_sc[...], NT_DIMS, preferred_element_type=F32)


def _norm_proj(x, g, wt, l, tm=512):
    s, d = x.shape
    return pl.pallas_call(
        _norm_proj_kernel,
        grid=(s // tm,),
        in_specs=[pl.BlockSpec((tm, d), lambda i: (i, 0)),
                  _layer_spec((1, d), l, lambda i: (0, 0)),
                  _wt_spec(C_PAD, d, l, lambda i: 0)],
        out_specs=[pl.BlockSpec((tm, d), lambda i: (i, 0)), pl.BlockSpec((tm, C_PAD), lambda i: (i, 0))],
        out_shape=[jax.ShapeDtypeStruct((s, d), BF16), jax.ShapeDtypeStruct((s, C_PAD), F32)],
        scratch_shapes=[pltpu.VMEM((C_PAD, d), BF16)],
        compiler_params=_params(1),
        name="norm_proj_c",
    )(x, g, wt)


def _proj_vt_kernel(u_ref, wt_ref, vt_ref, wb_sc):
    @pl.when(pl.program_id(0) == 0)
    def _cast():
        wb_sc[...] = wt_ref[0].astype(BF16)

    vt = lax.dot_general(wb_sc[...], u_ref[...], NT_DIMS, preferred_element_type=F32).astype(BF16)
    _store_vt_with_ones(vt_ref, vt, NA_HEADS)


def _proj_vt(u, wt, l, tm=1024):
    s, k = u.shape
    return pl.pallas_call(
        _proj_vt_kernel,
        grid=(s // tm,),
        in_specs=[pl.BlockSpec((tm, k), lambda i: (i, 0)),
                  _wt_spec(NA_W, k, l, lambda i: ROW_V)],
        out_specs=pl.BlockSpec((NA_HEADS * V_ROWS, tm), lambda i: (0, i)),
        out_shape=jax.ShapeDtypeStruct((NA_HEADS * V_ROWS, s), BF16),
        scratch_shapes=[pltpu.VMEM((NA_W, k), BF16)],
        compiler_params=_params(1),
        name="proj_na_vt",
    )(u, wt)


def _rope128(x, c, sl, sr):
    return x * c + pltpu.roll(x, LANES - QK_ROPE // 2, 1) * sl + pltpu.roll(x, QK_ROPE // 2, 1) * sr


def _mla_up_kernel(c_ref, gq_ref, gkv_ref, wqn_ref, wqp_ref, wkn_ref, wvt_ref,
                   cos_ref, sl_ref, sr_ref, q_ref, k_ref, vt_ref):
    scale = (QK_NOPE + QK_ROPE) ** -0.5 * LOG2E
    cq = _rms(c_ref[:, :Q_LORA], gq_ref[...]).astype(BF16)
    ckv = _rms(c_ref[:, Q_LORA:Q_LORA + KV_LORA], gkv_ref[...]).astype(BF16)
    kpe = c_ref[:, Q_LORA + KV_LORA:]
    kpe = jnp.where(lax.broadcasted_iota(jnp.int32, kpe.shape, 1) < QK_ROPE, kpe, 0.0)
    cos, sl, sr = cos_ref[...], sl_ref[...], sr_ref[...]
    kpe_rot = _rope128(kpe, cos, sl, sr).astype(BF16)
    qn = jnp.dot(cq, wqn_ref[...], preferred_element_type=F32) * scale
    qp = jnp.dot(cq, wqp_ref[...], preferred_element_type=F32)
    kn = jnp.dot(ckv, wkn_ref[...], preferred_element_type=F32)
    vt = lax.dot_general(wvt_ref[...], ckv, NT_DIMS, preferred_element_type=F32).astype(BF16)
    _store_vt_with_ones(vt_ref, vt, MLA_HEADS)
    for h in range(MLA_HEADS):
        lo = h * MLA_QK_PAD
        hs = slice(h * LANES, (h + 1) * LANES)
        q_ref[:, lo:lo + LANES] = qn[:, hs].astype(BF16)
        q_ref[:, lo + LANES:lo + 2 * LANES] = (_rope128(qp[:, hs], cos, sl, sr) * scale).astype(BF16)
        k_ref[:, lo:lo + LANES] = kn[:, hs].astype(BF16)
        k_ref[:, lo + LANES:lo + 2 * LANES] = kpe_rot


def _mla_up(c, gq, gkv, wqn, wqp, wkn, wvt, l, cos_t, sl_t, sr_t, tm=1024):
    s = c.shape[0]
    row = lambda w: pl.BlockSpec((tm, w), lambda i: (i, 0))
    full = lambda a: _layer_spec(a.shape[1:], l, lambda i: (0, 0))
    hw = MLA_HEADS * V_ROWS
    return pl.pallas_call(
        _mla_up_kernel,
        grid=(s // tm,),
        in_specs=[row(c.shape[1]), full(gq), full(gkv), full(wqn), full(wqp), full(wkn), full(wvt),
                  row(LANES), row(LANES), row(LANES)],
        out_specs=[row(MLA_HEADS * MLA_QK_PAD), row(MLA_HEADS * MLA_QK_PAD),
                   pl.BlockSpec((hw, tm), lambda i: (0, i))],
        out_shape=[jax.ShapeDtypeStruct((s, MLA_HEADS * MLA_QK_PAD), BF16),
                   jax.ShapeDtypeStruct((s, MLA_HEADS * MLA_QK_PAD), BF16),
                   jax.ShapeDtypeStruct((hw, s), BF16)],
        compiler_params=_params(1),
        name="mla_up",
    )(c, gq, gkv, wqn, wqp, wkn, wvt, cos_t, sl_t, sr_t)


def _mla_attn_kernel(*refs, tk, n_cast):
    q_ref, k_ref, vt_ref = refs[:3]
    cast_src = refs[3:3 + n_cast]
    o_ref = refs[3 + n_cast]
    cast_dst = refs[4 + n_cast:4 + 2 * n_cast]
    stat_sc, acc_sc = refs[4 + 2 * n_cast:]
    r_sc, mx_sc, m_sc = (stat_sc.at[pl.ds(j * F32_SUBLANES, 1), :] for j in range(3))

    for src, dst in zip(cast_src, cast_dst):
        dst[...] = src[0].astype(BF16)

    q = q_ref[...]
    n_chunks = k_ref.shape[0] // tk
    assert n_chunks % 2 == 0

    def scores(c):
        k = k_ref[pl.ds(pl.multiple_of(c * tk, tk), tk), :]
        return lax.dot_general(k, q, NT_DIMS, preferred_element_type=F32)

    def values(c, pt):
        vt = vt_ref[:, pl.ds(pl.multiple_of(c * tk, tk), tk)]
        return jnp.dot(vt, pt.astype(BF16), preferred_element_type=F32)

    def write_out():
        o_ref[...] = (acc_sc[:V_HEAD, :] / acc_sc[V_HEAD:V_HEAD + 1, :]).T.astype(o_ref.dtype)

    s_ref = lax.dot_general(k_ref[:MLA_REF_KEYS, :], q, NT_DIMS, preferred_element_type=F32)
    r = jnp.max(s_ref, axis=0, keepdims=True)
    r_sc[...] = r
    mx_sc[...] = r
    acc_sc[...] = jnp.zeros(acc_sc.shape, F32)

    def step(c):
        st = scores(c)
        mx_sc[...] = jnp.maximum(mx_sc[...], jnp.max(st, axis=0, keepdims=True))
        acc_sc[...] += values(c, jnp.exp2(st - r_sc[...]))

    def single_pass_body(j, carry):
        step(2 * j)
        step(2 * j + 1)
        return carry

    lax.fori_loop(0, n_chunks // 2, single_pass_body, 0)
    in_range = jnp.max(mx_sc[...] - r_sc[...]) <= MLA_MAX_GAP

    @pl.when(in_range)
    def _accept():
        write_out()

    @pl.when(jnp.logical_not(in_range))
    def _online_softmax():
        m_sc[...] = jnp.full(m_sc.shape, -jnp.inf, F32)
        acc_sc[...] = jnp.zeros(acc_sc.shape, F32)

        def body(c, carry):
            st = scores(c)
            m_prev = m_sc[...]
            m_new = jnp.maximum(m_prev, jnp.max(st, axis=0, keepdims=True))
            acc_sc[...] = jnp.exp2(m_prev - m_new) * acc_sc[...] + values(c, jnp.exp2(st - m_new))
            m_sc[...] = m_new
            return carry

        lax.fori_loop(0, n_chunks, body, 0)
        write_out()


def _mla_attn(q, k, vt, casts=(), l=0, tq=2048, tk=2048):
    s = q.shape[0]
    nq = s // tq
    steps = MLA_HEADS * nq
    for _, _, rows in casts:
        assert rows % (steps * BF16_SUBLANES) == 0

    def src_spec(w, row0, rows):
        share = rows // steps
        return _wt_spec(share, w.shape[2], l, lambda h, i: pl.multiple_of(row0 + (h * nq + i) * share, BF16_SUBLANES))

    dst_spec = lambda w, rows: pl.BlockSpec((rows // steps, w.shape[2]), lambda h, i: (h * nq + i, 0))
    return pl.pallas_call(
        functools.partial(_mla_attn_kernel, tk=tk, n_cast=len(casts)),
        grid=(MLA_HEADS, nq),
        in_specs=[pl.BlockSpec((tq, MLA_QK_PAD), lambda h, i: (i, h)),
                  pl.BlockSpec((s, MLA_QK_PAD), lambda h, i: (0, h)),
                  pl.BlockSpec((V_ROWS, s), lambda h, i: (h, 0))]
                 + [src_spec(w, row0, rows) for w, row0, rows in casts],
        out_specs=[pl.BlockSpec((tq, V_HEAD), lambda h, i: (i, h))]
                  + [dst_spec(w, rows) for w, _, rows in casts],
        out_shape=[jax.ShapeDtypeStruct((s, MLA_HEADS * V_HEAD), BF16)]
                  + [jax.ShapeDtypeStruct((rows, w.shape[2]), BF16) for w, _, rows in casts],
        scratch_shapes=[pltpu.VMEM((3 * F32_SUBLANES, tq), F32), pltpu.VMEM((V_ROWS, tq), F32)],
        compiler_params=_params(2),
        name="mla_attn",
    )(q, k, vt, *[w for w, _, _ in casts])


def _na_row_start(r):
    return min(max(r - NA_KH // 2, 0), ROWS - NA_KH)


def _na_win_start(blk):
    return min(max(blk * NA_Q_ROWS - NA_KH // 2, 0), ROWS - NA_WIN_ROWS)


def _na_row_window(blk, r_in_blk):
    r = blk * NA_Q_ROWS + r_in_blk
    return _na_row_start(r) - _na_win_start(blk), _na_row_start(r) - r + (NA_KH - 1)


def _na_rpb_rows(rpb):
    n_dy, n_dx = 2 * NA_KH - 1, 2 * NA_KW - 1
    return jnp.pad(rpb.astype(F32)[..., ::-1], ((0, 0), (0, 0), (0, 2 * NA_KH - n_dy), (0, LANES - n_dx)))


def _na_attn_kernel(q_ref, k_ref, vt_ref, r_ref, o_ref, toe_sc, bias_sc, sa_sc, sb_sc, cm_sc):
    last = NA_BLOCKS - 1
    ma_sc, mb_sc = (cm_sc.at[pl.ds(j * F32_SUBLANES, 1), :] for j in range(2))

    kc = lax.broadcasted_iota(jnp.int32, (GRID_W, LANES), 0)
    lane = lax.broadcasted_iota(jnp.int32, (GRID_W, LANES), 1)
    for half in range(2):
        qc = lane - half * GRID_W
        col_start = jnp.clip(qc - NA_KW // 2, 0, GRID_W - NA_KW)
        ok = (qc >= 0) & (qc < GRID_W) & (kc >= col_start) & (kc < col_start + NA_KW)
        shift = (LANES - (NA_KW - 1) + half * GRID_W) % LANES
        for dy in range(2 * NA_KH - 1):
            row = jnp.broadcast_to(r_ref[0, dy:dy + 1, :], (GRID_W, LANES))
            toe = pltpu.roll(row, shift, 1, stride=1, stride_axis=0)
            toe_sc[half, dy] = jnp.where(ok, toe * LOG2E, -jnp.inf)

    bias_sc[...] = jnp.full(bias_sc.shape, -jnp.inf, F32)
    for var, blk in enumerate((0, 1, last)):
        for pair in range(NA_Q_ROWS // 2):
            (off_e, dy_e), (off_o, dy_o) = _na_row_window(blk, 2 * pair), _na_row_window(blk, 2 * pair + 1)
            for b in range(min(off_e, off_o), max(off_e, off_o) + NA_KH):
                parts = []
                if 0 <= b - off_e < NA_KH:
                    parts.append(toe_sc[0, dy_e + b - off_e])
                if 0 <= b - off_o < NA_KH:
                    parts.append(toe_sc[1, dy_o + b - off_o])
                blkv = parts[0] if len(parts) == 1 else jnp.maximum(parts[0], parts[1])
                bias_sc[var, b * GRID_W:(b + 1) * GRID_W, pair * LANES:(pair + 1) * LANES] = blkv

    def win_start(blk):
        start = jnp.clip(blk * NA_TQ - (NA_KH // 2) * GRID_W, 0, SEQ - NA_TK)
        return pl.multiple_of(start, (NA_KH // 2) * GRID_W)

    def scores(blk, var, s_sc, cm_sc):
        q = q_ref[pl.ds(pl.multiple_of(blk * NA_TQ, NA_TQ), NA_TQ), :]
        k = k_ref[pl.ds(win_start(blk), NA_TK), :]
        st = lax.dot_general(k, q, NT_DIMS, preferred_element_type=F32)
        st = st * (NA_HEAD_DIM ** -0.5 * LOG2E) + bias_sc[var]
        s_sc[...] = st
        cm_sc[...] = jnp.max(st, axis=0, keepdims=True)

    def finish(blk, s_sc, cm_sc):
        vt = vt_ref[:, pl.ds(win_start(blk), NA_TK)]
        pt = jnp.exp2(s_sc[...] - cm_sc[...])
        acc = jnp.dot(vt, pt.astype(BF16), preferred_element_type=F32)
        o = (acc[:V_HEAD, :] / acc[V_HEAD:V_HEAD + 1, :]).T
        o_ref[pl.ds(pl.multiple_of(blk * NA_TQ, NA_TQ), NA_TQ), :] = o.astype(o_ref.dtype)

    assert NA_BLOCKS % 2 == 0
    scores(0, 0, sa_sc, ma_sc)

    def body(j, carry):
        b = 2 * j
        scores(b + 1, 1, sb_sc, mb_sc)
        finish(b, sa_sc, ma_sc)
        scores(b + 2, 1, sa_sc, ma_sc)
        finish(b + 1, sb_sc, mb_sc)
        return carry

    lax.fori_loop(0, NA_BLOCKS // 2 - 1, body, 0)
    scores(last, 2, sb_sc, mb_sc)
    finish(last - 1, sa_sc, ma_sc)
    finish(last, sb_sc, mb_sc)


def _na_attn(qk, vt, rpb_rows, l):
    s = qk.shape[0]
    return pl.pallas_call(
        _na_attn_kernel,
        grid=(NA_HEADS,),
        in_specs=[pl.BlockSpec((s, NA_HEAD_DIM), lambda h: (0, h)),
                  pl.BlockSpec((s, NA_HEAD_DIM), lambda h: (0, NA_HEADS + h)),
                  pl.BlockSpec((V_ROWS, s), lambda h: (h, 0)),
                  _layer_spec((1, 2 * NA_KH, LANES), l, lambda h: (h, 0, 0))],
        out_specs=pl.BlockSpec((s, NA_HEAD_DIM), lambda h: (0, h)),
        out_shape=jax.ShapeDtypeStruct((s, NA_HEADS * NA_HEAD_DIM), BF16),
        scratch_shapes=[pltpu.VMEM((2, 2 * NA_KH - 1, GRID_W, LANES), F32), pltpu.VMEM((3, NA_TK, NA_TQ), F32),
                        pltpu.VMEM((NA_TK, NA_TQ), F32), pltpu.VMEM((NA_TK, NA_TQ), F32),
                        pltpu.VMEM((2 * F32_SUBLANES, NA_TQ), F32)],
        compiler_params=_params(1),
        name="na_attn",
    )(qk, qk, vt, rpb_rows)


def _merge_kernel(ya_ref, yb_ref, u_ref, x_ref, woa_ref, wob_ref, wo_ref, wg_ref, o_ref):
    g = lax.dot_general(u_ref[...], wg_ref[...], NT_DIMS, preferred_element_type=F32)
    a = jnp.dot(ya_ref[...], woa_ref[...], preferred_element_type=F32)
    b = jnp.dot(yb_ref[...], wob_ref[...], preferred_element_type=F32)
    merged = (jax.nn.sigmoid(g[:, :D_MODEL]) * a + jax.nn.sigmoid(g[:, D_MODEL:]) * b).astype(BF16)
    o_ref[...] = x_ref[...] + jnp.dot(merged, wo_ref[...], preferred_element_type=F32)


def _merge(ya, yb, u, x, woa, wob, wo, wg_t, tm=256):
    s = ya.shape[0]
    row = lambda w: pl.BlockSpec((tm, w), lambda i: (i, 0))
    full = lambda a: pl.BlockSpec(a.shape, lambda i: (0, 0), pipeline_mode=pl.Buffered(1))
    return pl.pallas_call(
        _merge_kernel,
        grid=(s // tm,),
        in_specs=[row(ya.shape[1]), row(yb.shape[1]), row(D_MODEL), row(D_MODEL),
                  full(woa), full(wob), full(wo), full(wg_t)],
        out_specs=row(D_MODEL),
        out_shape=jax.ShapeDtypeStruct((s, D_MODEL), F32),
        compiler_params=_params(1),
        name="merge_out",
    )(ya, yb, u, x, woa, wob, wo, wg_t)


def _ffn_kernel(x_ref, g_ref, w1_ref, w2_ref, gf_ref, o_ref, u_sc, *, final_norm):
    f = pl.program_id(1)

    @pl.when(f == 0)
    def _init():
        x = x_ref[...]
        u_sc[...] = _rms(x, g_ref[...]).astype(BF16)
        o_ref[...] = x

    h = jnp.dot(u_sc[...], w1_ref[...], preferred_element_type=F32)
    a = jnp.square(jnp.maximum(h, 0.0)).astype(BF16)
    o_ref[...] += jnp.dot(a, w2_ref[...], preferred_element_type=F32)

    if final_norm:
        @pl.when(f == pl.num_programs(1) - 1)
        def _final():
            o_ref[...] = _rms(o_ref[...], gf_ref[...])


def _ffn(x, g, w1, w2, l, g_final, final_norm, tm=512, tf=1024):
    s, d = x.shape
    return pl.pallas_call(
        functools.partial(_ffn_kernel, final_norm=final_norm),
        grid=(s // tm, w1.shape[1] // tf),
        in_specs=[pl.BlockSpec((tm, d), lambda i, f: (i, 0)),
                  _layer_spec((1, d), l, lambda i, f: (0, 0)),
                  pl.BlockSpec((d, tf), lambda i, f: (0, f)),
                  pl.BlockSpec((tf, d), lambda i, f: (f, 0)),
                  pl.BlockSpec((1, d), lambda i, f: (0, 0))],
        out_specs=pl.BlockSpec((tm, d), lambda i, f: (i, 0)),
        out_shape=jax.ShapeDtypeStruct((s, d), F32),
        scratch_shapes=[pltpu.VMEM((tm, d), BF16)],
        compiler_params=_params(2),
        name="ffn",
    )(x, g, w1, w2, g_final)


def _rope_tables(s):
    pos = jnp.arange(s, dtype=F32)
    inv_freq = 1.0 / (ROPE_THETA ** (jnp.arange(0, QK_ROPE, 2, dtype=F32) / QK_ROPE))
    ang = pos[:, None] * inv_freq[None, :]
    cos, sin = jnp.cos(ang), jnp.sin(ang)
    half = QK_ROPE // 2
    z = lambda w: jnp.zeros((s, w), F32)
    cos_t = jnp.concatenate([cos, cos, z(LANES - QK_ROPE)], axis=1)
    sl_t = jnp.concatenate([-sin, z(LANES - half)], axis=1)
    sr_t = jnp.concatenate([z(half), sin, z(LANES - QK_ROPE)], axis=1)
    return cos_t, sl_t, sr_t


def kernel(x, norm_mix, w_in, norm_qa, w_uq, norm_kva, w_ukv, rpb, w_o_mla, w_o_na, w_out,
           norm_mlp, w_ff1, w_ff2, norm_final):
    b, s, d = x.shape
    assert (b, s, d) == (1, SEQ, D_MODEL)
    depth = w_in.shape[0]
    cos_t, sl_t, sr_t = _rope_tables(s)

    w_in_t = w_in.transpose(0, 2, 1)
    uq = w_uq.reshape(depth, Q_LORA, MLA_HEADS, QK_NOPE + QK_ROPE)
    w_qn = uq[..., :QK_NOPE].reshape(depth, Q_LORA, MLA_HEADS * QK_NOPE).astype(BF16)
    w_qp = jnp.pad(uq[..., QK_NOPE:], ((0, 0), (0, 0), (0, 0), (0, LANES - QK_ROPE)))
    w_qp = w_qp.reshape(depth, Q_LORA, MLA_HEADS * LANES).astype(BF16)
    ukv = w_ukv.reshape(depth, KV_LORA, MLA_HEADS, QK_NOPE + V_HEAD)
    w_kn = ukv[..., :QK_NOPE].reshape(depth, KV_LORA, MLA_HEADS * QK_NOPE).astype(BF16)
    w_vt = ukv[..., QK_NOPE:].reshape(depth, KV_LORA, MLA_HEADS * V_HEAD).transpose(0, 2, 1).astype(BF16)
    g_qa = norm_qa.reshape(depth, 1, Q_LORA)
    g_kva = norm_kva.reshape(depth, 1, KV_LORA)
    g_mlp = norm_mlp.reshape(depth, 1, d)
    g_mix = norm_mix.reshape(depth, 1, d)
    g_final = norm_final.reshape(1, d)
    rpb_rows = _na_rpb_rows(rpb)

    xs = x.reshape(s, d)
    for l in range(depth):
        u, c = _norm_proj(xs, g_mix, w_in_t, l)
        qk_na = _mm_nt(u, w_in_t, l, ROW_QK, 2 * NA_W, BF16, PROJ_TM, PROJ_TN, name="proj_na_qk")
        vt_na = _proj_vt(u, w_in_t, l)
        q, k, vt = _mla_up(c, g_qa, g_kva, w_qn, w_qp, w_kn, w_vt, l, cos_t, sl_t, sr_t)
        casts = tuple((w, 0, w.shape[1]) for w in (w_o_mla, w_o_na, w_out, w_ff1, w_ff2))
        casts += ((w_in_t, ROW_G, 2 * D_MODEL),)
        y_a, w_oa, w_ob, w_o, w_1, w_2, w_g_t = _mla_attn(q, k, vt, casts, l)
        y_b = _na_attn(qk_na, vt_na, rpb_rows, l)
        xs = _merge(y_a, y_b, u, xs, w_oa, w_ob, w_o, w_g_t)
        xs = _ffn(xs, g_mlp, w_1, w_2, l, g_final, final_norm=(l == depth - 1))
    return xs.reshape(b, s, d)
```

```python
import functools

import jax
import jax.numpy as jnp
import numpy as np
from jax import lax
from jax.experimental import pallas as pl
from jax.experimental.pallas import tpu as pltpu

D_MODEL = 2048
SEQ = 8192
GRID_W = 64
ROWS = SEQ // GRID_W
MLA_HEADS = 8
Q_LORA = 512
KV_LORA = 512
QK_NOPE = 128
QK_ROPE = 64
V_HEAD = 128
ROPE_THETA = 10000.0
NA_HEADS = 8
NA_HEAD_DIM = 128
NA_KH = 8
NA_KW = 16
D_FF = 4 * D_MODEL
EPS = 1e-6
LOG2E = float(np.log2(np.e))

LANES = 128
MLA_QK_PAD = 2 * LANES
F32_SUBLANES = 8
BF16_SUBLANES = 16
V_ROWS = V_HEAD + BF16_SUBLANES
MLA_REF_KEYS = 128
MLA_MAX_GAP = 60.0
NA_Q_ROWS = 8
NA_WIN_ROWS = 2 * NA_KH
NA_TQ = NA_Q_ROWS * GRID_W
NA_TK = NA_WIN_ROWS * GRID_W
NA_BLOCKS = SEQ // NA_TQ
NA_STRIP = NA_KH * GRID_W
VMEM_LIMIT = 56 * 1024 * 1024

BF16 = jnp.bfloat16
F32 = jnp.float32
NT_DIMS = (((1,), (1,)), ((), ()))


def _params(n_axes):
    return pltpu.CompilerParams(
        dimension_semantics=("arbitrary",) * n_axes, vmem_limit_bytes=VMEM_LIMIT)


def _rms(x, g):
    return x * lax.rsqrt(jnp.mean(x * x, axis=-1, keepdims=True) + EPS) * g


def _layer_spec(shape, l, index_map):
    return pl.BlockSpec((None,) + tuple(shape), lambda *ids: (l,) + tuple(index_map(*ids)))


def _store_vt_with_ones(vt_ref, vt, heads):
    for h in range(heads):
        vt_ref[h * V_ROWS:h * V_ROWS + V_HEAD, :] = vt[h * V_HEAD:(h + 1) * V_HEAD, :]
        vt_ref[h * V_ROWS + V_HEAD:(h + 1) * V_ROWS, :] = jnp.ones((V_ROWS - V_HEAD, vt.shape[1]), BF16)


C_END = Q_LORA + KV_LORA + QK_ROPE
C_PAD = C_END + LANES - QK_ROPE
NA_W = NA_HEADS * NA_HEAD_DIM
ROW_QK = C_END
ROW_V = ROW_QK + 2 * NA_W
ROW_G = ROW_V + NA_W
PROJ_TM = 1024
PROJ_TN = 1024


def _wt_spec(rows, k, l, row_map):
    return pl.BlockSpec((pl.Element(1), pl.Element(rows), pl.Element(k)),
                        lambda *ids: (l, row_map(*ids), 0))


def _mm_nt_kernel(a_ref, wt_ref, o_ref, wb_sc):
    @pl.when(pl.program_id(1) == 0)
    def _cast():
        wb_sc[...] = wt_ref[0].astype(BF16)

    o_ref[...] = lax.dot_general(a_ref[...], wb_sc[...], NT_DIMS,
                                 preferred_element_type=F32).astype(o_ref.dtype)


def _mm_nt(a, wt, l, row0, n, out_dtype, tm, tn, name):
    m, k = a.shape
    return pl.pallas_call(
        _mm_nt_kernel,
        grid=(n // tn, m // tm),
        in_specs=[pl.BlockSpec((tm, k), lambda j, i: (i, 0)),
                  _wt_spec(tn, k, l, lambda j, i: pl.multiple_of(row0 + j * tn, QK_ROPE))],
        out_specs=pl.BlockSpec((tm, tn), lambda j, i: (i, j)),
        out_shape=jax.ShapeDtypeStruct((m, n), out_dtype),
        scratch_shapes=[pltpu.VMEM((tn, k), BF16)],
        compiler_params=_params(2),
        name=name,
    )(a, wt)


def _norm_proj_kernel(x_ref, g_ref, wt_ref, u_ref, c_ref, wb_sc):
    @pl.when(pl.program_id(0) == 0)
    def _cast():
        wb_sc[...] = wt_ref[0].astype(BF16)

    u = _rms(x_ref[...], g_ref[...]).astype(BF16)
    u_ref[...] = u
    c_ref[...] = lax.dot_general(u, wb_sc[...], NT_DIMS, preferred_element_type=F32)


def _norm_proj(x, g, wt, l, tm=512):
    s, d = x.shape
    return pl.pallas_call(
        _norm_proj_kernel,
        grid=(s // tm,),
        in_specs=[pl.BlockSpec((tm, d), lambda i: (i, 0)),
                  _layer_spec((1, d), l, lambda i: (0, 0)),
                  _wt_spec(C_PAD, d, l, lambda i: 0)],
        out_specs=[pl.BlockSpec((tm, d), lambda i: (i, 0)), pl.BlockSpec((tm, C_PAD), lambda i: (i, 0))],
        out_shape=[jax.ShapeDtypeStruct((s, d), BF16), jax.ShapeDtypeStruct((s, C_PAD), F32)],
        scratch_shapes=[pltpu.VMEM((C_PAD, d), BF16)],
        compiler_params=_params(1),
        name="norm_proj_c",
    )(x, g, wt)


def _proj_vt_kernel(u_ref, wt_ref, vt_ref, wb_sc):
    @pl.when(pl.program_id(0) == 0)
    def _cast():
        wb_sc[...] = wt_ref[0].astype(BF16)

    vt = lax.dot_general(wb_sc[...], u_ref[...], NT_DIMS, preferred_element_type=F32).astype(BF16)
    _store_vt_with_ones(vt_ref, vt, NA_HEADS)


def _proj_vt(u, wt, l, tm=1024):
    s, k = u.shape
    return pl.pallas_call(
        _proj_vt_kernel,
        grid=(s // tm,),
        in_specs=[pl.BlockSpec((tm, k), lambda i: (i, 0)),
                  _wt_spec(NA_W, k, l, lambda i: ROW_V)],
        out_specs=pl.BlockSpec((NA_HEADS * V_ROWS, tm), lambda i: (0, i)),
        out_shape=jax.ShapeDtypeStruct((NA_HEADS * V_ROWS, s), BF16),
        scratch_shapes=[pltpu.VMEM((NA_W, k), BF16)],
        compiler_params=_params(1),
        name="proj_na_vt",
    )(u, wt)


def _rope128(x, c, sl, sr):
    return x * c + pltpu.roll(x, LANES - QK_ROPE // 2, 1) * sl + pltpu.roll(x, QK_ROPE // 2, 1) * sr


def _mla_up_kernel(c_ref, gq_ref, gkv_ref, wqn_ref, wqp_ref, wkn_ref, wvt_ref,
                   cos_ref, sl_ref, sr_ref, q_ref, k_ref, vt_ref):
    scale = (QK_NOPE + QK_ROPE) ** -0.5 * LOG2E
    cq = _rms(c_ref[:, :Q_LORA], gq_ref[...]).astype(BF16)
    ckv = _rms(c_ref[:, Q_LORA:Q_LORA + KV_LORA], gkv_ref[...]).astype(BF16)
    kpe = c_ref[:, Q_LORA + KV_LORA:]
    kpe = jnp.where(lax.broadcasted_iota(jnp.int32, kpe.shape, 1) < QK_ROPE, kpe, 0.0)
    cos, sl, sr = cos_ref[...], sl_ref[...], sr_ref[...]
    kpe_rot = _rope128(kpe, cos, sl, sr).astype(BF16)
    qn = jnp.dot(cq, wqn_ref[...], preferred_element_type=F32) * scale
    qp = jnp.dot(cq, wqp_ref[...], preferred_element_type=F32)
    kn = jnp.dot(ckv, wkn_ref[...], preferred_element_type=F32)
    vt = lax.dot_general(wvt_ref[...], ckv, NT_DIMS, preferred_element_type=F32).astype(BF16)
    _store_vt_with_ones(vt_ref, vt, MLA_HEADS)
    for h in range(MLA_HEADS):
        lo = h * MLA_QK_PAD
        hs = slice(h * LANES, (h + 1) * LANES)
        q_ref[:, lo:lo + LANES] = qn[:, hs].astype(BF16)
        q_ref[:, lo + LANES:lo + 2 * LANES] = (_rope128(qp[:, hs], cos, sl, sr) * scale).astype(BF16)
        k_ref[:, lo:lo + LANES] = kn[:, hs].astype(BF16)
        k_ref[:, lo + LANES:lo + 2 * LANES] = kpe_rot


def _mla_up(c, gq, gkv, wqn, wqp, wkn, wvt, l, cos_t, sl_t, sr_t, tm=1024):
    s = c.shape[0]
    row = lambda w: pl.BlockSpec((tm, w), lambda i: (i, 0))
    full = lambda a: _layer_spec(a.shape[1:], l, lambda i: (0, 0))
    hw = MLA_HEADS * V_ROWS
    return pl.pallas_call(
        _mla_up_kernel,
        grid=(s // tm,),
        in_specs=[row(c.shape[1]), full(gq), full(gkv), full(wqn), full(wqp), full(wkn), full(wvt),
                  row(LANES), row(LANES), row(LANES)],
        out_specs=[row(MLA_HEADS * MLA_QK_PAD), row(MLA_HEADS * MLA_QK_PAD),
                   pl.BlockSpec((hw, tm), lambda i: (0, i))],
        out_shape=[jax.ShapeDtypeStruct((s, MLA_HEADS * MLA_QK_PAD), BF16),
                   jax.ShapeDtypeStruct((s, MLA_HEADS * MLA_QK_PAD), BF16),
                   jax.ShapeDtypeStruct((hw, s), BF16)],
        compiler_params=_params(1),
        name="mla_up",
    )(c, gq, gkv, wqn, wqp, wkn, wvt, cos_t, sl_t, sr_t)


def _cast_plan(casts, l, steps, step_index):
    def src_spec(w, row0, rows):
        share = rows // steps
        assert share % BF16_SUBLANES == 0 and share * steps == rows
        return _wt_spec(share, w.shape[2], l,
                        lambda *ids: pl.multiple_of(row0 + step_index(*ids) * share, BF16_SUBLANES))

    dst_spec = lambda w, rows: pl.BlockSpec((rows // steps, w.shape[2]), lambda *ids: (step_index(*ids), 0))
    return ([src_spec(w, row0, rows) for w, row0, rows in casts],
            [dst_spec(w, rows) for w, _, rows in casts],
            [jax.ShapeDtypeStruct((rows, w.shape[2]), BF16) for w, _, rows in casts])


def _convert(cast_src, cast_dst):
    for src, dst in zip(cast_src, cast_dst):
        dst[...] = src[0].astype(BF16)


def _mla_attn_kernel(*refs, tk, n_cast):
    q_ref, k_ref, vt_ref = refs[:3]
    cast_src = refs[3:3 + n_cast]
    o_ref = refs[3 + n_cast]
    cast_dst = refs[4 + n_cast:4 + 2 * n_cast]
    stat_sc, acc_sc = refs[4 + 2 * n_cast:]
    r_sc, mx_sc, m_sc = (stat_sc.at[pl.ds(j * F32_SUBLANES, 1), :] for j in range(3))

    _convert(cast_src, cast_dst)

    q = q_ref[...]
    n_chunks = k_ref.shape[0] // tk
    assert n_chunks % 2 == 0

    def scores(c):
        k = k_ref[pl.ds(pl.multiple_of(c * tk, tk), tk), :]
        return lax.dot_general(k, q, NT_DIMS, preferred_element_type=F32)

    def values(c, pt):
        vt = vt_ref[:, pl.ds(pl.multiple_of(c * tk, tk), tk)]
        return jnp.dot(vt, pt.astype(BF16), preferred_element_type=F32)

    def write_out():
        o_ref[...] = (acc_sc[:V_HEAD, :] / acc_sc[V_HEAD:V_HEAD + 1, :]).T.astype(o_ref.dtype)

    s_ref = lax.dot_general(k_ref[:MLA_REF_KEYS, :], q, NT_DIMS, preferred_element_type=F32)
    r = jnp.max(s_ref, axis=0, keepdims=True)
    r_sc[...] = r
    mx_sc[...] = r
    acc_sc[...] = jnp.zeros(acc_sc.shape, F32)

    def step(c):
        st = scores(c)
        mx_sc[...] = jnp.maximum(mx_sc[...], jnp.max(st, axis=0, keepdims=True))
        acc_sc[...] += values(c, jnp.exp2(st - r_sc[...]))

    def single_pass_body(j, carry):
        step(2 * j)
        step(2 * j + 1)
        return carry

    lax.fori_loop(0, n_chunks // 2, single_pass_body, 0)
    in_range = jnp.max(mx_sc[...] - r_sc[...]) <= MLA_MAX_GAP

    @pl.when(in_range)
    def _accept():
        write_out()

    @pl.when(jnp.logical_not(in_range))
    def _online_softmax():
        m_sc[...] = jnp.full(m_sc.shape, -jnp.inf, F32)
        acc_sc[...] = jnp.zeros(acc_sc.shape, F32)

        def body(c, carry):
            st = scores(c)
            m_prev = m_sc[...]
            m_new = jnp.maximum(m_prev, jnp.max(st, axis=0, keepdims=True))
            acc_sc[...] = jnp.exp2(m_prev - m_new) * acc_sc[...] + values(c, jnp.exp2(st - m_new))
            m_sc[...] = m_new
            return carry

        lax.fori_loop(0, n_chunks, body, 0)
        write_out()


def _mla_attn(q, k, vt, casts=(), l=0, tq=2048, tk=2048):
    s = q.shape[0]
    nq = s // tq
    cast_in, cast_out, cast_shape = _cast_plan(casts, l, MLA_HEADS * nq, lambda h, i: h * nq + i)
    return pl.pallas_call(
        functools.partial(_mla_attn_kernel, tk=tk, n_cast=len(casts)),
        grid=(MLA_HEADS, nq),
        in_specs=[pl.BlockSpec((tq, MLA_QK_PAD), lambda h, i: (i, h)),
                  pl.BlockSpec((s, MLA_QK_PAD), lambda h, i: (0, h)),
                  pl.BlockSpec((V_ROWS, s), lambda h, i: (h, 0))]
                 + cast_in,
        out_specs=[pl.BlockSpec((tq, V_HEAD), lambda h, i: (i, h))]
                  + cast_out,
        out_shape=[jax.ShapeDtypeStruct((s, MLA_HEADS * V_HEAD), BF16)]
                  + cast_shape,
        scratch_shapes=[pltpu.VMEM((3 * F32_SUBLANES, tq), F32), pltpu.VMEM((V_ROWS, tq), F32)],
        compiler_params=_params(2),
        name="mla_attn",
    )(q, k, vt, *[w for w, _, _ in casts])


def _na_row_start(r):
    return min(max(r - NA_KH // 2, 0), ROWS - NA_KH)


def _na_win_start(blk):
    return min(max(blk * NA_Q_ROWS - NA_KH // 2, 0), ROWS - NA_WIN_ROWS)


def _na_row_window(blk, r_in_blk):
    r = blk * NA_Q_ROWS + r_in_blk
    return _na_row_start(r) - _na_win_start(blk), _na_row_start(r) - r + (NA_KH - 1)


def _na_rpb_rows(rpb):
    n_dy, n_dx = 2 * NA_KH - 1, 2 * NA_KW - 1
    return jnp.pad(rpb.astype(F32)[..., ::-1], ((0, 0), (0, 0), (0, 2 * NA_KH - n_dy), (0, LANES - n_dx)))


def _na_attn_kernel(q_ref, k_ref, vt_ref, r_ref, o_ref, toe_sc, bias_sc, sa_sc, sb_sc, cm_sc):
    last = NA_BLOCKS - 1
    ma_sc, mb_sc = (cm_sc.at[pl.ds(j * F32_SUBLANES, 1), :] for j in range(2))

    kc = lax.broadcasted_iota(jnp.int32, (GRID_W, LANES), 0)
    lane = lax.broadcasted_iota(jnp.int32, (GRID_W, LANES), 1)
    for half in range(2):
        qc = lane - half * GRID_W
        col_start = jnp.clip(qc - NA_KW // 2, 0, GRID_W - NA_KW)
        ok = (qc >= 0) & (qc < GRID_W) & (kc >= col_start) & (kc < col_start + NA_KW)
        shift = (LANES - (NA_KW - 1) + half * GRID_W) % LANES
        for dy in range(2 * NA_KH - 1):
            row = jnp.broadcast_to(r_ref[0, dy:dy + 1, :], (GRID_W, LANES))
            toe = pltpu.roll(row, shift, 1, stride=1, stride_axis=0)
            toe_sc[half, dy] = jnp.where(ok, toe * LOG2E, -jnp.inf)

    bias_sc[...] = jnp.full(bias_sc.shape, -jnp.inf, F32)
    for var, blk in enumerate((0, 1, last)):
        for pair in range(NA_Q_ROWS // 2):
            (off_e, dy_e), (off_o, dy_o) = _na_row_window(blk, 2 * pair), _na_row_window(blk, 2 * pair + 1)
            for b in range(min(off_e, off_o), max(off_e, off_o) + NA_KH):
                parts = []
                if 0 <= b - off_e < NA_KH:
                    parts.append(toe_sc[0, dy_e + b - off_e])
                if 0 <= b - off_o < NA_KH:
                    parts.append(toe_sc[1, dy_o + b - off_o])
                blkv = parts[0] if len(parts) == 1 else jnp.maximum(parts[0], parts[1])
                bias_sc[var, b * GRID_W:(b + 1) * GRID_W, pair * LANES:(pair + 1) * LANES] = blkv

    def win_start(blk):
        start = jnp.clip(blk * NA_TQ - (NA_KH // 2) * GRID_W, 0, SEQ - NA_TK)
        return pl.multiple_of(start, (NA_KH // 2) * GRID_W)

    def scores(blk, var, s_sc, cm_sc):
        q = q_ref[pl.ds(pl.multiple_of(blk * NA_TQ, NA_TQ), NA_TQ), :]
        k = k_ref[pl.ds(win_start(blk), NA_TK), :]
        st = lax.dot_general(k, q, NT_DIMS, preferred_element_type=F32)
        st = st * (NA_HEAD_DIM ** -0.5 * LOG2E) + bias_sc[var]
        s_sc[...] = st
        cm_sc[...] = jnp.max(st, axis=0, keepdims=True)

    def finish(blk, s_sc, cm_sc):
        vt = vt_ref[:, pl.ds(win_start(blk), NA_TK)]
        pt = jnp.exp2(s_sc[...] - cm_sc[...])
        acc = jnp.dot(vt, pt.astype(BF16), preferred_element_type=F32)
        o = (acc[:V_HEAD, :] / acc[V_HEAD:V_HEAD + 1, :]).T
        o_ref[pl.ds(pl.multiple_of(blk * NA_TQ, NA_TQ), NA_TQ), :] = o.astype(o_ref.dtype)

    assert NA_BLOCKS % 2 == 0
    scores(0, 0, sa_sc, ma_sc)

    def body(j, carry):
        b = 2 * j
        scores(b + 1, 1, sb_sc, mb_sc)
        finish(b, sa_sc, ma_sc)
        scores(b + 2, 1, sa_sc, ma_sc)
        finish(b + 1, sb_sc, mb_sc)
        return carry

    lax.fori_loop(0, NA_BLOCKS // 2 - 1, body, 0)
    scores(last, 2, sb_sc, mb_sc)
    finish(last - 1, sa_sc, ma_sc)
    finish(last, sb_sc, mb_sc)


def _na_attn(qk, vt, rpb_rows, l):
    s = qk.shape[0]
    return pl.pallas_call(
        _na_attn_kernel,
        grid=(NA_HEADS,),
        in_specs=[pl.BlockSpec((s, NA_HEAD_DIM), lambda h: (0, h)),
                  pl.BlockSpec((s, NA_HEAD_DIM), lambda h: (0, NA_HEADS + h)),
                  pl.BlockSpec((V_ROWS, s), lambda h: (h, 0)),
                  _layer_spec((1, 2 * NA_KH, LANES), l, lambda h: (h, 0, 0))],
        out_specs=pl.BlockSpec((s, NA_HEAD_DIM), lambda h: (0, h)),
        out_shape=jax.ShapeDtypeStruct((s, NA_HEADS * NA_HEAD_DIM), BF16),
        scratch_shapes=[pltpu.VMEM((2, 2 * NA_KH - 1, GRID_W, LANES), F32), pltpu.VMEM((3, NA_TK, NA_TQ), F32),
                        pltpu.VMEM((NA_TK, NA_TQ), F32), pltpu.VMEM((NA_TK, NA_TQ), F32),
                        pltpu.VMEM((2 * F32_SUBLANES, NA_TQ), F32)],
        compiler_params=_params(1),
        name="na_attn",
    )(qk, qk, vt, rpb_rows)


def _merge_kernel(ya_ref, yb_ref, u_ref, x_ref, woa_ref, wob_ref, wo_ref, wg_ref, o_ref):
    g = lax.dot_general(u_ref[...], wg_ref[...], NT_DIMS, preferred_element_type=F32)
    a = jnp.dot(ya_ref[...], woa_ref[...], preferred_element_type=F32)
    b = jnp.dot(yb_ref[...], wob_ref[...], preferred_element_type=F32)
    merged = (jax.nn.sigmoid(g[:, :D_MODEL]) * a + jax.nn.sigmoid(g[:, D_MODEL:]) * b).astype(BF16)
    o_ref[...] = x_ref[...] + jnp.dot(merged, wo_ref[...], preferred_element_type=F32)


def _merge(ya, yb, u, x, woa, wob, wo, wg_t, tm=256):
    s = ya.shape[0]
    row = lambda w: pl.BlockSpec((tm, w), lambda i: (i, 0))
    full = lambda a: pl.BlockSpec(a.shape, lambda i: (0, 0), pipeline_mode=pl.Buffered(1))
    return pl.pallas_call(
        _merge_kernel,
        grid=(s // tm,),
        in_specs=[row(ya.shape[1]), row(yb.shape[1]), row(D_MODEL), row(D_MODEL),
                  full(woa), full(wob), full(wo), full(wg_t)],
        out_specs=row(D_MODEL),
        out_shape=jax.ShapeDtypeStruct((s, D_MODEL), F32),
        compiler_params=_params(1),
        name="merge_out",
    )(ya, yb, u, x, woa, wob, wo, wg_t)


def _ffn_kernel(*refs, final_norm, n_cast):
    x_ref, g_ref, w1_ref, w2_ref, gf_ref = refs[:5]
    cast_src = refs[5:5 + n_cast]
    o_ref = refs[5 + n_cast]
    cast_dst = refs[6 + n_cast:6 + 2 * n_cast]
    u_sc = refs[6 + 2 * n_cast]
    f = pl.program_id(1)

    @pl.when(f == 0)
    def _init():
        x = x_ref[...]
        u_sc[...] = _rms(x, g_ref[...]).astype(BF16)
        o_ref[...] = x

    h = jnp.dot(u_sc[...], w1_ref[...], preferred_element_type=F32)
    a = jnp.square(jnp.maximum(h, 0.0)).astype(BF16)
    o_ref[...] += jnp.dot(a, w2_ref[...], preferred_element_type=F32)
    _convert(cast_src, cast_dst)

    if final_norm:
        @pl.when(f == pl.num_programs(1) - 1)
        def _final():
            o_ref[...] = _rms(o_ref[...], gf_ref[...])


def _ffn(x, g, w1, w2, l, g_final, final_norm, casts=(), cast_layer=0, tm=512, tf=1024):
    s, d = x.shape
    nf = w1.shape[1] // tf
    cast_in, cast_out, cast_shape = _cast_plan(casts, cast_layer, (s // tm) * nf, lambda i, f: i * nf + f)
    return pl.pallas_call(
        functools.partial(_ffn_kernel, final_norm=final_norm, n_cast=len(casts)),
        grid=(s // tm, nf),
        in_specs=[pl.BlockSpec((tm, d), lambda i, f: (i, 0)),
                  _layer_spec((1, d), l, lambda i, f: (0, 0)),
                  pl.BlockSpec((d, tf), lambda i, f: (0, f)),
                  pl.BlockSpec((tf, d), lambda i, f: (f, 0)),
                  pl.BlockSpec((1, d), lambda i, f: (0, 0))] + cast_in,
        out_specs=[pl.BlockSpec((tm, d), lambda i, f: (i, 0))] + cast_out,
        out_shape=[jax.ShapeDtypeStruct((s, d), F32)] + cast_shape,
        scratch_shapes=[pltpu.VMEM((tm, d), BF16)],
        compiler_params=_params(2),
        name="ffn",
    )(x, g, w1, w2, g_final, *[w for w, _, _ in casts])


def _rope_tables(s):
    pos = jnp.arange(s, dtype=F32)
    inv_freq = 1.0 / (ROPE_THETA ** (jnp.arange(0, QK_ROPE, 2, dtype=F32) / QK_ROPE))
    ang = pos[:, None] * inv_freq[None, :]
    cos, sin = jnp.cos(ang), jnp.sin(ang)
    half = QK_ROPE // 2
    z = lambda w: jnp.zeros((s, w), F32)
    cos_t = jnp.concatenate([cos, cos, z(LANES - QK_ROPE)], axis=1)
    sl_t = jnp.concatenate([-sin, z(LANES - half)], axis=1)
    sr_t = jnp.concatenate([z(half), sin, z(LANES - QK_ROPE)], axis=1)
    return cos_t, sl_t, sr_t


def kernel(x, norm_mix, w_in, norm_qa, w_uq, norm_kva, w_ukv, rpb, w_o_mla, w_o_na, w_out,
           norm_mlp, w_ff1, w_ff2, norm_final):
    b, s, d = x.shape
    assert (b, s, d) == (1, SEQ, D_MODEL)
    depth = w_in.shape[0]
    cos_t, sl_t, sr_t = _rope_tables(s)

    w_in_t = w_in.transpose(0, 2, 1)
    uq = w_uq.reshape(depth, Q_LORA, MLA_HEADS, QK_NOPE + QK_ROPE)
    w_qn = uq[..., :QK_NOPE].reshape(depth, Q_LORA, MLA_HEADS * QK_NOPE).astype(BF16)
    w_qp = jnp.pad(uq[..., QK_NOPE:], ((0, 0), (0, 0), (0, 0), (0, LANES - QK_ROPE)))
    w_qp = w_qp.reshape(depth, Q_LORA, MLA_HEADS * LANES).astype(BF16)
    ukv = w_ukv.reshape(depth, KV_LORA, MLA_HEADS, QK_NOPE + V_HEAD)
    w_kn = ukv[..., :QK_NOPE].reshape(depth, KV_LORA, MLA_HEADS * QK_NOPE).astype(BF16)
    w_vt = ukv[..., QK_NOPE:].reshape(depth, KV_LORA, MLA_HEADS * V_HEAD).transpose(0, 2, 1).astype(BF16)
    g_qa = norm_qa.reshape(depth, 1, Q_LORA)
    g_kva = norm_kva.reshape(depth, 1, KV_LORA)
    g_mlp = norm_mlp.reshape(depth, 1, d)
    g_mix = norm_mix.reshape(depth, 1, d)
    g_final = norm_final.reshape(1, d)
    rpb_rows = _na_rpb_rows(rpb)

    whole = lambda w: (w, 0, w.shape[1])
    small = (whole(w_o_mla), whole(w_o_na), whole(w_out))
    large = (whole(w_ff1), whole(w_ff2), (w_in_t, ROW_G, 2 * D_MODEL))

    xs = x.reshape(s, d)
    w_1 = w_2 = w_g_t = None
    for l in range(depth):
        u, c = _norm_proj(xs, g_mix, w_in_t, l)
        qk_na = _mm_nt(u, w_in_t, l, ROW_QK, 2 * NA_W, BF16, PROJ_TM, PROJ_TN, name="proj_na_qk")
        vt_na = _proj_vt(u, w_in_t, l)
        q, k, vt = _mla_up(c, g_qa, g_kva, w_qn, w_qp, w_kn, w_vt, l, cos_t, sl_t, sr_t)
        if l == 0:
            y_a, w_oa, w_ob, w_o, w_1, w_2, w_g_t = _mla_attn(q, k, vt, small + large, l)
        else:
            y_a, w_oa, w_ob, w_o = _mla_attn(q, k, vt, small, l)
        y_b = _na_attn(qk_na, vt_na, rpb_rows, l)
        xs = _merge(y_a, y_b, u, xs, w_oa, w_ob, w_o, w_g_t)
        last = l == depth - 1
        xs, *nxt = _ffn(xs, g_mlp, w_1, w_2, l, g_final, final_norm=last,
                        casts=() if last else large, cast_layer=l + 1)
        if not last:
            w_1, w_2, w_g_t = nxt
    return xs.reshape(b, s, d)
```

```python
import functools

import jax
import jax.numpy as jnp
import numpy as np
from jax import lax
from jax.experimental import pallas as pl
from jax.experimental.pallas import tpu as pltpu

D_MODEL = 2048
SEQ = 8192
GRID_W = 64
ROWS = SEQ // GRID_W
MLA_HEADS = 8
Q_LORA = 512
KV_LORA = 512
QK_NOPE = 128
QK_ROPE = 64
V_HEAD = 128
ROPE_THETA = 10000.0
NA_HEADS = 8
NA_HEAD_DIM = 128
NA_KH = 8
NA_KW = 16
D_FF = 4 * D_MODEL
EPS = 1e-6
LOG2E = float(np.log2(np.e))

LANES = 128
MLA_QK_PAD = 2 * LANES
F32_SUBLANES = 8
BF16_SUBLANES = 16
V_ROWS = V_HEAD + BF16_SUBLANES
MLA_REF_KEYS = 128
MLA_MAX_GAP = 60.0
NA_Q_ROWS = 8
NA_WIN_ROWS = 2 * NA_KH
NA_TQ = NA_Q_ROWS * GRID_W
NA_TK = NA_WIN_ROWS * GRID_W
NA_BLOCKS = SEQ // NA_TQ
NA_STRIP = NA_KH * GRID_W
VMEM_LIMIT = 56 * 1024 * 1024

BF16 = jnp.bfloat16
F32 = jnp.float32
NT_DIMS = (((1,), (1,)), ((), ()))


def _params(n_axes):
    return pltpu.CompilerParams(
        dimension_semantics=("arbitrary",) * n_axes, vmem_limit_bytes=VMEM_LIMIT)


def _rms(x, g):
    return x * lax.rsqrt(jnp.mean(x * x, axis=-1, keepdims=True) + EPS) * g


def _layer_spec(shape, l, index_map):
    return pl.BlockSpec((None,) + tuple(shape), lambda *ids: (l,) + tuple(index_map(*ids)))


def _store_vt_with_ones(vt_ref, vt, heads):
    for h in range(heads):
        vt_ref[h * V_ROWS:h * V_ROWS + V_HEAD, :] = vt[h * V_HEAD:(h + 1) * V_HEAD, :]
        vt_ref[h * V_ROWS + V_HEAD:(h + 1) * V_ROWS, :] = jnp.ones((V_ROWS - V_HEAD, vt.shape[1]), BF16)


C_END = Q_LORA + KV_LORA + QK_ROPE
C_PAD = C_END + LANES - QK_ROPE
NA_W = NA_HEADS * NA_HEAD_DIM
ROW_QK = C_END
ROW_V = ROW_QK + 2 * NA_W
ROW_G = ROW_V + NA_W
PROJ_TM = 1024
PROJ_TN = 1024


def _wt_spec(rows, k, l, row_map, **kw):
    return pl.BlockSpec((pl.Element(1), pl.Element(rows), pl.Element(k)),
                        lambda *ids: (l, row_map(*ids), 0), **kw)


def _mm_nt_kernel(a_ref, wt_ref, o_ref, wb_sc):
    @pl.when(pl.program_id(1) == 0)
    def _cast():
        wb_sc[...] = wt_ref[0].astype(BF16)

    o_ref[...] = lax.dot_general(a_ref[...], wb_sc[...], NT_DIMS,
                                 preferred_element_type=F32).astype(o_ref.dtype)


def _mm_nt(a, wt, l, row0, n, out_dtype, tm, tn, name):
    m, k = a.shape
    return pl.pallas_call(
        _mm_nt_kernel,
        grid=(n // tn, m // tm),
        in_specs=[pl.BlockSpec((tm, k), lambda j, i: (i, 0)),
                  _wt_spec(tn, k, l, lambda j, i: pl.multiple_of(row0 + j * tn, QK_ROPE))],
        out_specs=pl.BlockSpec((tm, tn), lambda j, i: (i, j)),
        out_shape=jax.ShapeDtypeStruct((m, n), out_dtype),
        scratch_shapes=[pltpu.VMEM((tn, k), BF16)],
        compiler_params=_params(2),
        name=name,
    )(a, wt)


def _norm_proj_kernel(x_ref, g_ref, wt_ref, u_ref, c_ref, wb_sc):
    @pl.when(pl.program_id(0) == 0)
    def _cast():
        wb_sc[...] = wt_ref[0].astype(BF16)

    u = _rms(x_ref[...], g_ref[...]).astype(BF16)
    u_ref[...] = u
    c_ref[...] = lax.dot_general(u, wb_sc[...], NT_DIMS, preferred_element_type=F32)


def _norm_proj(x, g, wt, l, tm=1024):
    s, d = x.shape
    return pl.pallas_call(
        _norm_proj_kernel,
        grid=(s // tm,),
        in_specs=[pl.BlockSpec((tm, d), lambda i: (i, 0)),
                  _layer_spec((1, d), l, lambda i: (0, 0)),
                  _wt_spec(C_PAD, d, l, lambda i: 0, pipeline_mode=pl.Buffered(1))],
        out_specs=[pl.BlockSpec((tm, d), lambda i: (i, 0)), pl.BlockSpec((tm, C_PAD), lambda i: (i, 0))],
        out_shape=[jax.ShapeDtypeStruct((s, d), BF16), jax.ShapeDtypeStruct((s, C_PAD), F32)],
        scratch_shapes=[pltpu.VMEM((C_PAD, d), BF16)],
        compiler_params=_params(1),
        name="norm_proj_c",
    )(x, g, wt)


def _proj_vt_kernel(u_ref, wt_ref, vt_ref, wb_sc):
    @pl.when(pl.program_id(0) == 0)
    def _cast():
        wb_sc[...] = wt_ref[0].astype(BF16)

    vt = lax.dot_general(wb_sc[...], u_ref[...], NT_DIMS, preferred_element_type=F32).astype(BF16)
    _store_vt_with_ones(vt_ref, vt, NA_HEADS)


def _proj_vt(u, wt, l, tm=1024):
    s, k = u.shape
    return pl.pallas_call(
        _proj_vt_kernel,
        grid=(s // tm,),
        in_specs=[pl.BlockSpec((tm, k), lambda i: (i, 0)),
                  _wt_spec(NA_W, k, l, lambda i: ROW_V)],
        out_specs=pl.BlockSpec((NA_HEADS * V_ROWS, tm), lambda i: (0, i)),
        out_shape=jax.ShapeDtypeStruct((NA_HEADS * V_ROWS, s), BF16),
        scratch_shapes=[pltpu.VMEM((NA_W, k), BF16)],
        compiler_params=_params(1),
        name="proj_na_vt",
    )(u, wt)


def _rope128(x, c, sl, sr):
    return x * c + pltpu.roll(x, LANES - QK_ROPE // 2, 1) * sl + pltpu.roll(x, QK_ROPE // 2, 1) * sr


def _mla_up_kernel(c_ref, gq_ref, gkv_ref, wqn_ref, wqp_ref, wkn_ref, wvt_ref,
                   cos_ref, sl_ref, sr_ref, q_ref, k_ref, vt_ref):
    scale = (QK_NOPE + QK_ROPE) ** -0.5 * LOG2E
    cq = _rms(c_ref[:, :Q_LORA], gq_ref[...]).astype(BF16)
    ckv = _rms(c_ref[:, Q_LORA:Q_LORA + KV_LORA], gkv_ref[...]).astype(BF16)
    kpe = c_ref[:, Q_LORA + KV_LORA:]
    kpe = jnp.where(lax.broadcasted_iota(jnp.int32, kpe.shape, 1) < QK_ROPE, kpe, 0.0)
    cos, sl, sr = cos_ref[...], sl_ref[...], sr_ref[...]
    kpe_rot = _rope128(kpe, cos, sl, sr).astype(BF16)
    qn = jnp.dot(cq, wqn_ref[...], preferred_element_type=F32) * scale
    qp = jnp.dot(cq, wqp_ref[...], preferred_element_type=F32)
    kn = jnp.dot(ckv, wkn_ref[...], preferred_element_type=F32)
    vt = lax.dot_general(wvt_ref[...], ckv, NT_DIMS, preferred_element_type=F32).astype(BF16)
    _store_vt_with_ones(vt_ref, vt, MLA_HEADS)
    for h in range(MLA_HEADS):
        lo = h * MLA_QK_PAD
        hs = slice(h * LANES, (h + 1) * LANES)
        q_ref[:, lo:lo + LANES] = qn[:, hs].astype(BF16)
        q_ref[:, lo + LANES:lo + 2 * LANES] = (_rope128(qp[:, hs], cos, sl, sr) * scale).astype(BF16)
        k_ref[:, lo:lo + LANES] = kn[:, hs].astype(BF16)
        k_ref[:, lo + LANES:lo + 2 * LANES] = kpe_rot


def _mla_up(c, gq, gkv, wqn, wqp, wkn, wvt, l, cos_t, sl_t, sr_t, tm=1024):
    s = c.shape[0]
    row = lambda w: pl.BlockSpec((tm, w), lambda i: (i, 0))
    full = lambda a: _layer_spec(a.shape[1:], l, lambda i: (0, 0))
    hw = MLA_HEADS * V_ROWS
    return pl.pallas_call(
        _mla_up_kernel,
        grid=(s // tm,),
        in_specs=[row(c.shape[1]), full(gq), full(gkv), full(wqn), full(wqp), full(wkn), full(wvt),
                  row(LANES), row(LANES), row(LANES)],
        out_specs=[row(MLA_HEADS * MLA_QK_PAD), row(MLA_HEADS * MLA_QK_PAD),
                   pl.BlockSpec((hw, tm), lambda i: (0, i))],
        out_shape=[jax.ShapeDtypeStruct((s, MLA_HEADS * MLA_QK_PAD), BF16),
                   jax.ShapeDtypeStruct((s, MLA_HEADS * MLA_QK_PAD), BF16),
                   jax.ShapeDtypeStruct((hw, s), BF16)],
        compiler_params=_params(1),
        name="mla_up",
    )(c, gq, gkv, wqn, wqp, wkn, wvt, cos_t, sl_t, sr_t)


def _mla_attn_kernel(*refs, tk, n_cast):
    q_ref, k_ref, vt_ref = refs[:3]
    cast_src = refs[3:3 + n_cast]
    o_ref = refs[3 + n_cast]
    cast_dst = refs[4 + n_cast:4 + 2 * n_cast]
    stat_sc, acc_sc = refs[4 + 2 * n_cast:]
    r_sc, mx_sc, m_sc = (stat_sc.at[pl.ds(j * F32_SUBLANES, 1), :] for j in range(3))

    for src, dst in zip(cast_src, cast_dst):
        dst[...] = src[0].astype(BF16)

    q = q_ref[...]
    n_chunks = k_ref.shape[0] // tk
    assert n_chunks % 2 == 0

    def scores(c):
        k = k_ref[pl.ds(pl.multiple_of(c * tk, tk), tk), :]
        return lax.dot_general(k, q, NT_DIMS, preferred_element_type=F32)

    def values(c, pt):
        vt = vt_ref[:, pl.ds(pl.multiple_of(c * tk, tk), tk)]
        return jnp.dot(vt, pt.astype(BF16), preferred_element_type=F32)

    def write_out():
        o_ref[...] = (acc_sc[:V_HEAD, :] / acc_sc[V_HEAD:V_HEAD + 1, :]).T.astype(o_ref.dtype)

    s_ref = lax.dot_general(k_ref[:MLA_REF_KEYS, :], q, NT_DIMS, preferred_element_type=F32)
    r = jnp.max(s_ref, axis=0, keepdims=True)
    r_sc[...] = r
    mx_sc[...] = r
    acc_sc[...] = jnp.zeros(acc_sc.shape, F32)

    def step(c):
        st = scores(c)
        mx_sc[...] = jnp.maximum(mx_sc[...], jnp.max(st, axis=0, keepdims=True))
        acc_sc[...] += values(c, jnp.exp2(st - r_sc[...]))

    def single_pass_body(j, carry):
        step(2 * j)
        step(2 * j + 1)
        return carry

    lax.fori_loop(0, n_chunks // 2, single_pass_body, 0)
    in_range = jnp.max(mx_sc[...] - r_sc[...]) <= MLA_MAX_GAP

    @pl.when(in_range)
    def _accept():
        write_out()

    @pl.when(jnp.logical_not(in_range))
    def _online_softmax():
        m_sc[...] = jnp.full(m_sc.shape, -jnp.inf, F32)
        acc_sc[...] = jnp.zeros(acc_sc.shape, F32)

        def body(c, carry):
            st = scores(c)
            m_prev = m_sc[...]
            m_new = jnp.maximum(m_prev, jnp.max(st, axis=0, keepdims=True))
            acc_sc[...] = jnp.exp2(m_prev - m_new) * acc_sc[...] + values(c, jnp.exp2(st - m_new))
            m_sc[...] = m_new
            return carry

        lax.fori_loop(0, n_chunks, body, 0)
        write_out()


def _mla_attn(q, k, vt, casts=(), l=0, tq=2048, tk=2048):
    s = q.shape[0]
    nq = s // tq
    steps = MLA_HEADS * nq
    for _, _, rows in casts:
        assert rows % (steps * BF16_SUBLANES) == 0

    def src_spec(w, row0, rows):
        share = rows // steps
        return _wt_spec(share, w.shape[2], l, lambda h, i: pl.multiple_of(row0 + (h * nq + i) * share, BF16_SUBLANES))

    dst_spec = lambda w, rows: pl.BlockSpec((rows // steps, w.shape[2]), lambda h, i: (h * nq + i, 0))
    return pl.pallas_call(
        functools.partial(_mla_attn_kernel, tk=tk, n_cast=len(casts)),
        grid=(MLA_HEADS, nq),
        in_specs=[pl.BlockSpec((tq, MLA_QK_PAD), lambda h, i: (i, h)),
                  pl.BlockSpec((s, MLA_QK_PAD), lambda h, i: (0, h)),
                  pl.BlockSpec((V_ROWS, s), lambda h, i: (h, 0))]
                 + [src_spec(w, row0, rows) for w, row0, rows in casts],
        out_specs=[pl.BlockSpec((tq, V_HEAD), lambda h, i: (i, h))]
                  + [dst_spec(w, rows) for w, _, rows in casts],
        out_shape=[jax.ShapeDtypeStruct((s, MLA_HEADS * V_HEAD), BF16)]
                  + [jax.ShapeDtypeStruct((rows, w.shape[2]), BF16) for w, _, rows in casts],
        scratch_shapes=[pltpu.VMEM((3 * F32_SUBLANES, tq), F32), pltpu.VMEM((V_ROWS, tq), F32)],
        compiler_params=_params(2),
        name="mla_attn",
    )(q, k, vt, *[w for w, _, _ in casts])


def _na_row_start(r):
    return min(max(r - NA_KH // 2, 0), ROWS - NA_KH)


def _na_win_start(blk):
    return min(max(blk * NA_Q_ROWS - NA_KH // 2, 0), ROWS - NA_WIN_ROWS)


def _na_row_window(blk, r_in_blk):
    r = blk * NA_Q_ROWS + r_in_blk
    return _na_row_start(r) - _na_win_start(blk), _na_row_start(r) - r + (NA_KH - 1)


def _na_rpb_rows(rpb):
    n_dy, n_dx = 2 * NA_KH - 1, 2 * NA_KW - 1
    return jnp.pad(rpb.astype(F32)[..., ::-1], ((0, 0), (0, 0), (0, 2 * NA_KH - n_dy), (0, LANES - n_dx)))


def _na_attn_kernel(q_ref, k_ref, vt_ref, r_ref, o_ref, toe_sc, bias_sc, sa_sc, sb_sc, cm_sc):
    last = NA_BLOCKS - 1
    ma_sc, mb_sc = (cm_sc.at[pl.ds(j * F32_SUBLANES, 1), :] for j in range(2))

    kc = lax.broadcasted_iota(jnp.int32, (GRID_W, LANES), 0)
    lane = lax.broadcasted_iota(jnp.int32, (GRID_W, LANES), 1)
    for half in range(2):
        qc = lane - half * GRID_W
        col_start = jnp.clip(qc - NA_KW // 2, 0, GRID_W - NA_KW)
        ok = (qc >= 0) & (qc < GRID_W) & (kc >= col_start) & (kc < col_start + NA_KW)
        shift = (LANES - (NA_KW - 1) + half * GRID_W) % LANES
        for dy in range(2 * NA_KH - 1):
            row = jnp.broadcast_to(r_ref[0, dy:dy + 1, :], (GRID_W, LANES))
            toe = pltpu.roll(row, shift, 1, stride=1, stride_axis=0)
            toe_sc[half, dy] = jnp.where(ok, toe * LOG2E, -jnp.inf)

    bias_sc[...] = jnp.full(bias_sc.shape, -jnp.inf, F32)
    for var, blk in enumerate((0, 1, last)):
        for pair in range(NA_Q_ROWS // 2):
            (off_e, dy_e), (off_o, dy_o) = _na_row_window(blk, 2 * pair), _na_row_window(blk, 2 * pair + 1)
            for b in range(min(off_e, off_o), max(off_e, off_o) + NA_KH):
                parts = []
                if 0 <= b - off_e < NA_KH:
                    parts.append(toe_sc[0, dy_e + b - off_e])
                if 0 <= b - off_o < NA_KH:
                    parts.append(toe_sc[1, dy_o + b - off_o])
                blkv = parts[0] if len(parts) == 1 else jnp.maximum(parts[0], parts[1])
                bias_sc[var, b * GRID_W:(b + 1) * GRID_W, pair * LANES:(pair + 1) * LANES] = blkv

    def win_start(blk):
        start = jnp.clip(blk * NA_TQ - (NA_KH // 2) * GRID_W, 0, SEQ - NA_TK)
        return pl.multiple_of(start, (NA_KH // 2) * GRID_W)

    def scores(blk, var, s_sc, cm_sc):
        q = q_ref[pl.ds(pl.multiple_of(blk * NA_TQ, NA_TQ), NA_TQ), :]
        k = k_ref[pl.ds(win_start(blk), NA_TK), :]
        st = lax.dot_general(k, q, NT_DIMS, preferred_element_type=F32)
        st = st * (NA_HEAD_DIM ** -0.5 * LOG2E) + bias_sc[var]
        s_sc[...] = st
        cm_sc[...] = jnp.max(st, axis=0, keepdims=True)

    def finish(blk, s_sc, cm_sc):
        vt = vt_ref[:, pl.ds(win_start(blk), NA_TK)]
        pt = jnp.exp2(s_sc[...] - cm_sc[...])
        acc = jnp.dot(vt, pt.astype(BF16), preferred_element_type=F32)
        o = (acc[:V_HEAD, :] / acc[V_HEAD:V_HEAD + 1, :]).T
        o_ref[pl.ds(pl.multiple_of(blk * NA_TQ, NA_TQ), NA_TQ), :] = o.astype(o_ref.dtype)

    assert NA_BLOCKS % 4 == 0
    scores(0, 0, sa_sc, ma_sc)

    def body(j, carry):
        b = 4 * j
        scores(b + 1, 1, sb_sc, mb_sc)
        finish(b, sa_sc, ma_sc)
        scores(b + 2, 1, sa_sc, ma_sc)
        finish(b + 1, sb_sc, mb_sc)
        scores(b + 3, 1, sb_sc, mb_sc)
        finish(b + 2, sa_sc, ma_sc)
        scores(b + 4, 1, sa_sc, ma_sc)
        finish(b + 3, sb_sc, mb_sc)
        return carry

    lax.fori_loop(0, (NA_BLOCKS - 4) // 4, body, 0)
    scores(last - 2, 1, sb_sc, mb_sc)
    finish(last - 3, sa_sc, ma_sc)
    scores(last - 1, 1, sa_sc, ma_sc)
    finish(last - 2, sb_sc, mb_sc)
    scores(last, 2, sb_sc, mb_sc)
    finish(last - 1, sa_sc, ma_sc)
    finish(last, sb_sc, mb_sc)


def _na_attn(qk, vt, rpb_rows, l):
    s = qk.shape[0]
    return pl.pallas_call(
        _na_attn_kernel,
        grid=(NA_HEADS,),
        in_specs=[pl.BlockSpec((s, NA_HEAD_DIM), lambda h: (0, h)),
                  pl.BlockSpec((s, NA_HEAD_DIM), lambda h: (0, NA_HEADS + h)),
                  pl.BlockSpec((V_ROWS, s), lambda h: (h, 0)),
                  _layer_spec((1, 2 * NA_KH, LANES), l, lambda h: (h, 0, 0))],
        out_specs=pl.BlockSpec((s, NA_HEAD_DIM), lambda h: (0, h)),
        out_shape=jax.ShapeDtypeStruct((s, NA_HEADS * NA_HEAD_DIM), BF16),
        scratch_shapes=[pltpu.VMEM((2, 2 * NA_KH - 1, GRID_W, LANES), F32), pltpu.VMEM((3, NA_TK, NA_TQ), F32),
                        pltpu.VMEM((NA_TK, NA_TQ), F32), pltpu.VMEM((NA_TK, NA_TQ), F32),
                        pltpu.VMEM((2 * F32_SUBLANES, NA_TQ), F32)],
        compiler_params=_params(1),
        name="na_attn",
    )(qk, qk, vt, rpb_rows)


def _merge_kernel(ya_ref, yb_ref, u_ref, x_ref, woa_ref, wob_ref, wo_ref, wg_ref, o_ref):
    g = lax.dot_general(u_ref[...], wg_ref[...], NT_DIMS, preferred_element_type=F32)
    a = jnp.dot(ya_ref[...], woa_ref[...], preferred_element_type=F32)
    b = jnp.dot(yb_ref[...], wob_ref[...], preferred_element_type=F32)
    merged = (jax.nn.sigmoid(g[:, :D_MODEL]) * a + jax.nn.sigmoid(g[:, D_MODEL:]) * b).astype(BF16)
    o_ref[...] = x_ref[...] + jnp.dot(merged, wo_ref[...], preferred_element_type=F32)


def _merge(ya, yb, u, x, woa, wob, wo, wg_t, tm=256):
    s = ya.shape[0]
    row = lambda w: pl.BlockSpec((tm, w), lambda i: (i, 0))
    full = lambda a: pl.BlockSpec(a.shape, lambda i: (0, 0), pipeline_mode=pl.Buffered(1))
    return pl.pallas_call(
        _merge_kernel,
        grid=(s // tm,),
        in_specs=[row(ya.shape[1]), row(yb.shape[1]), row(D_MODEL), row(D_MODEL),
                  full(woa), full(wob), full(wo), full(wg_t)],
        out_specs=row(D_MODEL),
        out_shape=jax.ShapeDtypeStruct((s, D_MODEL), F32),
        compiler_params=_params(1),
        name="merge_out",
    )(ya, yb, u, x, woa, wob, wo, wg_t)


def _ffn_kernel(x_ref, g_ref, w1_ref, w2_ref, gf_ref, o_ref, u_sc, *, final_norm):
    f = pl.program_id(1)

    @pl.when(f == 0)
    def _init():
        x = x_ref[...]
        u_sc[...] = _rms(x, g_ref[...]).astype(BF16)
        o_ref[...] = x

    h = jnp.dot(u_sc[...], w1_ref[...], preferred_element_type=F32)
    a = jnp.square(jnp.maximum(h, 0.0)).astype(BF16)
    o_ref[...] += jnp.dot(a, w2_ref[...], preferred_element_type=F32)

    if final_norm:
        @pl.when(f == pl.num_programs(1) - 1)
        def _final():
            o_ref[...] = _rms(o_ref[...], gf_ref[...])


def _ffn(x, g, w1, w2, l, g_final, final_norm, tm=512, tf=1024):
    s, d = x.shape
    return pl.pallas_call(
        functools.partial(_ffn_kernel, final_norm=final_norm),
        grid=(s // tm, w1.shape[1] // tf),
        in_specs=[pl.BlockSpec((tm, d), lambda i, f: (i, 0)),
                  _layer_spec((1, d), l, lambda i, f: (0, 0)),
                  pl.BlockSpec((d, tf), lambda i, f: (0, f)),
                  pl.BlockSpec((tf, d), lambda i, f: (f, 0)),
                  pl.BlockSpec((1, d), lambda i, f: (0, 0))],
        out_specs=pl.BlockSpec((tm, d), lambda i, f: (i, 0)),
        out_shape=jax.ShapeDtypeStruct((s, d), F32),
        scratch_shapes=[pltpu.VMEM((tm, d), BF16)],
        compiler_params=_params(2),
        name="ffn",
    )(x, g, w1, w2, g_final)


def _rope_tables(s):
    pos = jnp.arange(s, dtype=F32)
    inv_freq = 1.0 / (ROPE_THETA ** (jnp.arange(0, QK_ROPE, 2, dtype=F32) / QK_ROPE))
    ang = pos[:, None] * inv_freq[None, :]
    cos, sin = jnp.cos(ang), jnp.sin(ang)
    half = QK_ROPE // 2
    z = lambda w: jnp.zeros((s, w), F32)
    cos_t = jnp.concatenate([cos, cos, z(LANES - QK_ROPE)], axis=1)
    sl_t = jnp.concatenate([-sin, z(LANES - half)], axis=1)
    sr_t = jnp.concatenate([z(half), sin, z(LANES - QK_ROPE)], axis=1)
    return cos_t, sl_t, sr_t


def kernel(x, norm_mix, w_in, norm_qa, w_uq, norm_kva, w_ukv, rpb, w_o_mla, w_o_na, w_out,
           norm_mlp, w_ff1, w_ff2, norm_final):
    b, s, d = x.shape
    assert (b, s, d) == (1, SEQ, D_MODEL)
    depth = w_in.shape[0]
    cos_t, sl_t, sr_t = _rope_tables(s)

    w_in_t = w_in.transpose(0, 2, 1)
    uq = w_uq.reshape(depth, Q_LORA, MLA_HEADS, QK_NOPE + QK_ROPE)
    w_qn = uq[..., :QK_NOPE].reshape(depth, Q_LORA, MLA_HEADS * QK_NOPE).astype(BF16)
    w_qp = jnp.pad(uq[..., QK_NOPE:], ((0, 0), (0, 0), (0, 0), (0, LANES - QK_ROPE)))
    w_qp = w_qp.reshape(depth, Q_LORA, MLA_HEADS * LANES).astype(BF16)
    ukv = w_ukv.reshape(depth, KV_LORA, MLA_HEADS, QK_NOPE + V_HEAD)
    w_kn = ukv[..., :QK_NOPE].reshape(depth, KV_LORA, MLA_HEADS * QK_NOPE).astype(BF16)
    w_vt = ukv[..., QK_NOPE:].reshape(depth, KV_LORA, MLA_HEADS * V_HEAD).transpose(0, 2, 1).astype(BF16)
    g_qa = norm_qa.reshape(depth, 1, Q_LORA)
    g_kva = norm_kva.reshape(depth, 1, KV_LORA)
    g_mlp = norm_mlp.reshape(depth, 1, d)
    g_mix = norm_mix.reshape(depth, 1, d)
    g_final = norm_final.reshape(1, d)
    rpb_rows = _na_rpb_rows(rpb)

    xs = x.reshape(s, d)
    for l in range(depth):
        u, c = _norm_proj(xs, g_mix, w_in_t, l)
        qk_na = _mm_nt(u, w_in_t, l, ROW_QK, 2 * NA_W, BF16, PROJ_TM, PROJ_TN, name="proj_na_qk")
        vt_na = _proj_vt(u, w_in_t, l)
        q, k, vt = _mla_up(c, g_qa, g_kva, w_qn, w_qp, w_kn, w_vt, l, cos_t, sl_t, sr_t)
        casts = tuple((w, 0, w.shape[1]) for w in (w_o_mla, w_o_na, w_out, w_ff1, w_ff2))
        casts += ((w_in_t, ROW_G, 2 * D_MODEL),)
        y_a, w_oa, w_ob, w_o, w_1, w_2, w_g_t = _mla_attn(q, k, vt, casts, l)
        y_b = _na_attn(qk_na, vt_na, rpb_rows, l)
        xs = _merge(y_a, y_b, u, xs, w_oa, w_ob, w_o, w_g_t)
        xs = _ffn(xs, g_mlp, w_1, w_2, l, g_final, final_norm=(l == depth - 1))
    return xs.reshape(b, s, d)
```

```python
import functools

import jax
import jax.numpy as jnp
import numpy as np
from jax import lax
from jax.experimental import pallas as pl
from jax.experimental.pallas import tpu as pltpu

D_MODEL = 2048
SEQ = 8192
GRID_W = 64
ROWS = SEQ // GRID_W
MLA_HEADS = 8
Q_LORA = 512
KV_LORA = 512
QK_NOPE = 128
QK_ROPE = 64
V_HEAD = 128
ROPE_THETA = 10000.0
NA_HEADS = 8
NA_HEAD_DIM = 128
NA_KH = 8
NA_KW = 16
D_FF = 4 * D_MODEL
EPS = 1e-6
LOG2E = float(np.log2(np.e))

LANES = 128
MLA_QK_PAD = 2 * LANES
F32_SUBLANES = 8
BF16_SUBLANES = 16
V_ROWS = V_HEAD + BF16_SUBLANES
MLA_REF_KEYS = 128
MLA_MAX_GAP = 60.0
NA_Q_ROWS = 8
NA_WIN_ROWS = 2 * NA_KH
NA_TQ = NA_Q_ROWS * GRID_W
NA_TK = NA_WIN_ROWS * GRID_W
NA_BLOCKS = SEQ // NA_TQ
NA_STRIP = NA_KH * GRID_W
VMEM_LIMIT = 56 * 1024 * 1024

BF16 = jnp.bfloat16
F32 = jnp.float32
NT_DIMS = (((1,), (1,)), ((), ()))


def _params(n_axes):
    return pltpu.CompilerParams(
        dimension_semantics=("arbitrary",) * n_axes, vmem_limit_bytes=VMEM_LIMIT)


def _rms(x, g):
    return x * lax.rsqrt(jnp.mean(x * x, axis=-1, keepdims=True) + EPS) * g


def _layer_spec(shape, l, index_map):
    return pl.BlockSpec((None,) + tuple(shape), lambda *ids: (l,) + tuple(index_map(*ids)))


def _store_vt_with_ones(vt_ref, vt, heads):
    for h in range(heads):
        vt_ref[h * V_ROWS:h * V_ROWS + V_HEAD, :] = vt[h * V_HEAD:(h + 1) * V_HEAD, :]
        vt_ref[h * V_ROWS + V_HEAD:(h + 1) * V_ROWS, :] = jnp.ones((V_ROWS - V_HEAD, vt.shape[1]), BF16)


C_END = Q_LORA + KV_LORA + QK_ROPE
C_PAD = C_END + LANES - QK_ROPE
NA_W = NA_HEADS * NA_HEAD_DIM
ROW_QK = C_END
ROW_V = ROW_QK + 2 * NA_W
ROW_G = ROW_V + NA_W
PROJ_TM = 1024
PROJ_TN = 1024


def _wt_spec(rows, k, l, row_map, **kw):
    return pl.BlockSpec((pl.Element(1), pl.Element(rows), pl.Element(k)),
                        lambda *ids: (l, row_map(*ids), 0), **kw)


def _mm_nt_kernel(a_ref, wt_ref, o_ref, wb_sc):
    @pl.when(pl.program_id(1) == 0)
    def _cast():
        wb_sc[...] = wt_ref[0].astype(BF16)

    o_ref[...] = lax.dot_general(a_ref[...], wb_sc[...], NT_DIMS,
                                 preferred_element_type=F32).astype(o_ref.dtype)


def _mm_nt(a, wt, l, row0, n, out_dtype, tm, tn, name):
    m, k = a.shape
    return pl.pallas_call(
        _mm_nt_kernel,
        grid=(n // tn, m // tm),
        in_specs=[pl.BlockSpec((tm, k), lambda j, i: (i, 0)),
                  _wt_spec(tn, k, l, lambda j, i: pl.multiple_of(row0 + j * tn, QK_ROPE))],
        out_specs=pl.BlockSpec((tm, tn), lambda j, i: (i, j)),
        out_shape=jax.ShapeDtypeStruct((m, n), out_dtype),
        scratch_shapes=[pltpu.VMEM((tn, k), BF16)],
        compiler_params=_params(2),
        name=name,
    )(a, wt)


def _norm_proj_kernel(x_ref, g_ref, wt_ref, u_ref, c_ref, wb_sc):
    @pl.when(pl.program_id(0) == 0)
    def _cast():
        wb_sc[...] = wt_ref[0].astype(BF16)

    u = _rms(x_ref[...], g_ref[...]).astype(BF16)
    u_ref[...] = u
    c_ref[...] = lax.dot_general(u, wb_sc[...], NT_DIMS, preferred_element_type=F32)


def _norm_proj(x, g, wt, l, tm=1024):
    s, d = x.shape
    return pl.pallas_call(
        _norm_proj_kernel,
        grid=(s // tm,),
        in_specs=[pl.BlockSpec((tm, d), lambda i: (i, 0)),
                  _layer_spec((1, d), l, lambda i: (0, 0)),
                  _wt_spec(C_PAD, d, l, lambda i: 0, pipeline_mode=pl.Buffered(1))],
        out_specs=[pl.BlockSpec((tm, d), lambda i: (i, 0)), pl.BlockSpec((tm, C_PAD), lambda i: (i, 0))],
        out_shape=[jax.ShapeDtypeStruct((s, d), BF16), jax.ShapeDtypeStruct((s, C_PAD), F32)],
        scratch_shapes=[pltpu.VMEM((C_PAD, d), BF16)],
        compiler_params=_params(1),
        name="norm_proj_c",
    )(x, g, wt)


def _proj_vt_kernel(u_ref, wt_ref, vt_ref, wb_sc):
    @pl.when(pl.program_id(0) == 0)
    def _cast():
        wb_sc[...] = wt_ref[0].astype(BF16)

    vt = lax.dot_general(wb_sc[...], u_ref[...], NT_DIMS, preferred_element_type=F32).astype(BF16)
    _store_vt_with_ones(vt_ref, vt, NA_HEADS)


def _proj_vt(u, wt, l, tm=1024):
    s, k = u.shape
    return pl.pallas_call(
        _proj_vt_kernel,
        grid=(s // tm,),
        in_specs=[pl.BlockSpec((tm, k), lambda i: (i, 0)),
                  _wt_spec(NA_W, k, l, lambda i: ROW_V)],
        out_specs=pl.BlockSpec((NA_HEADS * V_ROWS, tm), lambda i: (0, i)),
        out_shape=jax.ShapeDtypeStruct((NA_HEADS * V_ROWS, s), BF16),
        scratch_shapes=[pltpu.VMEM((NA_W, k), BF16)],
        compiler_params=_params(1),
        name="proj_na_vt",
    )(u, wt)


def _rope128(x, c, sl, sr):
    return x * c + pltpu.roll(x, LANES - QK_ROPE // 2, 1) * sl + pltpu.roll(x, QK_ROPE // 2, 1) * sr


def _mla_up_kernel(c_ref, gq_ref, gkv_ref, wqn_ref, wqp_ref, wkn_ref, wvt_ref,
                   cos_ref, sl_ref, sr_ref, q_ref, k_ref, vt_ref):
    scale = (QK_NOPE + QK_ROPE) ** -0.5 * LOG2E
    cq = _rms(c_ref[:, :Q_LORA], gq_ref[...]).astype(BF16)
    ckv = _rms(c_ref[:, Q_LORA:Q_LORA + KV_LORA], gkv_ref[...]).astype(BF16)
    kpe = c_ref[:, Q_LORA + KV_LORA:]
    kpe = jnp.where(lax.broadcasted_iota(jnp.int32, kpe.shape, 1) < QK_ROPE, kpe, 0.0)
    cos, sl, sr = cos_ref[...], sl_ref[...], sr_ref[...]
    kpe_rot = _rope128(kpe, cos, sl, sr).astype(BF16)
    qn = jnp.dot(cq, wqn_ref[...], preferred_element_type=F32) * scale
    qp = jnp.dot(cq, wqp_ref[...], preferred_element_type=F32)
    kn = jnp.dot(ckv, wkn_ref[...], preferred_element_type=F32)
    vt = lax.dot_general(wvt_ref[...], ckv, NT_DIMS, preferred_element_type=F32).astype(BF16)
    _store_vt_with_ones(vt_ref, vt, MLA_HEADS)
    for h in range(MLA_HEADS):
        lo = h * MLA_QK_PAD
        hs = slice(h * LANES, (h + 1) * LANES)
        q_ref[:, lo:lo + LANES] = qn[:, hs].astype(BF16)
        q_ref[:, lo + LANES:lo + 2 * LANES] = (_rope128(qp[:, hs], cos, sl, sr) * scale).astype(BF16)
        k_ref[:, lo:lo + LANES] = kn[:, hs].astype(BF16)
        k_ref[:, lo + LANES:lo + 2 * LANES] = kpe_rot


def _mla_up(c, gq, gkv, wqn, wqp, wkn, wvt, l, cos_t, sl_t, sr_t, tm=1024):
    s = c.shape[0]
    row = lambda w: pl.BlockSpec((tm, w), lambda i: (i, 0))
    full = lambda a: _layer_spec(a.shape[1:], l, lambda i: (0, 0))
    hw = MLA_HEADS * V_ROWS
    return pl.pallas_call(
        _mla_up_kernel,
        grid=(s // tm,),
        in_specs=[row(c.shape[1]), full(gq), full(gkv), full(wqn), full(wqp), full(wkn), full(wvt),
                  row(LANES), row(LANES), row(LANES)],
        out_specs=[row(MLA_HEADS * MLA_QK_PAD), row(MLA_HEADS * MLA_QK_PAD),
                   pl.BlockSpec((hw, tm), lambda i: (0, i))],
        out_shape=[jax.ShapeDtypeStruct((s, MLA_HEADS * MLA_QK_PAD), BF16),
                   jax.ShapeDtypeStruct((s, MLA_HEADS * MLA_QK_PAD), BF16),
                   jax.ShapeDtypeStruct((hw, s), BF16)],
        compiler_params=_params(1),
        name="mla_up",
    )(c, gq, gkv, wqn, wqp, wkn, wvt, cos_t, sl_t, sr_t)


def _mla_attn_kernel(*refs, tk, n_cast):
    q_ref, k_ref, vt_ref = refs[:3]
    cast_src = refs[3:3 + n_cast]
    o_ref = refs[3 + n_cast]
    cast_dst = refs[4 + n_cast:4 + 2 * n_cast]
    stat_sc, acc_sc = refs[4 + 2 * n_cast:]
    r_sc, mx_sc, m_sc = (stat_sc.at[pl.ds(j * F32_SUBLANES, 1), :] for j in range(3))

    for src, dst in zip(cast_src, cast_dst):
        dst[...] = src[0].astype(BF16)

    q = q_ref[...]
    n_chunks = k_ref.shape[0] // tk
    assert n_chunks % 2 == 0

    def scores(c):
        k = k_ref[pl.ds(pl.multiple_of(c * tk, tk), tk), :]
        return lax.dot_general(k, q, NT_DIMS, preferred_element_type=F32)

    def values(c, pt):
        vt = vt_ref[:, pl.ds(pl.multiple_of(c * tk, tk), tk)]
        return jnp.dot(vt, pt.astype(BF16), preferred_element_type=F32)

    def write_out():
        o_ref[...] = (acc_sc[:V_HEAD, :] / acc_sc[V_HEAD:V_HEAD + 1, :]).T.astype(o_ref.dtype)

    s_ref = lax.dot_general(k_ref[:MLA_REF_KEYS, :], q, NT_DIMS, preferred_element_type=F32)
    r = jnp.max(s_ref, axis=0, keepdims=True)
    r_sc[...] = r
    mx_sc[...] = r
    acc_sc[...] = jnp.zeros(acc_sc.shape, F32)

    def step(c):
        st = scores(c)
        mx_sc[...] = jnp.maximum(mx_sc[...], jnp.max(st, axis=0, keepdims=True))
        acc_sc[...] += values(c, jnp.exp2(st - r_sc[...]))

    def single_pass_body(j, carry):
        step(2 * j)
        step(2 * j + 1)
        return carry

    lax.fori_loop(0, n_chunks // 2, single_pass_body, 0)
    in_range = jnp.max(mx_sc[...] - r_sc[...]) <= MLA_MAX_GAP

    @pl.when(in_range)
    def _accept():
        write_out()

    @pl.when(jnp.logical_not(in_range))
    def _online_softmax():
        m_sc[...] = jnp.full(m_sc.shape, -jnp.inf, F32)
        acc_sc[...] = jnp.zeros(acc_sc.shape, F32)

        def body(c, carry):
            st = scores(c)
            m_prev = m_sc[...]
            m_new = jnp.maximum(m_prev, jnp.max(st, axis=0, keepdims=True))
            acc_sc[...] = jnp.exp2(m_prev - m_new) * acc_sc[...] + values(c, jnp.exp2(st - m_new))
            m_sc[...] = m_new
            return carry

        lax.fori_loop(0, n_chunks, body, 0)
        write_out()


def _mla_attn(q, k, vt, casts=(), l=0, tq=2048, tk=2048):
    s = q.shape[0]
    nq = s // tq
    steps = MLA_HEADS * nq
    for _, _, rows in casts:
        assert rows % (steps * BF16_SUBLANES) == 0

    def src_spec(w, row0, rows):
        share = rows // steps
        return _wt_spec(share, w.shape[2], l, lambda h, i: pl.multiple_of(row0 + (h * nq + i) * share, BF16_SUBLANES))

    dst_spec = lambda w, rows: pl.BlockSpec((rows // steps, w.shape[2]), lambda h, i: (h * nq + i, 0))
    return pl.pallas_call(
        functools.partial(_mla_attn_kernel, tk=tk, n_cast=len(casts)),
        grid=(MLA_HEADS, nq),
        in_specs=[pl.BlockSpec((tq, MLA_QK_PAD), lambda h, i: (i, h)),
                  pl.BlockSpec((s, MLA_QK_PAD), lambda h, i: (0, h)),
                  pl.BlockSpec((V_ROWS, s), lambda h, i: (h, 0))]
                 + [src_spec(w, row0, rows) for w, row0, rows in casts],
        out_specs=[pl.BlockSpec((tq, V_HEAD), lambda h, i: (i, h))]
                  + [dst_spec(w, rows) for w, _, rows in casts],
        out_shape=[jax.ShapeDtypeStruct((s, MLA_HEADS * V_HEAD), BF16)]
                  + [jax.ShapeDtypeStruct((rows, w.shape[2]), BF16) for w, _, rows in casts],
        scratch_shapes=[pltpu.VMEM((3 * F32_SUBLANES, tq), F32), pltpu.VMEM((V_ROWS, tq), F32)],
        compiler_params=_params(2),
        name="mla_attn",
    )(q, k, vt, *[w for w, _, _ in casts])


def _na_row_start(r):
    return min(max(r - NA_KH // 2, 0), ROWS - NA_KH)


def _na_win_start(blk):
    return min(max(blk * NA_Q_ROWS - NA_KH // 2, 0), ROWS - NA_WIN_ROWS)


def _na_row_window(blk, r_in_blk):
    r = blk * NA_Q_ROWS + r_in_blk
    return _na_row_start(r) - _na_win_start(blk), _na_row_start(r) - r + (NA_KH - 1)


def _na_rpb_rows(rpb):
    n_dy, n_dx = 2 * NA_KH - 1, 2 * NA_KW - 1
    return jnp.pad(rpb.astype(F32)[..., ::-1], ((0, 0), (0, 0), (0, 2 * NA_KH - n_dy), (0, LANES - n_dx)))


def _na_attn_kernel(q_ref, k_ref, vt_ref, r_ref, o_ref, toe_sc, bias_sc, sa_sc, sb_sc, cm_sc):
    last = NA_BLOCKS - 1
    ma_sc, mb_sc = (cm_sc.at[pl.ds(j * F32_SUBLANES, 1), :] for j in range(2))

    kc = lax.broadcasted_iota(jnp.int32, (GRID_W, LANES), 0)
    lane = lax.broadcasted_iota(jnp.int32, (GRID_W, LANES), 1)
    for half in range(2):
        qc = lane - half * GRID_W
        col_start = jnp.clip(qc - NA_KW // 2, 0, GRID_W - NA_KW)
        ok = (qc >= 0) & (qc < GRID_W) & (kc >= col_start) & (kc < col_start + NA_KW)
        shift = (LANES - (NA_KW - 1) + half * GRID_W) % LANES
        for dy in range(2 * NA_KH - 1):
            row = jnp.broadcast_to(r_ref[0, dy:dy + 1, :], (GRID_W, LANES))
            toe = pltpu.roll(row, shift, 1, stride=1, stride_axis=0)
            toe_sc[half, dy] = jnp.where(ok, toe * LOG2E, -jnp.inf)

    bias_sc[...] = jnp.full(bias_sc.shape, -jnp.inf, F32)
    for var, blk in enumerate((0, 1, last)):
        for pair in range(NA_Q_ROWS // 2):
            (off_e, dy_e), (off_o, dy_o) = _na_row_window(blk, 2 * pair), _na_row_window(blk, 2 * pair + 1)
            for b in range(min(off_e, off_o), max(off_e, off_o) + NA_KH):
                parts = []
                if 0 <= b - off_e < NA_KH:
                    parts.append(toe_sc[0, dy_e + b - off_e])
                if 0 <= b - off_o < NA_KH:
                    parts.append(toe_sc[1, dy_o + b - off_o])
                blkv = parts[0] if len(parts) == 1 else jnp.maximum(parts[0], parts[1])
                bias_sc[var, b * GRID_W:(b + 1) * GRID_W, pair * LANES:(pair + 1) * LANES] = blkv

    def win_start(blk):
        start = jnp.clip(blk * NA_TQ - (NA_KH // 2) * GRID_W, 0, SEQ - NA_TK)
        return pl.multiple_of(start, (NA_KH // 2) * GRID_W)

    def scores(blk, var, s_sc, cm_sc):
        q = q_ref[pl.ds(pl.multiple_of(blk * NA_TQ, NA_TQ), NA_TQ), :]
        k = k_ref[pl.ds(win_start(blk), NA_TK), :]
        st = lax.dot_general(k, q, NT_DIMS, preferred_element_type=F32)
        st = st * (NA_HEAD_DIM ** -0.5 * LOG2E) + bias_sc[var]
        s_sc[...] = st
        cm_sc[...] = jnp.max(st, axis=0, keepdims=True)

    def finish(blk, s_sc, cm_sc):
        vt = vt_ref[:, pl.ds(win_start(blk), NA_TK)]
        pt = jnp.exp2(s_sc[...] - cm_sc[...])
        acc = jnp.dot(vt, pt.astype(BF16), preferred_element_type=F32)
        o = (acc[:V_HEAD, :] / acc[V_HEAD:V_HEAD + 1, :]).T
        o_ref[pl.ds(pl.multiple_of(blk * NA_TQ, NA_TQ), NA_TQ), :] = o.astype(o_ref.dtype)

    assert NA_BLOCKS % 4 == 0
    scores(0, 0, sa_sc, ma_sc)

    def body(j, carry):
        b = 4 * j
        scores(b + 1, 1, sb_sc, mb_sc)
        finish(b, sa_sc, ma_sc)
        scores(b + 2, 1, sa_sc, ma_sc)
        finish(b + 1, sb_sc, mb_sc)
        scores(b + 3, 1, sb_sc, mb_sc)
        finish(b + 2, sa_sc, ma_sc)
        scores(b + 4, 1, sa_sc, ma_sc)
        finish(b + 3, sb_sc, mb_sc)
        return carry

    lax.fori_loop(0, (NA_BLOCKS - 4) // 4, body, 0)
    scores(last - 2, 1, sb_sc, mb_sc)
    finish(last - 3, sa_sc, ma_sc)
    scores(last - 1, 1, sa_sc, ma_sc)
    finish(last - 2, sb_sc, mb_sc)
    scores(last, 2, sb_sc, mb_sc)
    finish(last - 1, sa_sc, ma_sc)
    finish(last, sb_sc, mb_sc)


def _na_attn(qk, vt, rpb_rows, l):
    s = qk.shape[0]
    return pl.pallas_call(
        _na_attn_kernel,
        grid=(NA_HEADS,),
        in_specs=[pl.BlockSpec((s, NA_HEAD_DIM), lambda h: (0, h)),
                  pl.BlockSpec((s, NA_HEAD_DIM), lambda h: (0, NA_HEADS + h)),
                  pl.BlockSpec((V_ROWS, s), lambda h: (h, 0)),
                  _layer_spec((1, 2 * NA_KH, LANES), l, lambda h: (h, 0, 0))],
        out_specs=pl.BlockSpec((s, NA_HEAD_DIM), lambda h: (0, h)),
        out_shape=jax.ShapeDtypeStruct((s, NA_HEADS * NA_HEAD_DIM), BF16),
        scratch_shapes=[pltpu.VMEM((2, 2 * NA_KH - 1, GRID_W, LANES), F32), pltpu.VMEM((3, NA_TK, NA_TQ), F32),
                        pltpu.VMEM((NA_TK, NA_TQ), F32), pltpu.VMEM((NA_TK, NA_TQ), F32),
                        pltpu.VMEM((2 * F32_SUBLANES, NA_TQ), F32)],
        compiler_params=_params(1),
        name="na_attn",
    )(qk, qk, vt, rpb_rows)


def _merge_kernel(ya_ref, yb_ref, u_ref, x_ref, woa_ref, wob_ref, wo_ref, wg_ref, o_ref):
    g = lax.dot_general(u_ref[...], wg_ref[...], NT_DIMS, preferred_element_type=F32)
    a = jnp.dot(ya_ref[...], woa_ref[...], preferred_element_type=F32)
    b = jnp.dot(yb_ref[...], wob_ref[...], preferred_element_type=F32)
    merged = (jax.nn.sigmoid(g[:, :D_MODEL]) * a + jax.nn.sigmoid(g[:, D_MODEL:]) * b).astype(BF16)
    o_ref[...] = x_ref[...] + jnp.dot(merged, wo_ref[...], preferred_element_type=F32)


def _merge(ya, yb, u, x, woa, wob, wo, wg_t, tm=256):
    s = ya.shape[0]
    row = lambda w: pl.BlockSpec((tm, w), lambda i: (i, 0))
    full = lambda a: pl.BlockSpec(a.shape, lambda i: (0, 0), pipeline_mode=pl.Buffered(1))
    return pl.pallas_call(
        _merge_kernel,
        grid=(s // tm,),
        in_specs=[row(ya.shape[1]), row(yb.shape[1]), row(D_MODEL), row(D_MODEL),
                  full(woa), full(wob), full(wo), full(wg_t)],
        out_specs=row(D_MODEL),
        out_shape=jax.ShapeDtypeStruct((s, D_MODEL), F32),
        compiler_params=_params(1),
        name="merge_out",
    )(ya, yb, u, x, woa, wob, wo, wg_t)


def _ffn_kernel(x_ref, g_ref, w1_ref, w2_ref, gf_ref, o_ref, u_sc, *, final_norm):
    f = pl.program_id(1)

    @pl.when(f == 0)
    def _init():
        x = x_ref[...]
        u_sc[...] = _rms(x, g_ref[...]).astype(BF16)
        o_ref[...] = x

    h = jnp.dot(u_sc[...], w1_ref[...], preferred_element_type=F32)
    a = jnp.square(jnp.maximum(h, 0.0)).astype(BF16)
    o_ref[...] += jnp.dot(a, w2_ref[...], preferred_element_type=F32)

    if final_norm:
        @pl.when(f == pl.num_programs(1) - 1)
        def _final():
            o_ref[...] = _rms(o_ref[...], gf_ref[...])


def _ffn(x, g, w1, w2, l, g_final, final_norm, tm=512, tf=2048):
    s, d = x.shape
    return pl.pallas_call(
        functools.partial(_ffn_kernel, final_norm=final_norm),
        grid=(s // tm, w1.shape[1] // tf),
        in_specs=[pl.BlockSpec((tm, d), lambda i, f: (i, 0)),
                  _layer_spec((1, d), l, lambda i, f: (0, 0)),
                  pl.BlockSpec((d, tf), lambda i, f: (0, f)),
                  pl.BlockSpec((tf, d), lambda i, f: (f, 0)),
                  pl.BlockSpec((1, d), lambda i, f: (0, 0))],
        out_specs=pl.BlockSpec((tm, d), lambda i, f: (i, 0)),
        out_shape=jax.ShapeDtypeStruct((s, d), F32),
        scratch_shapes=[pltpu.VMEM((tm, d), BF16)],
        compiler_params=_params(2),
        name="ffn",
    )(x, g, w1, w2, g_final)


def _rope_tables(s):
    pos = jnp.arange(s, dtype=F32)
    inv_freq = 1.0 / (ROPE_THETA ** (jnp.arange(0, QK_ROPE, 2, dtype=F32) / QK_ROPE))
    ang = pos[:, None] * inv_freq[None, :]
    cos, sin = jnp.cos(ang), jnp.sin(ang)
    half = QK_ROPE // 2
    z = lambda w: jnp.zeros((s, w), F32)
    cos_t = jnp.concatenate([cos, cos, z(LANES - QK_ROPE)], axis=1)
    sl_t = jnp.concatenate([-sin, z(LANES - half)], axis=1)
    sr_t = jnp.concatenate([z(half), sin, z(LANES - QK_ROPE)], axis=1)
    return cos_t, sl_t, sr_t


def kernel(x, norm_mix, w_in, norm_qa, w_uq, norm_kva, w_ukv, rpb, w_o_mla, w_o_na, w_out,
           norm_mlp, w_ff1, w_ff2, norm_final):
    b, s, d = x.shape
    assert (b, s, d) == (1, SEQ, D_MODEL)
    depth = w_in.shape[0]
    cos_t, sl_t, sr_t = _rope_tables(s)

    w_in_t = w_in.transpose(0, 2, 1)
    uq = w_uq.reshape(depth, Q_LORA, MLA_HEADS, QK_NOPE + QK_ROPE)
    w_qn = uq[..., :QK_NOPE].reshape(depth, Q_LORA, MLA_HEADS * QK_NOPE).astype(BF16)
    w_qp = jnp.pad(uq[..., QK_NOPE:], ((0, 0), (0, 0), (0, 0), (0, LANES - QK_ROPE)))
    w_qp = w_qp.reshape(depth, Q_LORA, MLA_HEADS * LANES).astype(BF16)
    ukv = w_ukv.reshape(depth, KV_LORA, MLA_HEADS, QK_NOPE + V_HEAD)
    w_kn = ukv[..., :QK_NOPE].reshape(depth, KV_LORA, MLA_HEADS * QK_NOPE).astype(BF16)
    w_vt = ukv[..., QK_NOPE:].reshape(depth, KV_LORA, MLA_HEADS * V_HEAD).transpose(0, 2, 1).astype(BF16)
    g_qa = norm_qa.reshape(depth, 1, Q_LORA)
    g_kva = norm_kva.reshape(depth, 1, KV_LORA)
    g_mlp = norm_mlp.reshape(depth, 1, d)
    g_mix = norm_mix.reshape(depth, 1, d)
    g_final = norm_final.reshape(1, d)
    rpb_rows = _na_rpb_rows(rpb)

    xs = x.reshape(s, d)
    for l in range(depth):
        u, c = _norm_proj(xs, g_mix, w_in_t, l)
        qk_na = _mm_nt(u, w_in_t, l, ROW_QK, 2 * NA_W, BF16, PROJ_TM, PROJ_TN, name="proj_na_qk")
        vt_na = _proj_vt(u, w_in_t, l)
        q, k, vt = _mla_up(c, g_qa, g_kva, w_qn, w_qp, w_kn, w_vt, l, cos_t, sl_t, sr_t)
        casts = tuple((w, 0, w.shape[1]) for w in (w_o_mla, w_o_na, w_out, w_ff1, w_ff2))
        casts += ((w_in_t, ROW_G, 2 * D_MODEL),)
        y_a, w_oa, w_ob, w_o, w_1, w_2, w_g_t = _mla_attn(q, k, vt, casts, l)
        y_b = _na_attn(qk_na, vt_na, rpb_rows, l)
        xs = _merge(y_a, y_b, u, xs, w_oa, w_ob, w_o, w_g_t)
        xs = _ffn(xs, g_mlp, w_1, w_2, l, g_final, final_norm=(l == depth - 1))
    return xs.reshape(b, s, d)
```
